```python
import jax, jax.numpy as jnp
from jax import lax
import numpy as np

D_MODEL = 1024
BATCH = 2
SEQ = 8192
DEPTH = 2

GRID_W = 64
CTX_LEN = 256
N_HEADS = 8
N_KV_HEADS = 2
GROUP = N_HEADS // N_KV_HEADS
HEAD_DIM = 64
ATTN_WIDTH = N_HEADS * HEAD_DIM
KV_WIDTH = N_KV_HEADS * HEAD_DIM
WINDOW = 128
BLOCK = 128
ATTN_SCALE = HEAD_DIM ** -0.5
ROPE_BASE = 10000.0
ROPE_FREQS = HEAD_DIM // 4
CONV_WIDTH = 512
CONV_KERNEL = 31
CONV_PAD = (CONV_KERNEL - 1) // 2
LRU_WIDTH = 512
LRU_BLOCKS = 8
LRU_BLOCK_DIM = LRU_WIDTH // LRU_BLOCKS
LRU_CONV = 4
LRU_PAD = (2, 1)
LRU_C = 8.0
N_BRANCH = 3
FFN_HIDDEN = -(-8 * D_MODEL // (3 * 256)) * 256
SPLITS = (ATTN_WIDTH, KV_WIDTH, KV_WIDTH, CONV_WIDTH, CONV_WIDTH, LRU_WIDTH, LRU_WIDTH)
IN_COLS = sum(SPLITS) + N_BRANCH * D_MODEL
EPS = 1e-6
NEG_INF = -1e30

kernel_name = 'hybrid_gated_attn_conv_rglru_dit'


def rms_norm(x, g):
    xf = x.astype(jnp.float32)
    y = xf * lax.rsqrt(jnp.mean(xf * xf, axis=-1, keepdims=True) + EPS)
    return (y * g.astype(jnp.float32)).astype(x.dtype)


def layer_norm(x, g, b):
    xf = x.astype(jnp.float32)
    mu = jnp.mean(xf, axis=-1, keepdims=True)
    var = jnp.mean(jnp.square(xf - mu), axis=-1, keepdims=True)
    return ((xf - mu) * lax.rsqrt(var + EPS) * g.astype(jnp.float32) + b.astype(jnp.float32)).astype(x.dtype)


def modulate(h, shift, scale):
    return h * (1 + scale) + shift


def split_proj(z):
    idx = [int(i) for i in np.cumsum(SPLITS)]
    return jnp.split(z, idx, axis=-1)


def axial_rope(rows, dtype):
    row = jnp.repeat(jnp.arange(rows, dtype=jnp.float32), GRID_W)
    col = jnp.tile(jnp.arange(GRID_W, dtype=jnp.float32), rows)
    inv = jnp.power(ROPE_BASE, -jnp.arange(ROPE_FREQS, dtype=jnp.float32) / ROPE_FREQS)
    ang = jnp.concatenate([row[:, None] * inv[None], col[:, None] * inv[None]], axis=-1)
    return jnp.cos(ang).astype(dtype), jnp.sin(ang).astype(dtype)


def apply_rope(x, cos, sin):
    x1, x2 = x[..., :HEAD_DIM // 2], x[..., HEAD_DIM // 2:]
    cos, sin = cos[None, :, None, :], sin[None, :, None, :]
    return jnp.concatenate([x1 * cos - x2 * sin, x1 * sin + x2 * cos], axis=-1)


def softmax_with_sink(s, sink_logit):
    m = jnp.maximum(jnp.max(s, axis=-1, keepdims=True), sink_logit)
    p = jnp.exp(s - m)
    return p / (jnp.sum(p, axis=-1, keepdims=True) + jnp.exp(sink_logit - m))


def band_blocks(t):
    B, S = t.shape[0], t.shape[1]
    nb = S // BLOCK
    tp = jnp.pad(t, ((0, 0), (BLOCK, BLOCK), (0, 0), (0, 0))).reshape(B, nb + 2, BLOCK, *t.shape[2:])
    return jnp.concatenate([tp[:, :-2], tp[:, 1:-1], tp[:, 2:]], axis=2)


def windowed_attention(q, k, v, k_ctx, v_ctx, sink):
    B, S = q.shape[0], q.shape[1]
    nb = S // BLOCK
    qb = q.reshape(B, nb, BLOCK, N_KV_HEADS, GROUP, HEAD_DIM) * ATTN_SCALE
    kb, vb = band_blocks(k), band_blocks(v)
    s_loc = jnp.einsum('bnqhgd,bnjhd->bnhgqj', qb, kb).astype(jnp.float32)
    blk = jnp.arange(nb)[:, None]
    q_pos = blk * BLOCK + jnp.arange(BLOCK)[None, :]
    k_pos = (blk - 1) * BLOCK + jnp.arange(3 * BLOCK)[None, :]
    valid = ((jnp.abs(q_pos[:, :, None] - k_pos[:, None, :]) <= WINDOW)
             & (k_pos >= 0)[:, None, :] & (k_pos < S)[:, None, :])
    s_loc = jnp.where(valid[None, :, None, None], s_loc, NEG_INF)
    s_ctx = jnp.einsum('bnqhgd,bchd->bnhgqc', qb, k_ctx).astype(jnp.float32)
    sink_b = sink.astype(jnp.float32).reshape(1, 1, N_KV_HEADS, GROUP, 1, 1)
    p = softmax_with_sink(jnp.concatenate([s_loc, s_ctx], axis=-1), sink_b).astype(v.dtype)
    out = (jnp.einsum('bnhgqj,bnjhd->bnqhgd', p[..., :3 * BLOCK], vb)
           + jnp.einsum('bnhgqc,bchd->bnqhgd', p[..., 3 * BLOCK:], v_ctx))
    return out.reshape(B, S, ATTN_WIDTH)


def context_attention(q, k, v, sink):
    B, C = q.shape[0], q.shape[1]
    qg = q.reshape(B, C, N_KV_HEADS, GROUP, HEAD_DIM) * ATTN_SCALE
    s = jnp.einsum('bqhgd,bjhd->bhgqj', qg, k).astype(jnp.float32)
    p = softmax_with_sink(s, sink.astype(jnp.float32).reshape(1, N_KV_HEADS, GROUP, 1, 1)).astype(v.dtype)
    return jnp.einsum('bhgqj,bjhd->bqhgd', p, v).reshape(B, C, ATTN_WIDTH)


def depthwise_conv(x, w, b, pad):
    out = lax.conv_general_dilated(x, w[:, None, :], window_strides=(1,), padding=[pad],
                                   dimension_numbers=('NWC', 'WIO', 'NWC'),
                                   feature_group_count=x.shape[-1])
    return out + b


def conformer_conv(val, gate, dw_w, dw_b, ln_g, ln_b):
    h = val * jax.nn.sigmoid(gate)
    h = depthwise_conv(h, dw_w, dw_b, (CONV_PAD, CONV_PAD))
    return jax.nn.silu(layer_norm(h, ln_g, ln_b))


def block_diag(x, w, b):
    xb = x.reshape(*x.shape[:-1], LRU_BLOCKS, LRU_BLOCK_DIM)
    return jnp.einsum('btnd,nde->btne', xb, w).reshape(x.shape) + b


def rglru_coeffs(u, wa, ba, wx, bx, lam):
    r = jax.nn.sigmoid(block_diag(u, wa, ba).astype(jnp.float32))
    i = jax.nn.sigmoid(block_diag(u, wx, bx).astype(jnp.float32))
    log_a = -LRU_C * r * jax.nn.softplus(-lam.astype(jnp.float32))
    a = jnp.exp(log_a)
    b = jnp.sqrt(-jnp.expm1(2.0 * log_a)) * (i * u.astype(jnp.float32))
    return a, b


def _combine(left, right):
    a_l, b_l = left
    a_r, b_r = right
    return a_l * a_r, a_r * b_l + b_r


def linear_scan(a, b, h0, reverse):
    if reverse:
        a, b = jnp.flip(a, axis=1), jnp.flip(b, axis=1)
    if h0 is not None:
        b = b.at[:, 0].add(a[:, 0] * h0)
    _, h = lax.associative_scan(_combine, (a, b), axis=1)
    return jnp.flip(h, axis=1) if reverse else h


def rglru_branch(x_lat, g_lat, x_ctx, g_ctx, need_ctx, conv_w, conv_b, wa, ba, wx, bx, lam):
    u_lat = depthwise_conv(x_lat, conv_w, conv_b, LRU_PAD)
    u_ctx = depthwise_conv(x_ctx, conv_w, conv_b, LRU_PAD)
    lat_dirs, ctx_dirs = [], []
    for d, reverse in enumerate((False, True)):
        a_c, b_c = rglru_coeffs(u_ctx, wa[d], ba[d], wx[d], bx[d], lam[d])
        h_c = linear_scan(a_c, b_c, None, reverse)
        h0 = h_c[:, 0] if reverse else h_c[:, -1]
        a_l, b_l = rglru_coeffs(u_lat, wa[d], ba[d], wx[d], bx[d], lam[d])
        lat_dirs.append(linear_scan(a_l, b_l, h0, reverse))
        ctx_dirs.append(h_c)
    y_lat = (lat_dirs[0] + lat_dirs[1]).astype(x_lat.dtype) * jax.nn.gelu(g_lat)
    y_ctx = (ctx_dirs[0] + ctx_dirs[1]).astype(x_ctx.dtype) * jax.nn.gelu(g_ctx) if need_ctx else None
    return y_lat, y_ctx


def token_mixers(a_lat, a_ctx, cos, sin, need_ctx, w_in, attn_sink, conv_dw_w, conv_dw_b, conv_ln_g,
                 conv_ln_b, lru_conv_w, lru_conv_b, lru_wa, lru_ba, lru_wx, lru_bx, lru_lam,
                 w_o_attn, w_o_conv, w_o_lru, w_out):
    q_l, k_l, v_l, cva_l, cvg_l, lx_l, lg_l, gate_l = split_proj(a_lat @ w_in)
    q_c, k_c, v_c, cva_c, cvg_c, lx_c, lg_c, gate_c = split_proj(a_ctx @ w_in)
    heads = lambda t, n: t.reshape(*t.shape[:-1], n, HEAD_DIM)
    k_ctx, v_ctx = heads(k_c, N_KV_HEADS), heads(v_c, N_KV_HEADS)
    y_attn_l = windowed_attention(apply_rope(heads(q_l, N_HEADS), cos, sin),
                                  apply_rope(heads(k_l, N_KV_HEADS), cos, sin),
                                  heads(v_l, N_KV_HEADS), k_ctx, v_ctx, attn_sink)
    y_conv_l = conformer_conv(cva_l, cvg_l, conv_dw_w, conv_dw_b, conv_ln_g, conv_ln_b)
    y_lru_l, y_lru_c = rglru_branch(lx_l, lg_l, lx_c, lg_c, need_ctx, lru_conv_w, lru_conv_b,
                                    lru_wa, lru_ba, lru_wx, lru_bx, lru_lam)

    def merge(ya, yb, yc, gates):
        ga, gb, gc = jnp.split(jax.nn.sigmoid(gates), N_BRANCH, axis=-1)
        return (ga * (ya @ w_o_attn) + gb * (yb @ w_o_conv) + gc * (yc @ w_o_lru)) @ w_out

    out_lat = merge(y_attn_l, y_conv_l, y_lru_l, gate_l)
    out_ctx = None
    if need_ctx:
        y_attn_c = context_attention(heads(q_c, N_HEADS), k_ctx, v_ctx, attn_sink)
        y_conv_c = conformer_conv(cva_c, cvg_c, conv_dw_w, conv_dw_b, conv_ln_g, conv_ln_b)
        out_ctx = merge(y_attn_c, y_conv_c, y_lru_c, gate_c)
    return out_lat, out_ctx


def swiglu(h, w_up, w_down):
    up, gate = jnp.split(h @ w_up, 2, axis=-1)
    return (jax.nn.silu(gate) * up) @ w_down


def setup_inputs(seed: int = 0) -> dict:
    key = jax.random.key(seed)
    ks = iter(jax.random.split(key, 32))
    L = DEPTH
    nrm = lambda shape, scale: jax.random.normal(next(ks), shape, jnp.float32) * scale
    gain = lambda shape: 1.0 + nrm(shape, 0.02)
    a0 = jax.random.uniform(next(ks), (L, 2, LRU_WIDTH), jnp.float32, minval=0.9, maxval=0.999)
    lam = jnp.log(a0) - jnp.log1p(-a0)
    return {
        'x': nrm((BATCH, SEQ, D_MODEL), 1.0),
        'c': nrm((BATCH, D_MODEL), 1.0),
        'ctx': nrm((BATCH, CTX_LEN, D_MODEL), 1.0),
        'c_ctx': nrm((D_MODEL,), 1.0),
        'mod_w': nrm((L, D_MODEL, 6 * D_MODEL), 0.5 * D_MODEL ** -0.5),
        'mod_b': nrm((L, 6 * D_MODEL), 0.02),
        'norm1_g': gain((L, D_MODEL)),
        'norm2_g': gain((L, D_MODEL)),
        'w_in': nrm((L, D_MODEL, IN_COLS), D_MODEL ** -0.5),
        'attn_sink': nrm((L, N_HEADS), 0.5),
        'conv_dw_w': nrm((L, CONV_KERNEL, CONV_WIDTH), CONV_KERNEL ** -0.5),
        'conv_dw_b': nrm((L, CONV_WIDTH), 0.02),
        'conv_ln_g': gain((L, CONV_WIDTH)),
        'conv_ln_b': nrm((L, CONV_WIDTH), 0.02),
        'lru_conv_w': nrm((L, LRU_CONV, LRU_WIDTH), LRU_CONV ** -0.5),
        'lru_conv_b': nrm((L, LRU_WIDTH), 0.02),
        'lru_wa': nrm((L, 2, LRU_BLOCKS, LRU_BLOCK_DIM, LRU_BLOCK_DIM), LRU_BLOCK_DIM ** -0.5),
        'lru_ba': nrm((L, 2, LRU_WIDTH), 0.02),
        'lru_wx': nrm((L, 2, LRU_BLOCKS, LRU_BLOCK_DIM, LRU_BLOCK_DIM), LRU_BLOCK_DIM ** -0.5),
        'lru_bx': nrm((L, 2, LRU_WIDTH), 0.02),
        'lru_lam': lam,
        'w_o_attn': nrm((L, ATTN_WIDTH, D_MODEL), ATTN_WIDTH ** -0.5),
        'w_o_conv': nrm((L, CONV_WIDTH, D_MODEL), CONV_WIDTH ** -0.5),
        'w_o_lru': nrm((L, LRU_WIDTH, D_MODEL), LRU_WIDTH ** -0.5),
        'w_out': nrm((L, D_MODEL, D_MODEL), D_MODEL ** -0.5),
        'ffn_w_up': nrm((L, D_MODEL, 2 * FFN_HIDDEN), D_MODEL ** -0.5),
        'ffn_w_down': nrm((L, FFN_HIDDEN, D_MODEL), FFN_HIDDEN ** -0.5),
        'final_norm_g': gain((D_MODEL,)),
    }


def reference(x, c, ctx, c_ctx, mod_w, mod_b, norm1_g, norm2_g, w_in, attn_sink, conv_dw_w, conv_dw_b,
              conv_ln_g, conv_ln_b, lru_conv_w, lru_conv_b, lru_wa, lru_ba, lru_wx, lru_bx, lru_lam,
              w_o_attn, w_o_conv, w_o_lru, w_out, ffn_w_up, ffn_w_down, final_norm_g):
    rows = x.shape[1] // GRID_W
    cos, sin = axial_rope(rows, x.dtype)
    silu_c = jax.nn.silu(c)
    silu_cc = jax.nn.silu(c_ctx)
    h_lat, h_ctx = x, ctx
    for l in range(DEPTH):
        need_ctx = l < DEPTH - 1
        sh1, sc1, g1, sh2, sc2, g2 = jnp.split((silu_c @ mod_w[l] + mod_b[l])[:, None, :], 6, axis=-1)
        csh1, csc1, cg1, csh2, csc2, cg2 = jnp.split((silu_cc @ mod_w[l] + mod_b[l])[None, None, :], 6, axis=-1)
        a_lat = modulate(rms_norm(h_lat, norm1_g[l]), sh1, sc1)
        a_ctx = modulate(rms_norm(h_ctx, norm1_g[l]), csh1, csc1)
        m_lat, m_ctx = token_mixers(a_lat, a_ctx, cos, sin, need_ctx, w_in[l], attn_sink[l], conv_dw_w[l],
                                    conv_dw_b[l], conv_ln_g[l], conv_ln_b[l], lru_conv_w[l], lru_conv_b[l],
                                    lru_wa[l], lru_ba[l], lru_wx[l], lru_bx[l], lru_lam[l],
                                    w_o_attn[l], w_o_conv[l], w_o_lru[l], w_out[l])
        h_lat = h_lat + g1 * m_lat
        h_lat = h_lat + g2 * swiglu(modulate(rms_norm(h_lat, norm2_g[l]), sh2, sc2), ffn_w_up[l], ffn_w_down[l])
        if need_ctx:
            h_ctx = h_ctx + cg1 * m_ctx
            h_ctx = h_ctx + cg2 * swiglu(modulate(rms_norm(h_ctx, norm2_g[l]), csh2, csc2),
                                         ffn_w_up[l], ffn_w_down[l])
    return rms_norm(h_lat, final_norm_g)
```

```python
import functools

import jax
import jax.numpy as jnp
import numpy as np
from jax import lax
from jax.experimental import pallas as pl
from jax.experimental.pallas import tpu as pltpu

D_MODEL = 1024
GRID_W = 64
N_HEADS = 8
N_KV_HEADS = 2
GROUP = N_HEADS // N_KV_HEADS
HEAD_DIM = 64
ATTN_WIDTH = N_HEADS * HEAD_DIM
KV_WIDTH = N_KV_HEADS * HEAD_DIM
BLOCK = 128
ATTN_SCALE = HEAD_DIM ** -0.5
ROPE_BASE = 10000.0
ROPE_FREQS = HEAD_DIM // 4
CONV_WIDTH = 512
CONV_KERNEL = 31
CONV_PAD = (CONV_KERNEL - 1) // 2
LRU_WIDTH = 512
LRU_BLOCKS = 8
LRU_BLOCK_DIM = LRU_WIDTH // LRU_BLOCKS
LRU_CONV = 4
LRU_PAD_LEFT = 2
LRU_C = 8.0
FFN_HIDDEN = 2816
MIX_COLS = ATTN_WIDTH + 2 * KV_WIDTH + 2 * CONV_WIDTH + 2 * LRU_WIDTH
EPS = 1e-6
NEG_INF = -1e30

LANES = 128
SUBLANES = 8
VMEM_LIMIT_BYTES = 56 * 1024 * 1024

MOD_ROWS = 8
FFN_CHUNK = 256
CONV_HALO = 16
CONV_ROWS = 32
SCAN_CHUNK = 256
SCAN_SEGS = SUBLANES
SCAN_SEG = SCAN_CHUNK // SCAN_SEGS
SCAN_SEG_PITCH = SCAN_SEG + 8

f32 = jnp.float32
bf16 = jnp.bfloat16


def _mm(a, b):
    return jnp.dot(a, b, preferred_element_type=f32)


def _const_spec(shape):
    n = len(shape)
    return pl.BlockSpec(shape, lambda *_: (0,) * n, pipeline_mode=pl.Buffered(1))


def _params(sem):
    return pltpu.CompilerParams(dimension_semantics=sem, vmem_limit_bytes=VMEM_LIMIT_BYTES)


def _norm_mod(x, g, shift, scale):
    y = x * lax.rsqrt(jnp.mean(x * x, axis=-1, keepdims=True) + EPS)
    return (y * g) * (1.0 + scale) + shift


def _mod_kernel(c_ref, w_ref, b_ref, o_ref):
    c = c_ref[...]
    s = (c * jax.nn.sigmoid(c)).astype(bf16)
    o_ref[0] = _mm(s, w_ref[0].astype(bf16)) + b_ref[0]


def _modulation(cvec, mod_w, mod_b):
    depth, _, cols = mod_w.shape
    tn = 1024
    return pl.pallas_call(
        _mod_kernel,
        grid=(depth, cols // tn),
        in_specs=[
            pl.BlockSpec((MOD_ROWS, D_MODEL), lambda l, j: (0, 0)),
            pl.BlockSpec((1, D_MODEL, tn), lambda l, j: (l, 0, j)),
            pl.BlockSpec((1, 1, tn), lambda l, j: (l, 0, j)),
        ],
        out_specs=pl.BlockSpec((1, MOD_ROWS, tn), lambda l, j: (l, 0, j)),
        out_shape=jax.ShapeDtypeStruct((depth, MOD_ROWS, cols), f32),
        compiler_params=_params(("arbitrary", "arbitrary")),
        name="modulation",
    )(cvec, mod_w, mod_b.reshape(depth, 1, cols))


def _rope(z, cos, sin_signed):
    lane = lax.broadcasted_iota(jnp.int32, z.shape, 1)
    first_half = (lane & (HEAD_DIM - 1)) < HEAD_DIM // 2
    partner = jnp.where(first_half, pltpu.roll(z, LANES - HEAD_DIM // 2, axis=1),
                        pltpu.roll(z, HEAD_DIM // 2, axis=1))
    return z * cos + partner * sin_signed


def _inproj_kernel(rope, tiles_per_seq, fixed_row, h_ref, mod_ref, g_ref, w_ref, *rest):
    if rope:
        cos_ref, sin_ref, q_ref, k_ref, v_ref, hc_ref, lx_ref, gl_ref = rest
    else:
        q_ref, k_ref, v_ref, hc_ref, lx_ref, gl_ref = rest
    row = fixed_row if fixed_row is not None else pl.program_id(0) // tiles_per_seq
    shift = mod_ref[pl.ds(row, 1), 0:D_MODEL]
    scale = mod_ref[pl.ds(row, 1), D_MODEL:2 * D_MODEL]
    a = _norm_mod(h_ref[...], g_ref[...], shift, scale).astype(bf16)

    if rope:
        cos, sin = cos_ref[...], sin_ref[...]
        fix = lambda z: _rope(z, cos, sin)
    else:
        fix = lambda z: z
    zq = _mm(a, w_ref[:, 0:ATTN_WIDTH])
    for j in range(ATTN_WIDTH // LANES):
        zj = fix(zq[:, j * LANES:(j + 1) * LANES]) * ATTN_SCALE
        q_ref[:, j * LANES:(j + 1) * LANES] = zj.astype(bf16)
    c0 = ATTN_WIDTH
    zkv = _mm(a, w_ref[:, c0:c0 + 2 * KV_WIDTH])
    k_ref[...] = fix(zkv[:, 0:KV_WIDTH]).astype(bf16)
    v_ref[...] = zkv[:, KV_WIDTH:].astype(bf16)
    c0 += 2 * KV_WIDTH
    val = _mm(a, w_ref[:, c0:c0 + CONV_WIDTH])
    gate = _mm(a, w_ref[:, c0 + CONV_WIDTH:c0 + 2 * CONV_WIDTH])
    hc_ref[...] = val * jax.nn.sigmoid(gate)
    c0 += 2 * CONV_WIDTH
    lx_ref[...] = _mm(a, w_ref[:, c0:c0 + LRU_WIDTH])
    gl_ref[...] = jax.nn.gelu(_mm(a, w_ref[:, c0 + LRU_WIDTH:c0 + 2 * LRU_WIDTH]))


def _inproj(h, mod, g, w_mix, tm, tiles_per_seq, fixed_row, cos=None, sin=None):
    rows = h.shape[0]
    rope = cos is not None
    row_spec = lambda width: pl.BlockSpec((tm, width), lambda i: (i, 0))
    in_specs = [row_spec(D_MODEL), _const_spec((MOD_ROWS, 6 * D_MODEL)), _const_spec((1, D_MODEL)),
                _const_spec((D_MODEL, MIX_COLS))]
    args = [h, mod, g, w_mix]
    if rope:
        tab = pl.BlockSpec((tm, LANES), lambda i: (i % tiles_per_seq, 0))
        in_specs += [tab, tab]
        args += [cos, sin]
    widths = (ATTN_WIDTH, KV_WIDTH, KV_WIDTH, CONV_WIDTH, LRU_WIDTH, LRU_WIDTH)
    dtypes = (bf16, bf16, bf16, f32, f32, f32)
    return pl.pallas_call(
        functools.partial(_inproj_kernel, rope, tiles_per_seq, fixed_row),
        grid=(rows // tm,),
        in_specs=in_specs,
        out_specs=[row_spec(w) for w in widths],
        out_shape=[jax.ShapeDtypeStruct((rows, w), d) for w, d in zip(widths, dtypes)],
        compiler_params=_params(("arbitrary",)),
        name="inproj_rope" if rope else "inproj_ctx",
    )(*args)


def _attn_kernel(has_local, nblk, sink_ref, q_ref, *rest):
    if has_local:
        kp, ko, kn, vp, vo, vn, kc, vc, o_ref = rest
        k_parts, v_parts = [kp, ko, kn, kc], [vp, vo, vn, vc]
    else:
        kc, vc, o_ref = rest
        k_parts, v_parts = [kc], [vc]
    j = pl.program_id(1)
    q = q_ref[0]
    tq = q.shape[0]
    rows = GROUP * tq
    if has_local:
        qi = lax.broadcasted_iota(jnp.int32, (rows, BLOCK), 0) & (tq - 1)
        kj = lax.broadcasted_iota(jnp.int32, (rows, BLOCK), 1)
        prev_ok = kj + jnp.where(j > 0, 0, -2 * BLOCK) >= qi
        next_ok = kj + jnp.where(j < nblk - 1, 0, 2 * BLOCK) <= qi
    outs = []
    for g in range(N_KV_HEADS):
        hs = slice(g * HEAD_DIM, (g + 1) * HEAD_DIM)
        q4 = jnp.concatenate(
            [q[:, (g * GROUP + r) * HEAD_DIM:(g * GROUP + r + 1) * HEAD_DIM] for r in range(GROUP)], axis=0)
        kcat = jnp.concatenate([p[0][:, hs] for p in k_parts], axis=0)
        vcat = jnp.concatenate([p[0][:, hs] for p in v_parts], axis=0)
        s = lax.dot_general(q4, kcat, (((1,), (1,)), ((), ())), preferred_element_type=f32)
        if has_local:
            s = jnp.concatenate([
                jnp.where(prev_ok, s[:, 0:BLOCK], NEG_INF),
                s[:, BLOCK:2 * BLOCK],
                jnp.where(next_ok, s[:, 2 * BLOCK:3 * BLOCK], NEG_INF),
                s[:, 3 * BLOCK:]], axis=1)
        sink = jnp.concatenate(
            [jnp.full((tq, 1), sink_ref[g * GROUP + r], f32) for r in range(GROUP)], axis=0)
        m = jnp.maximum(jnp.max(s, axis=-1, keepdims=True), sink)
        p = jnp.exp(s - m)
        den = jnp.sum(p, axis=-1, keepdims=True) + jnp.exp(sink - m)
        o = _mm(p.astype(bf16), vcat) / den
        outs += [o[r * tq:(r + 1) * tq] for r in range(GROUP)]
    o_ref[0] = jnp.concatenate(outs, axis=1).astype(bf16)


def _attention(sink, q, k, v, kc, vc):
    bsz, seq, _ = q.shape
    ctx_len = kc.shape[1]
    has_local = k is not None
    tq = BLOCK if has_local else seq
    nblk = seq // tq
    ctx_spec = pl.BlockSpec((1, ctx_len, KV_WIDTH), lambda b, j: (b, 0, 0))
    in_specs = [pl.BlockSpec(memory_space=pltpu.SMEM),
                pl.BlockSpec((1, tq, ATTN_WIDTH), lambda b, j: (b, j, 0))]
    args = [sink, q]
    if has_local:
        prev = pl.BlockSpec((1, BLOCK, KV_WIDTH), lambda b, j: (b, jnp.maximum(j - 1, 0), 0))
        own = pl.BlockSpec((1, BLOCK, KV_WIDTH), lambda b, j: (b, j, 0))
        nxt = pl.BlockSpec((1, BLOCK, KV_WIDTH), lambda b, j: (b, jnp.minimum(j + 1, nblk - 1), 0))
        in_specs += [prev, own, nxt, prev, own, nxt]
        args += [k, k, k, v, v, v]
    in_specs += [ctx_spec, ctx_spec]
    args += [kc, vc]
    return pl.pallas_call(
        functools.partial(_attn_kernel, has_local, nblk),
        grid=(bsz, nblk),
        in_specs=in_specs,
        out_specs=pl.BlockSpec((1, tq, ATTN_WIDTH), lambda b, j: (b, j, 0)),
        out_shape=jax.ShapeDtypeStruct((bsz, seq, ATTN_WIDTH), bf16),
        compiler_params=_params(("arbitrary", "arbitrary")),
        name="attn_window" if has_local else "attn_ctx",
    )(*args)


def _conv_kernel(ntiles, tt, x_ref, prev_ref, next_ref, w_ref, b_ref, g_ref, beta_ref, o_ref, win_ref):
    j = pl.program_id(1)
    zero = jnp.zeros((CONV_HALO, CONV_WIDTH), f32)
    win_ref[CONV_HALO:CONV_HALO + tt] = x_ref[0]

    @pl.when(j > 0)
    def _():
        win_ref[0:CONV_HALO] = prev_ref[0]

    @pl.when(j == 0)
    def _():
        win_ref[0:CONV_HALO] = zero

    @pl.when(j < ntiles - 1)
    def _():
        win_ref[CONV_HALO + tt:] = next_ref[0]

    @pl.when(j == ntiles - 1)
    def _():
        win_ref[CONV_HALO + tt:] = zero

    taps = [w_ref[k:k + 1, :] for k in range(CONV_KERNEL)]
    bias, gain, beta = b_ref[...], g_ref[...], beta_ref[...]
    for r0 in range(0, tt, CONV_ROWS):
        acc = jnp.broadcast_to(bias, (CONV_ROWS, CONV_WIDTH))
        for k in range(CONV_KERNEL):
            off = r0 + CONV_HALO - CONV_PAD + k
            acc = acc + taps[k] * win_ref[off:off + CONV_ROWS, :]
        mu = jnp.mean(acc, axis=-1, keepdims=True)
        xc = acc - mu
        var = jnp.mean(xc * xc, axis=-1, keepdims=True)
        y = xc * lax.rsqrt(var + EPS) * gain + beta
        o_ref[0, r0:r0 + CONV_ROWS, :] = (y * jax.nn.sigmoid(y)).astype(bf16)


def _conformer_conv(hc, w, b, ln_g, ln_b, tt):
    bsz, seq, _ = hc.shape
    ntiles = seq // tt
    hpt = tt // CONV_HALO
    nhalo = seq // CONV_HALO
    row = lambda a: a.reshape(1, CONV_WIDTH)
    return pl.pallas_call(
        functools.partial(_conv_kernel, ntiles, tt),
        grid=(bsz, ntiles),
        in_specs=[
            pl.BlockSpec((1, tt, CONV_WIDTH), lambda bi, j: (bi, j, 0)),
            pl.BlockSpec((1, CONV_HALO, CONV_WIDTH), lambda bi, j: (bi, jnp.maximum(j * hpt - 1, 0), 0)),
            pl.BlockSpec((1, CONV_HALO, CONV_WIDTH),
                         lambda bi, j: (bi, jnp.minimum((j + 1) * hpt, nhalo - 1), 0)),
            _const_spec((CONV_KERNEL, CONV_WIDTH)),
            _const_spec((1, CONV_WIDTH)), _const_spec((1, CONV_WIDTH)), _const_spec((1, CONV_WIDTH)),
        ],
        out_specs=pl.BlockSpec((1, tt, CONV_WIDTH), lambda bi, j: (bi, j, 0)),
        out_shape=jax.ShapeDtypeStruct((bsz, seq, CONV_WIDTH), bf16),
        scratch_shapes=[pltpu.VMEM((tt + 2 * CONV_HALO, CONV_WIDTH), f32)],
        compiler_params=_params(("arbitrary", "arbitrary")),
        name="conformer_conv",
    )(hc, hc, hc, w, row(b), row(ln_g), row(ln_b))


SCAN_PAD = 8


def _lru_kernel(seq, ctx_len, ct,
                xl_ref, gl_ref, xc_ref, gc_ref, cw_ref, cb_ref, wa_ref, ba_ref, wx_ref, bx_ref, lam_ref,
                yl_ref, yc_ref,
                xp_ref, u_ref, hf_ref, a0_ref, b0_ref, h0_ref, p0_ref, a1_ref, b1_ref, h1_ref, p1_ref):
    seg_bufs = ((a0_ref, b0_ref, h0_ref, p0_ref), (a1_ref, b1_ref, h1_ref, p1_ref))
    cw = [cw_ref[k:k + 1, :] for k in range(LRU_CONV)]
    cb = cb_ref[...]
    neg_lam = -lam_ref[...]
    softplus_neg_lam = jnp.maximum(neg_lam, 0.0) + jnp.log1p(jnp.exp(-jnp.abs(neg_lam)))

    def conv_into_u(src_ref, n):
        zero = jnp.zeros((SCAN_PAD, ct), f32)
        xp_ref[0:SCAN_PAD] = zero
        xp_ref[SCAN_PAD + n:2 * SCAN_PAD + n] = zero
        for r0 in range(0, n, SCAN_CHUNK):
            xp_ref[SCAN_PAD + r0:SCAN_PAD + r0 + SCAN_CHUNK] = src_ref[0, r0:r0 + SCAN_CHUNK, :]
        for r0 in range(0, n, SCAN_CHUNK):
            acc = jnp.broadcast_to(cb, (SCAN_CHUNK, ct))
            for k in range(LRU_CONV):
                off = SCAN_PAD + r0 + k - LRU_PAD_LEFT
                acc = acc + cw[k] * xp_ref[off:off + SCAN_CHUNK, :]
            u_ref[r0:r0 + SCAN_CHUNK] = acc

    def coeffs(r0, d):
        uc = u_ref[pl.ds(r0, SCAN_CHUNK)]
        ub = uc.astype(bf16)
        r = jax.nn.sigmoid(_mm(ub, wa_ref[d]) + ba_ref[d:d + 1])
        gate_in = jax.nn.sigmoid(_mm(ub, wx_ref[d]) + bx_ref[d:d + 1])
        log_a = (-LRU_C * r) * softplus_neg_lam[d:d + 1]
        a = jnp.exp(log_a)
        return a, jnp.sqrt(-jnp.tanh(log_a) * (a * a + 1.0)) * (gate_in * uc)

    def stage(d, a, b):
        a_ref, b_ref = seg_bufs[d][0], seg_bufs[d][1]
        for s in range(SCAN_SEGS):
            a_ref[s * SCAN_SEG_PITCH:s * SCAN_SEG_PITCH + SCAN_SEG] = a[s * SCAN_SEG:(s + 1) * SCAN_SEG]
            b_ref[s * SCAN_SEG_PITCH:s * SCAN_SEG_PITCH + SCAN_SEG] = b[s * SCAN_SEG:(s + 1) * SCAN_SEG]

    def seg_step(d, t, hl, pa):
        a_ref, b_ref, h_ref, p_ref = seg_bufs[d]
        rows = pl.ds(t, SCAN_SEGS, stride=SCAN_SEG_PITCH)
        av = a_ref[rows]
        hl = av * hl + b_ref[rows]
        pa = av * pa
        h_ref[rows] = hl
        p_ref[rows] = pa
        return hl, pa

    def finish(d, hl, pa, carry):
        _, _, h_ref, p_ref = seg_bufs[d]
        order = range(SCAN_SEGS) if d == 0 else range(SCAN_SEGS - 1, -1, -1)
        pieces = [None] * SCAN_SEGS
        for s in order:
            lo = s * SCAN_SEG_PITCH
            pieces[s] = h_ref[lo:lo + SCAN_SEG] + p_ref[lo:lo + SCAN_SEG] * carry
            carry = pa[s:s + 1] * carry + hl[s:s + 1]
        return jnp.concatenate(pieces, axis=0), carry

    zeros8 = jnp.zeros((SCAN_SEGS, ct), f32)
    ones8 = jnp.ones((SCAN_SEGS, ct), f32)

    def scan_pair(r0_fwd, r0_rev, carry_f, carry_r):
        stage(0, *coeffs(r0_fwd, 0))
        stage(1, *coeffs(r0_rev, 1))

        def body(t, c):
            hf, pf, hr, pr = c
            hf, pf = seg_step(0, t, hf, pf)
            hr, pr = seg_step(1, SCAN_SEG - 1 - t, hr, pr)
            return hf, pf, hr, pr
        hf, pf, hr, pr = lax.fori_loop(0, SCAN_SEG, body, (zeros8, ones8, zeros8, ones8), unroll=4)
        out_f, carry_f = finish(0, hf, pf, carry_f)
        out_r, carry_r = finish(1, hr, pr, carry_r)
        return out_f, carry_f, out_r, carry_r

    conv_into_u(xc_ref, ctx_len)
    zero_row = jnp.zeros((1, ct), f32)
    assert ctx_len == SCAN_CHUNK
    cf, carry_f, cr, carry_r = scan_pair(0, 0, zero_row, zero_row)
    yc_ref[0] = ((cf + cr) * gc_ref[0]).astype(bf16)

    conv_into_u(xl_ref, seq)
    nchunk = seq // SCAN_CHUNK

    def chunk_body(c, carries):
        carry_f, carry_r = carries
        r0f = pl.multiple_of(c * SCAN_CHUNK, SCAN_CHUNK)
        r0r = pl.multiple_of((nchunk - 1 - c) * SCAN_CHUNK, SCAN_CHUNK)
        out_f, carry_f, out_r, carry_r = scan_pair(r0f, r0r, carry_f, carry_r)
        first_visit = c < nchunk // 2

        @pl.when(first_visit)
        def _():
            hf_ref[pl.ds(r0f, SCAN_CHUNK)] = out_f
            hf_ref[pl.ds(r0r, SCAN_CHUNK)] = out_r

        @pl.when(jnp.logical_not(first_visit))
        def _():
            yl_ref[0, pl.ds(r0f, SCAN_CHUNK), :] = (
                (out_f + hf_ref[pl.ds(r0f, SCAN_CHUNK)]) * gl_ref[0, pl.ds(r0f, SCAN_CHUNK), :]).astype(bf16)
            yl_ref[0, pl.ds(r0r, SCAN_CHUNK), :] = (
                (out_r + hf_ref[pl.ds(r0r, SCAN_CHUNK)]) * gl_ref[0, pl.ds(r0r, SCAN_CHUNK), :]).astype(bf16)
        return carry_f, carry_r

    lax.fori_loop(0, nchunk, chunk_body, (carry_f, carry_r))


def _rglru(xl, gl, xc, gc, conv_w, conv_b, wa_bd, ba, wx_bd, bx, lam, ct):
    bsz, seq, _ = xl.shape
    ctx_len = xc.shape[1]
    seg_rows = SCAN_SEGS * SCAN_SEG_PITCH
    col = lambda rows: pl.BlockSpec((1, rows, ct), lambda b, c: (b, 0, c))
    vec = lambda rows: pl.BlockSpec((rows, ct), lambda b, c: (0, c))
    bd = pl.BlockSpec((2, ct, ct), lambda b, c: (0, c, c))
    return pl.pallas_call(
        functools.partial(_lru_kernel, seq, ctx_len, ct),
        grid=(bsz, LRU_WIDTH // ct),
        in_specs=[col(seq), col(seq), col(ctx_len), col(ctx_len),
                  vec(LRU_CONV), vec(1), bd, vec(2), bd, vec(2), vec(2)],
        out_specs=[col(seq), col(ctx_len)],
        out_shape=[jax.ShapeDtypeStruct((bsz, seq, LRU_WIDTH), bf16),
                   jax.ShapeDtypeStruct((bsz, ctx_len, LRU_WIDTH), bf16)],
        scratch_shapes=[pltpu.VMEM((seq + 2 * SCAN_PAD, ct), f32), pltpu.VMEM((seq, ct), f32),
                        pltpu.VMEM((seq, ct), f32)] + [pltpu.VMEM((seg_rows, ct), f32)] * 8,
        compiler_params=_params(("arbitrary", "arbitrary")),
        name="rglru",
    )(xl, gl, xc, gc, conv_w, conv_b.reshape(1, LRU_WIDTH), wa_bd, ba, wx_bd, bx, lam)


def _tail_kernel(final, tiles_per_seq, fixed_row,
                 h_ref, ya_ref, yb_ref, yc_ref, mod_ref, g1_ref, g2_ref, gf_ref,
                 wg_ref, woa_ref, wob_ref, woc_ref, wout_ref, wup_ref, wdn_ref, o_ref, acc_ref):
    row = fixed_row if fixed_row is not None else pl.program_id(0) // tiles_per_seq
    mod = lambda i: mod_ref[pl.ds(row, 1), i * D_MODEL:(i + 1) * D_MODEL]
    x = h_ref[...]
    a = _norm_mod(x, g1_ref[...], mod(0), mod(1)).astype(bf16)
    merged = None
    for i, (y_ref, w_ref) in enumerate(((ya_ref, woa_ref), (yb_ref, wob_ref), (yc_ref, woc_ref))):
        gate = jax.nn.sigmoid(_mm(a, wg_ref[:, i * D_MODEL:(i + 1) * D_MODEL]))
        term = gate * _mm(y_ref[...], w_ref[...])
        merged = term if merged is None else merged + term
    h1 = x + mod(2) * _mm(merged.astype(bf16), wout_ref[...])
    a2 = _norm_mod(h1, g2_ref[...], mod(3), mod(4)).astype(bf16)

    acc_ref[...] = jnp.zeros_like(acc_ref)

    def ffn_step(c, carry):
        up = _mm(a2, wup_ref[0, c])
        gate = _mm(a2, wup_ref[1, c])
        act = ((gate * jax.nn.sigmoid(gate)) * up).astype(bf16)
        acc_ref[...] += _mm(act, wdn_ref[c])
        return carry
    lax.fori_loop(0, FFN_HIDDEN // FFN_CHUNK, ffn_step, 0)
    h2 = h1 + mod(5) * acc_ref[...]
    if final:
        h2 = h2 * lax.rsqrt(jnp.mean(h2 * h2, axis=-1, keepdims=True) + EPS) * gf_ref[...]
    o_ref[...] = h2


def _tail(h, ya, yb, yc, mod, g1, g2, gf, wts, tm, tiles_per_seq, fixed_row, final):
    rows = h.shape[0]
    nchunk = FFN_HIDDEN // FFN_CHUNK
    row_spec = lambda width: pl.BlockSpec((tm, width), lambda i: (i, 0))
    wg, woa, wob, woc, wout, wup, wdn = wts
    return pl.pallas_call(
        functools.partial(_tail_kernel, final, tiles_per_seq, fixed_row),
        grid=(rows // tm,),
        in_specs=[row_spec(D_MODEL), row_spec(ATTN_WIDTH), row_spec(CONV_WIDTH), row_spec(LRU_WIDTH),
                  _const_spec((MOD_ROWS, 6 * D_MODEL)),
                  _const_spec((1, D_MODEL)), _const_spec((1, D_MODEL)), _const_spec((1, D_MODEL)),
                  _const_spec(wg.shape), _const_spec(woa.shape), _const_spec(wob.shape),
                  _const_spec(woc.shape), _const_spec(wout.shape), _const_spec(wup.shape),
                  _const_spec(wdn.shape)],
        out_specs=row_spec(D_MODEL),
        out_shape=jax.ShapeDtypeStruct((rows, D_MODEL), f32),
        scratch_shapes=[pltpu.VMEM((tm, D_MODEL), f32)],
        compiler_params=_params(("arbitrary",)),
        name="merge_out_swiglu",
    )(h, ya, yb, yc, mod, g1, g2, gf, wg, woa, wob, woc, wout, wup, wdn)


def _rope_tables(seq):
    rows = seq // GRID_W
    row = jnp.repeat(jnp.arange(rows, dtype=f32), GRID_W)
    col = jnp.tile(jnp.arange(GRID_W, dtype=f32), rows)
    inv = jnp.power(ROPE_BASE, -jnp.arange(ROPE_FREQS, dtype=f32) / ROPE_FREQS)
    ang = jnp.concatenate([row[:, None] * inv[None], col[:, None] * inv[None]], axis=-1)
    cos, sin = jnp.cos(ang), jnp.sin(ang)
    reps = LANES // HEAD_DIM
    return jnp.tile(jnp.concatenate([cos, cos], axis=-1), (1, reps)), \
        jnp.tile(jnp.concatenate([-sin, sin], axis=-1), (1, reps))


def _block_diag(w):
    two, nb, d, e = w.shape
    eye = jnp.eye(nb, dtype=w.dtype)
    return jnp.einsum('xnde,nm->xndme', w, eye).reshape(two, nb * d, nb * e)


def kernel(x, c, ctx, c_ctx, mod_w, mod_b, norm1_g, norm2_g, w_in, attn_sink, conv_dw_w, conv_dw_b, conv_ln_g,
           conv_ln_b, lru_conv_w, lru_conv_b, lru_wa, lru_ba, lru_wx, lru_bx, lru_lam, w_o_attn, w_o_conv,
           w_o_lru, w_out, ffn_w_up, ffn_w_down, final_norm_g):
    bsz, seq, _ = x.shape
    ctx_len = ctx.shape[1]
    depth = mod_w.shape[0]
    assert bsz + 1 <= MOD_ROWS and seq % BLOCK == 0 and ctx_len == SCAN_CHUNK
    tm = 512
    tiles_per_seq = seq // tm
    tm_ctx = ctx_len
    ctx_row = bsz
    nchunk = FFN_HIDDEN // FFN_CHUNK

    cvec = jnp.zeros((MOD_ROWS, D_MODEL), f32).at[:bsz].set(c).at[ctx_row].set(c_ctx)
    mod_all = _modulation(cvec, mod_w, mod_b)
    cos, sin = _rope_tables(seq)
    row = lambda v: v.reshape(1, -1)

    h_lat = x.reshape(bsz * seq, D_MODEL)
    h_ctx = ctx.reshape(bsz * ctx_len, D_MODEL)
    for l in range(depth):
        need_ctx = l < depth - 1
        mod = mod_all[l]
        w_mix = w_in[l, :, :MIX_COLS].astype(bf16)
        wts = (
            w_in[l, :, MIX_COLS:].astype(bf16),
            w_o_attn[l].astype(bf16), w_o_conv[l].astype(bf16), w_o_lru[l].astype(bf16),
            w_out[l].astype(bf16),
            ffn_w_up[l].astype(bf16).reshape(D_MODEL, 2, nchunk, FFN_CHUNK).transpose(1, 2, 0, 3),
            ffn_w_down[l].astype(bf16).reshape(nchunk, FFN_CHUNK, D_MODEL),
        )
        g1, g2 = row(norm1_g[l]), row(norm2_g[l])

        q, k, v, hc, lx, gl = _inproj(h_lat, mod, g1, w_mix, tm, tiles_per_seq, None, cos, sin)
        qc, kc, vc, hcc, lxc, glc = _inproj(h_ctx, mod, g1, w_mix, tm_ctx, 1, ctx_row)
        shp = lambda t, n: t.reshape(bsz, n, t.shape[-1])
        kc3, vc3 = shp(kc, ctx_len), shp(vc, ctx_len)

        y_attn = _attention(attn_sink[l], shp(q, seq), shp(k, seq), shp(v, seq), kc3, vc3)
        y_conv = _conformer_conv(shp(hc, seq), conv_dw_w[l], conv_dw_b[l], conv_ln_g[l], conv_ln_b[l], 256)
        y_lru, y_lru_c = _rglru(shp(lx, seq), shp(gl, seq), shp(lxc, ctx_len), shp(glc, ctx_len),
                                lru_conv_w[l], lru_conv_b[l], _block_diag(lru_wa[l]).astype(bf16), lru_ba[l],
                                _block_diag(lru_wx[l]).astype(bf16), lru_bx[l], lru_lam[l], LANES)
        flat = lambda t: t.reshape(-1, t.shape[-1])
        h_lat = _tail(h_lat, flat(y_attn), flat(y_conv), flat(y_lru), mod, g1, g2, row(final_norm_g), wts,
                      tm, tiles_per_seq, None, final=not need_ctx)
        if need_ctx:
            y_attn_c = _attention(attn_sink[l], shp(qc, ctx_len), None, None, kc3, vc3)
            y_conv_c = _conformer_conv(shp(hcc, ctx_len), conv_dw_w[l], conv_dw_b[l], conv_ln_g[l],
                                       conv_ln_b[l], ctx_len)
            h_ctx = _tail(h_ctx, flat(y_attn_c), flat(y_conv_c), flat(y_lru_c), mod, g1, g2,
                          row(final_norm_g), wts, tm_ctx, 1, ctx_row, final=False)
    return h_lat.reshape(bsz, seq, D_MODEL)
```

```python
import functools

import jax
import jax.numpy as jnp
import numpy as np
from jax import lax
from jax.experimental import pallas as pl
from jax.experimental.pallas import tpu as pltpu

D_MODEL = 1024
GRID_W = 64
N_HEADS = 8
N_KV_HEADS = 2
GROUP = N_HEADS // N_KV_HEADS
HEAD_DIM = 64
ATTN_WIDTH = N_HEADS * HEAD_DIM
KV_WIDTH = N_KV_HEADS * HEAD_DIM
BLOCK = 128
ATTN_SCALE = HEAD_DIM ** -0.5
ROPE_BASE = 10000.0
ROPE_FREQS = HEAD_DIM // 4
CONV_WIDTH = 512
CONV_KERNEL = 31
CONV_PAD = (CONV_KERNEL - 1) // 2
LRU_WIDTH = 512
LRU_BLOCKS = 8
LRU_BLOCK_DIM = LRU_WIDTH // LRU_BLOCKS
LRU_CONV = 4
LRU_PAD_LEFT = 2
LRU_C = 8.0
FFN_HIDDEN = 2816
MIX_COLS = ATTN_WIDTH + 2 * KV_WIDTH + 2 * CONV_WIDTH + 2 * LRU_WIDTH
EPS = 1e-6
NEG_INF = -1e30

LANES = 128
SUBLANES = 8
VMEM_LIMIT_BYTES = 56 * 1024 * 1024

MOD_ROWS = 8
FFN_CHUNK = 256
CONV_HALO = 16
CONV_ROWS = 64
SCAN_CHUNK = 256
SCAN_SEGS = SUBLANES
SCAN_SEG = SCAN_CHUNK // SCAN_SEGS
SCAN_SEG_PITCH = SCAN_SEG + 8

f32 = jnp.float32
bf16 = jnp.bfloat16


def _mm(a, b):
    return jnp.dot(a, b, preferred_element_type=f32)


def _const_spec(shape):
    n = len(shape)
    return pl.BlockSpec(shape, lambda *_: (0,) * n, pipeline_mode=pl.Buffered(1))


def _params(sem):
    return pltpu.CompilerParams(dimension_semantics=sem, vmem_limit_bytes=VMEM_LIMIT_BYTES)


def _norm_mod(x, g, shift, scale):
    y = x * lax.rsqrt(jnp.mean(x * x, axis=-1, keepdims=True) + EPS)
    return (y * g) * (1.0 + scale) + shift


def _mod_kernel(c_ref, w_ref, b_ref, o_ref):
    c = c_ref[...]
    s = (c * jax.nn.sigmoid(c)).astype(bf16)
    o_ref[0] = _mm(s, w_ref[0].astype(bf16)) + b_ref[0]


def _modulation(cvec, mod_w, mod_b):
    depth, _, cols = mod_w.shape
    tn = 1024
    return pl.pallas_call(
        _mod_kernel,
        grid=(depth, cols // tn),
        in_specs=[
            pl.BlockSpec((MOD_ROWS, D_MODEL), lambda l, j: (0, 0)),
            pl.BlockSpec((1, D_MODEL, tn), lambda l, j: (l, 0, j)),
            pl.BlockSpec((1, 1, tn), lambda l, j: (l, 0, j)),
        ],
        out_specs=pl.BlockSpec((1, MOD_ROWS, tn), lambda l, j: (l, 0, j)),
        out_shape=jax.ShapeDtypeStruct((depth, MOD_ROWS, cols), f32),
        compiler_params=_params(("arbitrary", "arbitrary")),
        name="modulation",
    )(cvec, mod_w, mod_b.reshape(depth, 1, cols))


def _rope(z, cos, sin_signed):
    lane = lax.broadcasted_iota(jnp.int32, z.shape, 1)
    first_half = (lane & (HEAD_DIM - 1)) < HEAD_DIM // 2
    partner = jnp.where(first_half, pltpu.roll(z, LANES - HEAD_DIM // 2, axis=1),
                        pltpu.roll(z, HEAD_DIM // 2, axis=1))
    return z * cos + partner * sin_signed


def _dup_heads(z):
    low_half = lax.broadcasted_iota(jnp.int32, z.shape, 1) < HEAD_DIM
    swapped = pltpu.roll(z, HEAD_DIM, axis=1)
    return jnp.concatenate([jnp.where(low_half, z, swapped), jnp.where(low_half, swapped, z)], axis=1)


def _inproj_kernel(rope,tiles_per_seq, fixed_row, h_ref, mod_ref, g_ref, w_ref, *rest):
    if rope:
        cos_ref, sin_ref, q_ref, k_ref, v_ref, hc_ref, lx_ref, gl_ref = rest
    else:
        q_ref, k_ref, v_ref, hc_ref, lx_ref, gl_ref = rest
    row = fixed_row if fixed_row is not None else pl.program_id(0) // tiles_per_seq
    shift = mod_ref[pl.ds(row, 1), 0:D_MODEL]
    scale = mod_ref[pl.ds(row, 1), D_MODEL:2 * D_MODEL]
    a = _norm_mod(h_ref[...], g_ref[...], shift, scale).astype(bf16)

    if rope:
        cos, sin = cos_ref[...], sin_ref[...]
        fix = lambda z: _rope(z, cos, sin)
    else:
        fix = lambda z: z
    zq = _mm(a, w_ref[:, 0:ATTN_WIDTH])
    for j in range(ATTN_WIDTH // LANES):
        zj = fix(zq[:, j * LANES:(j + 1) * LANES]) * ATTN_SCALE
        q_ref[:, j * LANES:(j + 1) * LANES] = zj.astype(bf16)
    c0 = ATTN_WIDTH
    zkv = _mm(a, w_ref[:, c0:c0 + 2 * KV_WIDTH])
    k_ref[...] = _dup_heads(fix(zkv[:, 0:KV_WIDTH])).astype(bf16)
    v_ref[...] = _dup_heads(zkv[:, KV_WIDTH:]).astype(bf16)
    c0 += 2 * KV_WIDTH
    val = _mm(a, w_ref[:, c0:c0 + CONV_WIDTH])
    gate = _mm(a, w_ref[:, c0 + CONV_WIDTH:c0 + 2 * CONV_WIDTH])
    hc_ref[...] = val * jax.nn.sigmoid(gate)
    c0 += 2 * CONV_WIDTH
    lx_ref[...] = _mm(a, w_ref[:, c0:c0 + LRU_WIDTH])
    gl_ref[...] = jax.nn.gelu(_mm(a, w_ref[:, c0 + LRU_WIDTH:c0 + 2 * LRU_WIDTH]))


def _inproj(h, mod, g, w_mix, tm, tiles_per_seq, fixed_row, cos=None, sin=None):
    rows = h.shape[0]
    rope = cos is not None
    row_spec = lambda width: pl.BlockSpec((tm, width), lambda i: (i, 0))
    in_specs = [row_spec(D_MODEL), _const_spec((MOD_ROWS, 6 * D_MODEL)), _const_spec((1, D_MODEL)),
                _const_spec((D_MODEL, MIX_COLS))]
    args = [h, mod, g, w_mix]
    if rope:
        tab = pl.BlockSpec((tm, LANES), lambda i: (i % tiles_per_seq, 0))
        in_specs += [tab, tab]
        args += [cos, sin]
    widths = (ATTN_WIDTH, 2 * KV_WIDTH, 2 * KV_WIDTH, CONV_WIDTH, LRU_WIDTH, LRU_WIDTH)
    dtypes = (bf16, bf16, bf16, f32, f32, f32)
    return pl.pallas_call(
        functools.partial(_inproj_kernel, rope, tiles_per_seq, fixed_row),
        grid=(rows // tm,),
        in_specs=in_specs,
        out_specs=[row_spec(w) for w in widths],
        out_shape=[jax.ShapeDtypeStruct((rows, w), d) for w, d in zip(widths, dtypes)],
        compiler_params=_params(("arbitrary",)),
        name="inproj_rope" if rope else "inproj_ctx",
    )(*args)


def _attn_block(sink_ref, q, k_parts, v_parts, prev_ok, next_ok):
    tq = q.shape[0]
    low_half = lax.broadcasted_iota(jnp.int32, (tq, LANES), 1) < HEAD_DIM
    zero = jnp.zeros((tq, LANES), q.dtype)
    outs = []
    for g in range(N_KV_HEADS):
        gs = slice(g * LANES, (g + 1) * LANES)
        pairs = [q[:, (2 * g + i) * LANES:(2 * g + i + 1) * LANES] for i in range(GROUP // 2)]
        q4 = jnp.concatenate(
            [jnp.where(low_half if r % 2 == 0 else ~low_half, pairs[r // 2], zero) for r in range(GROUP)], axis=0)
        kcat = jnp.concatenate([p[:, gs] for p in k_parts], axis=0)
        vcat = jnp.concatenate([p[:, gs] for p in v_parts], axis=0)
        s = lax.dot_general(q4, kcat, (((1,), (1,)), ((), ())), preferred_element_type=f32)
        if prev_ok is not None:
            s = jnp.concatenate([
                jnp.where(prev_ok, s[:, 0:BLOCK], NEG_INF),
                s[:, BLOCK:2 * BLOCK],
                jnp.where(next_ok, s[:, 2 * BLOCK:3 * BLOCK], NEG_INF),
                s[:, 3 * BLOCK:]], axis=1)
        sink = jnp.concatenate(
            [jnp.full((tq, 1), sink_ref[g * GROUP + r], f32) for r in range(GROUP)], axis=0)
        m = jnp.maximum(jnp.max(s, axis=-1, keepdims=True), sink)
        p = jnp.exp(s - m)
        den = jnp.sum(p, axis=-1, keepdims=True) + jnp.exp(sink - m)
        o = _mm(p.astype(bf16), vcat) / den
        for i in range(GROUP // 2):
            outs.append(jnp.where(low_half, o[2 * i * tq:(2 * i + 1) * tq], o[(2 * i + 1) * tq:(2 * i + 2) * tq]))
    return jnp.concatenate(outs, axis=1)


def _attn_window_kernel(nsteps, sink_ref, q_ref, kp, ko, kn, vp, vo, vn, kc, vc, o_ref):
    j = pl.program_id(1)
    rows = GROUP * BLOCK
    qi = lax.broadcasted_iota(jnp.int32, (rows, BLOCK), 0) & (BLOCK - 1)
    kj = lax.broadcasted_iota(jnp.int32, (rows, BLOCK), 1)
    first_prev_ok = kj + jnp.where(j > 0, 0, -2 * BLOCK) >= qi
    last_next_ok = kj + jnp.where(j < nsteps - 1, 0, 2 * BLOCK) <= qi
    k_blocks = [kp[0], ko[0, 0:BLOCK], ko[0, BLOCK:2 * BLOCK], kn[0]]
    v_blocks = [vp[0], vo[0, 0:BLOCK], vo[0, BLOCK:2 * BLOCK], vn[0]]
    for b in range(2):
        out = _attn_block(sink_ref, q_ref[0, b * BLOCK:(b + 1) * BLOCK],
                          k_blocks[b:b + 3] + [kc[0]], v_blocks[b:b + 3] + [vc[0]],
                          first_prev_ok if b == 0 else kj >= qi,
                          kj <= qi if b == 0 else last_next_ok)
        o_ref[0, b * BLOCK:(b + 1) * BLOCK, :] = out.astype(bf16)


def _attn_ctx_kernel(sink_ref, q_ref, kc, vc, o_ref):
    o_ref[0] = _attn_block(sink_ref, q_ref[0], [kc[0]], [vc[0]], None, None).astype(bf16)


def _attention(sink, q, k, v, kc, vc):
    bsz, seq, _ = q.shape
    ctx_len = kc.shape[1]
    kvw = kc.shape[2]
    ctx_spec = pl.BlockSpec((1, ctx_len, kvw), lambda b, j: (b, 0, 0))
    sink_spec = pl.BlockSpec(memory_space=pltpu.SMEM)
    if k is None:
        tq, nsteps = seq, 1
        body = _attn_ctx_kernel
        in_specs, args = [ctx_spec, ctx_spec], [kc, vc]
    else:
        tq = 2 * BLOCK
        nsteps = seq // tq
        nblk = seq // BLOCK
        body = functools.partial(_attn_window_kernel, nsteps)
        prev = pl.BlockSpec((1, BLOCK, kvw), lambda b, j: (b, jnp.maximum(2 * j - 1, 0), 0))
        own = pl.BlockSpec((1, tq, kvw), lambda b, j: (b, j, 0))
        nxt = pl.BlockSpec((1, BLOCK, kvw), lambda b, j: (b, jnp.minimum(2 * j + 2, nblk - 1), 0))
        in_specs = [prev, own, nxt, prev, own, nxt, ctx_spec, ctx_spec]
        args = [k, k, k, v, v, v, kc, vc]
    return pl.pallas_call(
        body,
        grid=(bsz, nsteps),
        in_specs=[sink_spec, pl.BlockSpec((1, tq, ATTN_WIDTH), lambda b, j: (b, j, 0))] + in_specs,
        out_specs=pl.BlockSpec((1, tq, ATTN_WIDTH), lambda b, j: (b, j, 0)),
        out_shape=jax.ShapeDtypeStruct((bsz, seq, ATTN_WIDTH), bf16),
        compiler_params=_params(("arbitrary", "arbitrary")),
        name="attn_ctx" if k is None else "attn_window",
    )(sink, q, *args)


def _conv_kernel(ntiles, tt, x_ref, prev_ref, next_ref, w_ref, b_ref, g_ref, beta_ref, o_ref, win_ref, acc_ref):
    j = pl.program_id(1)
    lane_tiles = [slice(c * LANES, (c + 1) * LANES) for c in range(CONV_WIDTH // LANES)]
    zero = jnp.zeros((CONV_HALO, LANES), f32)
    for c, cs in enumerate(lane_tiles):
        win_ref[c, CONV_HALO:CONV_HALO + tt] = x_ref[0, :, cs]

    @pl.when(j > 0)
    def _():
        for c, cs in enumerate(lane_tiles):
            win_ref[c, 0:CONV_HALO] = prev_ref[0, :, cs]

    @pl.when(j == 0)
    def _():
        for c in range(len(lane_tiles)):
            win_ref[c, 0:CONV_HALO] = zero

    @pl.when(j < ntiles - 1)
    def _():
        for c, cs in enumerate(lane_tiles):
            win_ref[c, CONV_HALO + tt:] = next_ref[0, :, cs]

    @pl.when(j == ntiles - 1)
    def _():
        for c in range(len(lane_tiles)):
            win_ref[c, CONV_HALO + tt:] = zero

    for c, cs in enumerate(lane_tiles):
        def chunk(i, carry, c=c, cs=cs):
            r0 = pl.multiple_of(i * CONV_ROWS, CONV_ROWS)
            acc = jnp.broadcast_to(b_ref[:, cs], (CONV_ROWS, LANES))
            for k in range(CONV_KERNEL):
                tap = jnp.broadcast_to(w_ref[k:k + 1, cs], (CONV_ROWS, LANES))
                acc = acc + tap * win_ref[c, pl.ds(r0 + (CONV_HALO - CONV_PAD + k), CONV_ROWS), :]
            acc_ref[pl.ds(r0, CONV_ROWS), cs] = acc
            return carry
        lax.fori_loop(0, tt // CONV_ROWS, chunk, 0)

    gain, beta = g_ref[...], beta_ref[...]
    for r0 in range(0, tt, CONV_ROWS):
        acc = acc_ref[r0:r0 + CONV_ROWS, :]
        mu = jnp.mean(acc, axis=-1, keepdims=True)
        xc = acc - mu
        var = jnp.mean(xc * xc, axis=-1, keepdims=True)
        y = xc * lax.rsqrt(var + EPS) * gain + beta
        o_ref[0, r0:r0 + CONV_ROWS, :] = (y * jax.nn.sigmoid(y)).astype(bf16)


def _conformer_conv(hc, w, b, ln_g, ln_b, tt):
    bsz, seq, _ = hc.shape
    ntiles = seq // tt
    hpt = tt // CONV_HALO
    nhalo = seq // CONV_HALO
    row = lambda a: a.reshape(1, CONV_WIDTH)
    return pl.pallas_call(
        functools.partial(_conv_kernel, ntiles, tt),
        grid=(bsz, ntiles),
        in_specs=[
            pl.BlockSpec((1, tt, CONV_WIDTH), lambda bi, j: (bi, j, 0)),
            pl.BlockSpec((1, CONV_HALO, CONV_WIDTH), lambda bi, j: (bi, jnp.maximum(j * hpt - 1, 0), 0)),
            pl.BlockSpec((1, CONV_HALO, CONV_WIDTH),
                         lambda bi, j: (bi, jnp.minimum((j + 1) * hpt, nhalo - 1), 0)),
            _const_spec((CONV_KERNEL, CONV_WIDTH)),
            _const_spec((1, CONV_WIDTH)), _const_spec((1, CONV_WIDTH)), _const_spec((1, CONV_WIDTH)),
        ],
        out_specs=pl.BlockSpec((1, tt, CONV_WIDTH), lambda bi, j: (bi, j, 0)),
        out_shape=jax.ShapeDtypeStruct((bsz, seq, CONV_WIDTH), bf16),
        scratch_shapes=[pltpu.VMEM((CONV_WIDTH // LANES, tt + 2 * CONV_HALO, LANES), f32),
                        pltpu.VMEM((tt, CONV_WIDTH), f32)],
        compiler_params=_params(("arbitrary", "arbitrary")),
        name="conformer_conv",
    )(hc, hc, hc, w, row(b), row(ln_g), row(ln_b))


SCAN_PAD = 8


def _lru_kernel(seq, ctx_len, ct,
                xl_ref, gl_ref, xc_ref, gc_ref, cw_ref, cb_ref, wa_ref, ba_ref, wx_ref, bx_ref, lam_ref,
                yl_ref, yc_ref,
                xp_ref, u_ref, hf_ref, a0_ref, b0_ref, h0_ref, p0_ref, a1_ref, b1_ref, h1_ref, p1_ref):
    seg_bufs = ((a0_ref, b0_ref, h0_ref, p0_ref), (a1_ref, b1_ref, h1_ref, p1_ref))
    cw = [cw_ref[k:k + 1, :] for k in range(LRU_CONV)]
    cb = cb_ref[...]
    neg_lam = -lam_ref[...]
    softplus_neg_lam = jnp.maximum(neg_lam, 0.0) + jnp.log1p(jnp.exp(-jnp.abs(neg_lam)))

    def conv_into_u(src_ref, n):
        zero = jnp.zeros((SCAN_PAD, ct), f32)
        xp_ref[0:SCAN_PAD] = zero
        xp_ref[SCAN_PAD + n:2 * SCAN_PAD + n] = zero
        for r0 in range(0, n, SCAN_CHUNK):
            xp_ref[SCAN_PAD + r0:SCAN_PAD + r0 + SCAN_CHUNK] = src_ref[0, r0:r0 + SCAN_CHUNK, :]
        for r0 in range(0, n, SCAN_CHUNK):
            acc = jnp.broadcast_to(cb, (SCAN_CHUNK, ct))
            for k in range(LRU_CONV):
                off = SCAN_PAD + r0 + k - LRU_PAD_LEFT
                acc = acc + cw[k] * xp_ref[off:off + SCAN_CHUNK, :]
            u_ref[r0:r0 + SCAN_CHUNK] = acc

    def coeffs(r0, d):
        uc = u_ref[pl.ds(r0, SCAN_CHUNK)]
        ub = uc.astype(bf16)
        r = jax.nn.sigmoid(_mm(ub, wa_ref[d]) + ba_ref[d:d + 1])
        gate_in = jax.nn.sigmoid(_mm(ub, wx_ref[d]) + bx_ref[d:d + 1])
        log_a = (-LRU_C * r) * softplus_neg_lam[d:d + 1]
        a = jnp.exp(log_a)
        return a, jnp.sqrt(-jnp.tanh(log_a) * (a * a + 1.0)) * (gate_in * uc)

    def stage(d, a, b):
        a_ref, b_ref = seg_bufs[d][0], seg_bufs[d][1]
        for s in range(SCAN_SEGS):
            a_ref[s * SCAN_SEG_PITCH:s * SCAN_SEG_PITCH + SCAN_SEG] = a[s * SCAN_SEG:(s + 1) * SCAN_SEG]
            b_ref[s * SCAN_SEG_PITCH:s * SCAN_SEG_PITCH + SCAN_SEG] = b[s * SCAN_SEG:(s + 1) * SCAN_SEG]

    def seg_step(d, t, hl, pa):
        a_ref, b_ref, h_ref, p_ref = seg_bufs[d]
        rows = pl.ds(t, SCAN_SEGS, stride=SCAN_SEG_PITCH)
        av = a_ref[rows]
        hl = av * hl + b_ref[rows]
        pa = av * pa
        h_ref[rows] = hl
        p_ref[rows] = pa
        return hl, pa

    def finish(d, hl, pa, carry):
        _, _, h_ref, p_ref = seg_bufs[d]
        order = range(SCAN_SEGS) if d == 0 else range(SCAN_SEGS - 1, -1, -1)
        pieces = [None] * SCAN_SEGS
        for s in order:
            lo = s * SCAN_SEG_PITCH
            pieces[s] = h_ref[lo:lo + SCAN_SEG] + p_ref[lo:lo + SCAN_SEG] * carry
            carry = pa[s:s + 1] * carry + hl[s:s + 1]
        return jnp.concatenate(pieces, axis=0), carry

    zeros8 = jnp.zeros((SCAN_SEGS, ct), f32)
    ones8 = jnp.ones((SCAN_SEGS, ct), f32)

    def scan_pair(r0_fwd, r0_rev, carry_f, carry_r):
        stage(0, *coeffs(r0_fwd, 0))
        stage(1, *coeffs(r0_rev, 1))

        def body(t, c):
            hf, pf, hr, pr = c
            hf, pf = seg_step(0, t, hf, pf)
            hr, pr = seg_step(1, SCAN_SEG - 1 - t, hr, pr)
            return hf, pf, hr, pr
        hf, pf, hr, pr = lax.fori_loop(0, SCAN_SEG, body, (zeros8, ones8, zeros8, ones8), unroll=4)
        out_f, carry_f = finish(0, hf, pf, carry_f)
        out_r, carry_r = finish(1, hr, pr, carry_r)
        return out_f, carry_f, out_r, carry_r

    conv_into_u(xc_ref, ctx_len)
    zero_row = jnp.zeros((1, ct), f32)
    assert ctx_len == SCAN_CHUNK
    cf, carry_f, cr, carry_r = scan_pair(0, 0, zero_row, zero_row)
    yc_ref[0] = ((cf + cr) * gc_ref[0]).astype(bf16)

    conv_into_u(xl_ref, seq)
    nchunk = seq // SCAN_CHUNK

    def chunk_body(c, carries):
        carry_f, carry_r = carries
        r0f = pl.multiple_of(c * SCAN_CHUNK, SCAN_CHUNK)
        r0r = pl.multiple_of((nchunk - 1 - c) * SCAN_CHUNK, SCAN_CHUNK)
        out_f, carry_f, out_r, carry_r = scan_pair(r0f, r0r, carry_f, carry_r)
        first_visit = c < nchunk // 2

        @pl.when(first_visit)
        def _():
            hf_ref[pl.ds(r0f, SCAN_CHUNK)] = out_f
            hf_ref[pl.ds(r0r, SCAN_CHUNK)] = out_r

        @pl.when(jnp.logical_not(first_visit))
        def _():
            yl_ref[0, pl.ds(r0f, SCAN_CHUNK), :] = (
                (out_f + hf_ref[pl.ds(r0f, SCAN_CHUNK)]) * gl_ref[0, pl.ds(r0f, SCAN_CHUNK), :]).astype(bf16)
            yl_ref[0, pl.ds(r0r, SCAN_CHUNK), :] = (
                (out_r + hf_ref[pl.ds(r0r, SCAN_CHUNK)]) * gl_ref[0, pl.ds(r0r, SCAN_CHUNK), :]).astype(bf16)
        return carry_f, carry_r

    lax.fori_loop(0, nchunk, chunk_body, (carry_f, carry_r))


def _rglru(xl, gl, xc, gc, conv_w, conv_b, wa_bd, ba, wx_bd, bx, lam, ct):
    bsz, seq, _ = xl.shape
    ctx_len = xc.shape[1]
    seg_rows = SCAN_SEGS * SCAN_SEG_PITCH
    col = lambda rows: pl.BlockSpec((1, rows, ct), lambda b, c: (b, 0, c))
    vec = lambda rows: pl.BlockSpec((rows, ct), lambda b, c: (0, c))
    bd = pl.BlockSpec((2, ct, ct), lambda b, c: (0, c, c))
    return pl.pallas_call(
        functools.partial(_lru_kernel, seq, ctx_len, ct),
        grid=(bsz, LRU_WIDTH // ct),
        in_specs=[col(seq), col(seq), col(ctx_len), col(ctx_len),
                  vec(LRU_CONV), vec(1), bd, vec(2), bd, vec(2), vec(2)],
        out_specs=[col(seq), col(ctx_len)],
        out_shape=[jax.ShapeDtypeStruct((bsz, seq, LRU_WIDTH), bf16),
                   jax.ShapeDtypeStruct((bsz, ctx_len, LRU_WIDTH), bf16)],
        scratch_shapes=[pltpu.VMEM((seq + 2 * SCAN_PAD, ct), f32), pltpu.VMEM((seq, ct), f32),
                        pltpu.VMEM((seq, ct), f32)] + [pltpu.VMEM((seg_rows, ct), f32)] * 8,
        compiler_params=_params(("arbitrary", "arbitrary")),
        name="rglru",
    )(xl, gl, xc, gc, conv_w, conv_b.reshape(1, LRU_WIDTH), wa_bd, ba, wx_bd, bx, lam)


def _tail_kernel(final, tiles_per_seq, fixed_row,
                 h_ref, ya_ref, yb_ref, yc_ref, mod_ref, g1_ref, g2_ref, gf_ref,
                 wg_ref, woa_ref, wob_ref, woc_ref, wout_ref, wup_ref, wdn_ref, o_ref, acc_ref):
    row = fixed_row if fixed_row is not None else pl.program_id(0) // tiles_per_seq
    mod = lambda i: mod_ref[pl.ds(row, 1), i * D_MODEL:(i + 1) * D_MODEL]
    x = h_ref[...]
    a = _norm_mod(x, g1_ref[...], mod(0), mod(1)).astype(bf16)
    merged = None
    for i, (y_ref, w_ref) in enumerate(((ya_ref, woa_ref), (yb_ref, wob_ref), (yc_ref, woc_ref))):
        gate = jax.nn.sigmoid(_mm(a, wg_ref[:, i * D_MODEL:(i + 1) * D_MODEL]))
        term = gate * _mm(y_ref[...], w_ref[...])
        merged = term if merged is None else merged + term
    h1 = x + mod(2) * _mm(merged.astype(bf16), wout_ref[...])
    a2 = _norm_mod(h1, g2_ref[...], mod(3), mod(4)).astype(bf16)

    acc_ref[...] = jnp.zeros_like(acc_ref)

    def ffn_step(c, carry):
        up = _mm(a2, wup_ref[0, c])
        gate = _mm(a2, wup_ref[1, c])
        act = ((gate * jax.nn.sigmoid(gate)) * up).astype(bf16)
        acc_ref[...] += _mm(act, wdn_ref[c])
        return carry
    lax.fori_loop(0, FFN_HIDDEN // FFN_CHUNK, ffn_step, 0)
    h2 = h1 + mod(5) * acc_ref[...]
    if final:
        h2 = h2 * lax.rsqrt(jnp.mean(h2 * h2, axis=-1, keepdims=True) + EPS) * gf_ref[...]
    o_ref[...] = h2


def _tail(h, ya, yb, yc, mod, g1, g2, gf, wts, tm, tiles_per_seq, fixed_row, final):
    rows = h.shape[0]
    nchunk = FFN_HIDDEN // FFN_CHUNK
    row_spec = lambda width: pl.BlockSpec((tm, width), lambda i: (i, 0))
    wg, woa, wob, woc, wout, wup, wdn = wts
    return pl.pallas_call(
        functools.partial(_tail_kernel, final, tiles_per_seq, fixed_row),
        grid=(rows // tm,),
        in_specs=[row_spec(D_MODEL), row_spec(ATTN_WIDTH), row_spec(CONV_WIDTH), row_spec(LRU_WIDTH),
                  _const_spec((MOD_ROWS, 6 * D_MODEL)),
                  _const_spec((1, D_MODEL)), _const_spec((1, D_MODEL)), _const_spec((1, D_MODEL)),
                  _const_spec(wg.shape), _const_spec(woa.shape), _const_spec(wob.shape),
                  _const_spec(woc.shape), _const_spec(wout.shape), _const_spec(wup.shape),
                  _const_spec(wdn.shape)],
        out_specs=row_spec(D_MODEL),
        out_shape=jax.ShapeDtypeStruct((rows, D_MODEL), f32),
        scratch_shapes=[pltpu.VMEM((tm, D_MODEL), f32)],
        compiler_params=_params(("arbitrary",)),
        name="merge_out_swiglu",
    )(h, ya, yb, yc, mod, g1, g2, gf, wg, woa, wob, woc, wout, wup, wdn)


def _rope_tables(seq):
    rows = seq // GRID_W
    row = jnp.repeat(jnp.arange(rows, dtype=f32), GRID_W)
    col = jnp.tile(jnp.arange(GRID_W, dtype=f32), rows)
    inv = jnp.power(ROPE_BASE, -jnp.arange(ROPE_FREQS, dtype=f32) / ROPE_FREQS)
    ang = jnp.concatenate([row[:, None] * inv[None], col[:, None] * inv[None]], axis=-1)
    cos, sin = jnp.cos(ang), jnp.sin(ang)
    reps = LANES // HEAD_DIM
    return jnp.tile(jnp.concatenate([cos, cos], axis=-1), (1, reps)), \
        jnp.tile(jnp.concatenate([-sin, sin], axis=-1), (1, reps))


def _block_diag(w):
    two, nb, d, e = w.shape
    eye = jnp.eye(nb, dtype=w.dtype)
    return jnp.einsum('xnde,nm->xndme', w, eye).reshape(two, nb * d, nb * e)


def kernel(x, c, ctx, c_ctx, mod_w, mod_b, norm1_g, norm2_g, w_in, attn_sink, conv_dw_w, conv_dw_b, conv_ln_g,
           conv_ln_b, lru_conv_w, lru_conv_b, lru_wa, lru_ba, lru_wx, lru_bx, lru_lam, w_o_attn, w_o_conv,
           w_o_lru, w_out, ffn_w_up, ffn_w_down, final_norm_g):
    bsz, seq, _ = x.shape
    ctx_len = ctx.shape[1]
    depth = mod_w.shape[0]
    assert bsz + 1 <= MOD_ROWS and seq % BLOCK == 0 and ctx_len == SCAN_CHUNK
    tm = 512
    tiles_per_seq = seq // tm
    tm_ctx = ctx_len
    ctx_row = bsz
    nchunk = FFN_HIDDEN // FFN_CHUNK

    cvec = jnp.zeros((MOD_ROWS, D_MODEL), f32).at[:bsz].set(c).at[ctx_row].set(c_ctx)
    mod_all = _modulation(cvec, mod_w, mod_b)
    cos, sin = _rope_tables(seq)
    row = lambda v: v.reshape(1, -1)

    h_lat = x.reshape(bsz * seq, D_MODEL)
    h_ctx = ctx.reshape(bsz * ctx_len, D_MODEL)
    for l in range(depth):
        need_ctx = l < depth - 1
        mod = mod_all[l]
        w_mix = w_in[l, :, :MIX_COLS].astype(bf16)
        wts = (
            w_in[l, :, MIX_COLS:].astype(bf16),
            w_o_attn[l].astype(bf16), w_o_conv[l].astype(bf16), w_o_lru[l].astype(bf16),
            w_out[l].astype(bf16),
            ffn_w_up[l].astype(bf16).reshape(D_MODEL, 2, nchunk, FFN_CHUNK).transpose(1, 2, 0, 3),
            ffn_w_down[l].astype(bf16).reshape(nchunk, FFN_CHUNK, D_MODEL),
        )
        g1, g2 = row(norm1_g[l]), row(norm2_g[l])

        q, k, v, hc, lx, gl = _inproj(h_lat, mod, g1, w_mix, tm, tiles_per_seq, None, cos, sin)
        qc, kc, vc, hcc, lxc, glc = _inproj(h_ctx, mod, g1, w_mix, tm_ctx, 1, ctx_row)
        shp = lambda t, n: t.reshape(bsz, n, t.shape[-1])
        kc3, vc3 = shp(kc, ctx_len), shp(vc, ctx_len)

        y_attn = _attention(attn_sink[l], shp(q, seq), shp(k, seq), shp(v, seq), kc3, vc3)
        y_conv = _conformer_conv(shp(hc, seq), conv_dw_w[l], conv_dw_b[l], conv_ln_g[l], conv_ln_b[l], 256)
        y_lru, y_lru_c = _rglru(shp(lx, seq), shp(gl, seq), shp(lxc, ctx_len), shp(glc, ctx_len),
                                lru_conv_w[l], lru_conv_b[l], _block_diag(lru_wa[l]).astype(bf16), lru_ba[l],
                                _block_diag(lru_wx[l]).astype(bf16), lru_bx[l], lru_lam[l], LANES)
        flat = lambda t: t.reshape(-1, t.shape[-1])
        h_lat = _tail(h_lat, flat(y_attn), flat(y_conv), flat(y_lru), mod, g1, g2, row(final_norm_g), wts,
                      tm, tiles_per_seq, None, final=not need_ctx)
        if need_ctx:
            y_attn_c = _attention(attn_sink[l], shp(qc, ctx_len), None, None, kc3, vc3)
            y_conv_c = _conformer_conv(shp(hcc, ctx_len), conv_dw_w[l], conv_dw_b[l], conv_ln_g[l],
                                       conv_ln_b[l], ctx_len)
            h_ctx = _tail(h_ctx, flat(y_attn_c), flat(y_conv_c), flat(y_lru_c), mod, g1, g2,
                          row(final_norm_g), wts, tm_ctx, 1, ctx_row, final=False)
    return h_lat.reshape(bsz, seq, D_MODEL)
```

```python
import functools

import jax
import jax.numpy as jnp
import numpy as np
from jax import lax
from jax.experimental import pallas as pl
from jax.experimental.pallas import tpu as pltpu

D_MODEL = 1024
GRID_W = 64
N_HEADS = 8
N_KV_HEADS = 2
GROUP = N_HEADS // N_KV_HEADS
HEAD_DIM = 64
ATTN_WIDTH = N_HEADS * HEAD_DIM
KV_WIDTH = N_KV_HEADS * HEAD_DIM
BLOCK = 128
ATTN_SCALE = HEAD_DIM ** -0.5
ROPE_BASE = 10000.0
ROPE_FREQS = HEAD_DIM // 4
CONV_WIDTH = 512
CONV_KERNEL = 31
CONV_PAD = (CONV_KERNEL - 1) // 2
LRU_WIDTH = 512
LRU_BLOCKS = 8
LRU_BLOCK_DIM = LRU_WIDTH // LRU_BLOCKS
LRU_CONV = 4
LRU_PAD_LEFT = 2
LRU_C = 8.0
FFN_HIDDEN = 2816
MIX_COLS = ATTN_WIDTH + 2 * KV_WIDTH + 2 * CONV_WIDTH + 2 * LRU_WIDTH
EPS = 1e-6
NEG_INF = -1e30

LANES = 128
SUBLANES = 8
VMEM_LIMIT_BYTES = 56 * 1024 * 1024

MOD_ROWS = 8
FFN_CHUNK = 256
CONV_HALO = 16
CONV_ROWS = 64
SCAN_CHUNK = 256
SCAN_SEGS = SUBLANES
SCAN_SEG = SCAN_CHUNK // SCAN_SEGS
SCAN_SEG_PITCH = SCAN_SEG + 8

f32 = jnp.float32
bf16 = jnp.bfloat16


def _mm(a, b):
    return jnp.dot(a, b, preferred_element_type=f32)


def _const_spec(shape):
    n = len(shape)
    return pl.BlockSpec(shape, lambda *_: (0,) * n, pipeline_mode=pl.Buffered(1))


def _params(sem):
    return pltpu.CompilerParams(dimension_semantics=sem, vmem_limit_bytes=VMEM_LIMIT_BYTES)


def _norm_mod(x, g, shift, scale):
    y = x * lax.rsqrt(jnp.mean(x * x, axis=-1, keepdims=True) + EPS)
    return (y * g) * (1.0 + scale) + shift


def _mod_kernel(c_ref, w_ref, b_ref, o_ref):
    c = c_ref[...]
    s = (c * jax.nn.sigmoid(c)).astype(bf16)
    o_ref[0] = _mm(s, w_ref[0].astype(bf16)) + b_ref[0]


def _modulation(cvec, mod_w, mod_b):
    depth, _, cols = mod_w.shape
    tn = 1024
    return pl.pallas_call(
        _mod_kernel,
        grid=(depth, cols // tn),
        in_specs=[
            pl.BlockSpec((MOD_ROWS, D_MODEL), lambda l, j: (0, 0)),
            pl.BlockSpec((1, D_MODEL, tn), lambda l, j: (l, 0, j)),
            pl.BlockSpec((1, 1, tn), lambda l, j: (l, 0, j)),
        ],
        out_specs=pl.BlockSpec((1, MOD_ROWS, tn), lambda l, j: (l, 0, j)),
        out_shape=jax.ShapeDtypeStruct((depth, MOD_ROWS, cols), f32),
        compiler_params=_params(("arbitrary", "arbitrary")),
        name="modulation",
    )(cvec, mod_w, mod_b.reshape(depth, 1, cols))


def _rope(z, cos, sin_signed):
    lane = lax.broadcasted_iota(jnp.int32, z.shape, 1)
    first_half = (lane & (HEAD_DIM - 1)) < HEAD_DIM // 2
    partner = jnp.where(first_half, pltpu.roll(z, LANES - HEAD_DIM // 2, axis=1),
                        pltpu.roll(z, HEAD_DIM // 2, axis=1))
    return z * cos + partner * sin_signed


def _dup_heads(z):
    low_half = lax.broadcasted_iota(jnp.int32, z.shape, 1) < HEAD_DIM
    swapped = pltpu.roll(z, HEAD_DIM, axis=1)
    return jnp.concatenate([jnp.where(low_half, z, swapped), jnp.where(low_half, swapped, z)], axis=1)


def _inproj_kernel(rope,tiles_per_seq, fixed_row, h_ref, mod_ref, g_ref, w_ref, *rest):
    if rope:
        cos_ref, sin_ref, q_ref, k_ref, v_ref, hc_ref, lx_ref, gl_ref = rest
    else:
        q_ref, k_ref, v_ref, hc_ref, lx_ref, gl_ref = rest
    row = fixed_row if fixed_row is not None else pl.program_id(0) // tiles_per_seq
    shift = mod_ref[pl.ds(row, 1), 0:D_MODEL]
    scale = mod_ref[pl.ds(row, 1), D_MODEL:2 * D_MODEL]
    a = _norm_mod(h_ref[...], g_ref[...], shift, scale).astype(bf16)

    if rope:
        cos, sin = cos_ref[...], sin_ref[...]
        fix = lambda z: _rope(z, cos, sin)
    else:
        fix = lambda z: z
    zq = _mm(a, w_ref[:, 0:ATTN_WIDTH])
    for j in range(ATTN_WIDTH // LANES):
        zj = fix(zq[:, j * LANES:(j + 1) * LANES]) * ATTN_SCALE
        q_ref[:, j * LANES:(j + 1) * LANES] = zj.astype(bf16)
    c0 = ATTN_WIDTH
    zkv = _mm(a, w_ref[:, c0:c0 + 2 * KV_WIDTH])
    k_ref[...] = _dup_heads(fix(zkv[:, 0:KV_WIDTH])).astype(bf16)
    v_ref[...] = _dup_heads(zkv[:, KV_WIDTH:]).astype(bf16)
    c0 += 2 * KV_WIDTH
    val = _mm(a, w_ref[:, c0:c0 + CONV_WIDTH])
    gate = _mm(a, w_ref[:, c0 + CONV_WIDTH:c0 + 2 * CONV_WIDTH])
    hc_ref[...] = val * jax.nn.sigmoid(gate)
    c0 += 2 * CONV_WIDTH
    lx_ref[...] = _mm(a, w_ref[:, c0:c0 + LRU_WIDTH])
    gl_ref[...] = jax.nn.gelu(_mm(a, w_ref[:, c0 + LRU_WIDTH:c0 + 2 * LRU_WIDTH]))


def _inproj(h, mod, g, w_mix, tm, tiles_per_seq, fixed_row, cos=None, sin=None):
    rows = h.shape[0]
    rope = cos is not None
    row_spec = lambda width: pl.BlockSpec((tm, width), lambda i: (i, 0))
    in_specs = [row_spec(D_MODEL), _const_spec((MOD_ROWS, 6 * D_MODEL)), _const_spec((1, D_MODEL)),
                _const_spec((D_MODEL, MIX_COLS))]
    args = [h, mod, g, w_mix]
    if rope:
        tab = pl.BlockSpec((tm, LANES), lambda i: (i % tiles_per_seq, 0))
        in_specs += [tab, tab]
        args += [cos, sin]
    widths = (ATTN_WIDTH, 2 * KV_WIDTH, 2 * KV_WIDTH, CONV_WIDTH, LRU_WIDTH, LRU_WIDTH)
    dtypes = (bf16, bf16, bf16, f32, f32, f32)
    return pl.pallas_call(
        functools.partial(_inproj_kernel, rope, tiles_per_seq, fixed_row),
        grid=(rows // tm,),
        in_specs=in_specs,
        out_specs=[row_spec(w) for w in widths],
        out_shape=[jax.ShapeDtypeStruct((rows, w), d) for w, d in zip(widths, dtypes)],
        compiler_params=_params(("arbitrary",)),
        name="inproj_rope" if rope else "inproj_ctx",
    )(*args)


def _attn_block(sink_ref, q, k_parts, v_parts, prev_ok, next_ok):
    tq = q.shape[0]
    low_half = lax.broadcasted_iota(jnp.int32, (tq, LANES), 1) < HEAD_DIM
    zero = jnp.zeros((tq, LANES), q.dtype)
    outs = []
    for g in range(N_KV_HEADS):
        gs = slice(g * LANES, (g + 1) * LANES)
        pairs = [q[:, (2 * g + i) * LANES:(2 * g + i + 1) * LANES] for i in range(GROUP // 2)]
        q4 = jnp.concatenate(
            [jnp.where(low_half if r % 2 == 0 else ~low_half, pairs[r // 2], zero) for r in range(GROUP)], axis=0)
        kcat = jnp.concatenate([p[:, gs] for p in k_parts], axis=0)
        vcat = jnp.concatenate([p[:, gs] for p in v_parts], axis=0)
        s = lax.dot_general(q4, kcat, (((1,), (1,)), ((), ())), preferred_element_type=f32)
        if prev_ok is not None:
            s = jnp.concatenate([
                jnp.where(prev_ok, s[:, 0:BLOCK], NEG_INF),
                s[:, BLOCK:2 * BLOCK],
                jnp.where(next_ok, s[:, 2 * BLOCK:3 * BLOCK], NEG_INF),
                s[:, 3 * BLOCK:]], axis=1)
        sink = jnp.concatenate(
            [jnp.full((tq, 1), sink_ref[g * GROUP + r], f32) for r in range(GROUP)], axis=0)
        m = jnp.maximum(jnp.max(s, axis=-1, keepdims=True), sink)
        p = jnp.exp(s - m)
        den = jnp.sum(p, axis=-1, keepdims=True) + jnp.exp(sink - m)
        o = _mm(p.astype(bf16), vcat) / den
        for i in range(GROUP // 2):
            outs.append(jnp.where(low_half, o[2 * i * tq:(2 * i + 1) * tq], o[(2 * i + 1) * tq:(2 * i + 2) * tq]))
    return jnp.concatenate(outs, axis=1)


def _attn_window_kernel(nsteps, sink_ref, q_ref, kp, ko, kn, vp, vo, vn, kc, vc, o_ref):
    j = pl.program_id(1)
    rows = GROUP * BLOCK
    qi = lax.broadcasted_iota(jnp.int32, (rows, BLOCK), 0) & (BLOCK - 1)
    kj = lax.broadcasted_iota(jnp.int32, (rows, BLOCK), 1)
    first_prev_ok = kj + jnp.where(j > 0, 0, -2 * BLOCK) >= qi
    last_next_ok = kj + jnp.where(j < nsteps - 1, 0, 2 * BLOCK) <= qi
    k_blocks = [kp[0], ko[0, 0:BLOCK], ko[0, BLOCK:2 * BLOCK], kn[0]]
    v_blocks = [vp[0], vo[0, 0:BLOCK], vo[0, BLOCK:2 * BLOCK], vn[0]]
    for b in range(2):
        out = _attn_block(sink_ref, q_ref[0, b * BLOCK:(b + 1) * BLOCK],
                          k_blocks[b:b + 3] + [kc[0]], v_blocks[b:b + 3] + [vc[0]],
                          first_prev_ok if b == 0 else kj >= qi,
                          kj <= qi if b == 0 else last_next_ok)
        o_ref[0, b * BLOCK:(b + 1) * BLOCK, :] = out.astype(bf16)


def _attn_ctx_kernel(sink_ref, q_ref, kc, vc, o_ref):
    o_ref[0] = _attn_block(sink_ref, q_ref[0], [kc[0]], [vc[0]], None, None).astype(bf16)


def _attention(sink, q, k, v, kc, vc):
    bsz, seq, _ = q.shape
    ctx_len = kc.shape[1]
    kvw = kc.shape[2]
    ctx_spec = pl.BlockSpec((1, ctx_len, kvw), lambda b, j: (b, 0, 0))
    sink_spec = pl.BlockSpec(memory_space=pltpu.SMEM)
    if k is None:
        tq, nsteps = seq, 1
        body = _attn_ctx_kernel
        in_specs, args = [ctx_spec, ctx_spec], [kc, vc]
    else:
        tq = 2 * BLOCK
        nsteps = seq // tq
        nblk = seq // BLOCK
        body = functools.partial(_attn_window_kernel, nsteps)
        prev = pl.BlockSpec((1, BLOCK, kvw), lambda b, j: (b, jnp.maximum(2 * j - 1, 0), 0))
        own = pl.BlockSpec((1, tq, kvw), lambda b, j: (b, j, 0))
        nxt = pl.BlockSpec((1, BLOCK, kvw), lambda b, j: (b, jnp.minimum(2 * j + 2, nblk - 1), 0))
        in_specs = [prev, own, nxt, prev, own, nxt, ctx_spec, ctx_spec]
        args = [k, k, k, v, v, v, kc, vc]
    return pl.pallas_call(
        body,
        grid=(bsz, nsteps),
        in_specs=[sink_spec, pl.BlockSpec((1, tq, ATTN_WIDTH), lambda b, j: (b, j, 0))] + in_specs,
        out_specs=pl.BlockSpec((1, tq, ATTN_WIDTH), lambda b, j: (b, j, 0)),
        out_shape=jax.ShapeDtypeStruct((bsz, seq, ATTN_WIDTH), bf16),
        compiler_params=_params(("arbitrary", "arbitrary")),
        name="attn_ctx" if k is None else "attn_window",
    )(sink, q, *args)


def _conv_kernel(ntiles, tt, x_ref, prev_ref, next_ref, w_ref, b_ref, g_ref, beta_ref, o_ref, win_ref, acc_ref):
    j = pl.program_id(1)
    lane_tiles = [slice(c * LANES, (c + 1) * LANES) for c in range(CONV_WIDTH // LANES)]
    zero = jnp.zeros((CONV_HALO, LANES), f32)
    for c, cs in enumerate(lane_tiles):
        win_ref[c, CONV_HALO:CONV_HALO + tt] = x_ref[0, :, cs]

    @pl.when(j > 0)
    def _():
        for c, cs in enumerate(lane_tiles):
            win_ref[c, 0:CONV_HALO] = prev_ref[0, :, cs]

    @pl.when(j == 0)
    def _():
        for c in range(len(lane_tiles)):
            win_ref[c, 0:CONV_HALO] = zero

    @pl.when(j < ntiles - 1)
    def _():
        for c, cs in enumerate(lane_tiles):
            win_ref[c, CONV_HALO + tt:] = next_ref[0, :, cs]

    @pl.when(j == ntiles - 1)
    def _():
        for c in range(len(lane_tiles)):
            win_ref[c, CONV_HALO + tt:] = zero

    for c, cs in enumerate(lane_tiles):
        def chunk(i, carry, c=c, cs=cs):
            r0 = pl.multiple_of(i * CONV_ROWS, CONV_ROWS)
            acc = jnp.broadcast_to(b_ref[:, cs], (CONV_ROWS, LANES))
            for k in range(CONV_KERNEL):
                tap = jnp.broadcast_to(w_ref[k:k + 1, cs], (CONV_ROWS, LANES))
                acc = acc + tap * win_ref[c, pl.ds(r0 + (CONV_HALO - CONV_PAD + k), CONV_ROWS), :]
            acc_ref[pl.ds(r0, CONV_ROWS), cs] = acc
            return carry
        lax.fori_loop(0, tt // CONV_ROWS, chunk, 0)

    gain, beta = g_ref[...], beta_ref[...]
    for r0 in range(0, tt, CONV_ROWS):
        acc = acc_ref[r0:r0 + CONV_ROWS, :]
        mu = jnp.mean(acc, axis=-1, keepdims=True)
        xc = acc - mu
        var = jnp.mean(xc * xc, axis=-1, keepdims=True)
        y = xc * lax.rsqrt(var + EPS) * gain + beta
        o_ref[0, r0:r0 + CONV_ROWS, :] = (y * jax.nn.sigmoid(y)).astype(bf16)


def _conformer_conv(hc, w, b, ln_g, ln_b, tt):
    bsz, seq, _ = hc.shape
    ntiles = seq // tt
    hpt = tt // CONV_HALO
    nhalo = seq // CONV_HALO
    row = lambda a: a.reshape(1, CONV_WIDTH)
    return pl.pallas_call(
        functools.partial(_conv_kernel, ntiles, tt),
        grid=(bsz, ntiles),
        in_specs=[
            pl.BlockSpec((1, tt, CONV_WIDTH), lambda bi, j: (bi, j, 0)),
            pl.BlockSpec((1, CONV_HALO, CONV_WIDTH), lambda bi, j: (bi, jnp.maximum(j * hpt - 1, 0), 0)),
            pl.BlockSpec((1, CONV_HALO, CONV_WIDTH),
                         lambda bi, j: (bi, jnp.minimum((j + 1) * hpt, nhalo - 1), 0)),
            _const_spec((CONV_KERNEL, CONV_WIDTH)),
            _const_spec((1, CONV_WIDTH)), _const_spec((1, CONV_WIDTH)), _const_spec((1, CONV_WIDTH)),
        ],
        out_specs=pl.BlockSpec((1, tt, CONV_WIDTH), lambda bi, j: (bi, j, 0)),
        out_shape=jax.ShapeDtypeStruct((bsz, seq, CONV_WIDTH), bf16),
        scratch_shapes=[pltpu.VMEM((CONV_WIDTH // LANES, tt + 2 * CONV_HALO, LANES), f32),
                        pltpu.VMEM((tt, CONV_WIDTH), f32)],
        compiler_params=_params(("arbitrary", "arbitrary")),
        name="conformer_conv",
    )(hc, hc, hc, w, row(b), row(ln_g), row(ln_b))


SCAN_PAD = 8


def _lru_kernel(seq, ctx_len, ct,
                xl_ref, gl_ref, xc_ref, gc_ref, cw_ref, cb_ref, wa_ref, ba_ref, wx_ref, bx_ref, lam_ref,
                yl_ref, yc_ref,
                xp_ref, u_ref, hf_ref, seg_ref):
    cw = [cw_ref[k:k + 1, :] for k in range(LRU_CONV)]
    cb = cb_ref[...]
    neg_lam = -lam_ref[...]
    softplus_neg_lam = jnp.maximum(neg_lam, 0.0) + jnp.log1p(jnp.exp(-jnp.abs(neg_lam)))

    def conv_into_u(src_ref, n):
        zero = jnp.zeros((SCAN_PAD, ct), f32)
        xp_ref[0:SCAN_PAD] = zero
        xp_ref[SCAN_PAD + n:2 * SCAN_PAD + n] = zero
        for r0 in range(0, n, SCAN_CHUNK):
            xp_ref[SCAN_PAD + r0:SCAN_PAD + r0 + SCAN_CHUNK] = src_ref[0, r0:r0 + SCAN_CHUNK, :]
        for r0 in range(0, n, SCAN_CHUNK):
            acc = jnp.broadcast_to(cb, (SCAN_CHUNK, ct))
            for k in range(LRU_CONV):
                off = SCAN_PAD + r0 + k - LRU_PAD_LEFT
                acc = acc + cw[k] * xp_ref[off:off + SCAN_CHUNK, :]
            u_ref[r0:r0 + SCAN_CHUNK] = acc

    half_log_a_scale = (-0.5 * LRU_C) * softplus_neg_lam

    def stage(buf, r0, d):
        uc = u_ref[pl.ds(r0, SCAN_CHUNK)]
        ub = uc.astype(bf16)
        tanh_r = jnp.tanh(0.5 * (_mm(ub, wa_ref[d]) + ba_ref[d:d + 1]))
        tanh_i = jnp.tanh(0.5 * (_mm(ub, wx_ref[d]) + bx_ref[d:d + 1]))
        log_a = (tanh_r + 1.0) * half_log_a_scale[d:d + 1]
        a = jnp.exp(log_a)
        b = jnp.sqrt(-jnp.tanh(log_a) * (a * a + 1.0)) * ((0.5 * tanh_i + 0.5) * uc)
        for s in range(SCAN_SEGS):
            lo = s * SCAN_SEG_PITCH
            seg_ref[buf, d, 0, lo:lo + SCAN_SEG] = a[s * SCAN_SEG:(s + 1) * SCAN_SEG]
            seg_ref[buf, d, 1, lo:lo + SCAN_SEG] = b[s * SCAN_SEG:(s + 1) * SCAN_SEG]

    def scan(buf, d, carry):
        hl = jnp.zeros((SCAN_SEGS, ct), f32)
        pa = jnp.ones((SCAN_SEGS, ct), f32)
        for i in range(SCAN_SEG):
            t = i if d == 0 else SCAN_SEG - 1 - i
            rows = pl.ds(t, SCAN_SEGS, stride=SCAN_SEG_PITCH)
            av = seg_ref[buf, d, 0, rows]
            hl = av * hl + seg_ref[buf, d, 1, rows]
            pa = av * pa
            seg_ref[buf, d, 2, rows] = hl
            seg_ref[buf, d, 3, rows] = pa
        order = range(SCAN_SEGS) if d == 0 else range(SCAN_SEGS - 1, -1, -1)
        pieces = [None] * SCAN_SEGS
        for s in order:
            lo = s * SCAN_SEG_PITCH
            pieces[s] = seg_ref[buf, d, 2, lo:lo + SCAN_SEG] + seg_ref[buf, d, 3, lo:lo + SCAN_SEG] * carry
            carry = pa[s:s + 1] * carry + hl[s:s + 1]
        return jnp.concatenate(pieces, axis=0), carry

    conv_into_u(xc_ref, ctx_len)
    zero_row = jnp.zeros((1, ct), f32)
    assert ctx_len == SCAN_CHUNK
    stage(0, 0, 0)
    stage(0, 0, 1)
    cf, carry_f = scan(0, 0, zero_row)
    cr, carry_r = scan(0, 1, zero_row)
    yc_ref[0] = ((cf + cr) * gc_ref[0]).astype(bf16)

    conv_into_u(xl_ref, seq)
    nchunk = seq // SCAN_CHUNK
    fwd_start = lambda c: pl.multiple_of(c * SCAN_CHUNK, SCAN_CHUNK)
    rev_start = lambda c: pl.multiple_of((nchunk - 1 - c) * SCAN_CHUNK, SCAN_CHUNK)
    stage(0, fwd_start(0), 0)
    stage(0, rev_start(0), 1)

    assert nchunk % 4 == 0

    def emit(first_visit, r0, states):
        if first_visit:
            hf_ref[pl.ds(r0, SCAN_CHUNK)] = states
        else:
            yl_ref[0, pl.ds(r0, SCAN_CHUNK), :] = (
                (states + hf_ref[pl.ds(r0, SCAN_CHUNK)]) * gl_ref[0, pl.ds(r0, SCAN_CHUNK), :]).astype(bf16)

    def pair_body(first_visit, it, carries):
        carry_f, carry_r = carries
        for buf in range(2):
            c = 2 * it + buf
            nxt = jnp.minimum(c + 1, nchunk - 1)
            stage(1 - buf, fwd_start(nxt), 0)
            stage(1 - buf, rev_start(nxt), 1)
            out_f, carry_f = scan(buf, 0, carry_f)
            out_r, carry_r = scan(buf, 1, carry_r)
            emit(first_visit, fwd_start(c), out_f)
            emit(first_visit, rev_start(c), out_r)
        return carry_f, carry_r

    carries = lax.fori_loop(0, nchunk // 4, functools.partial(pair_body, True), (carry_f, carry_r))
    lax.fori_loop(nchunk // 4, nchunk // 2, functools.partial(pair_body, False), carries)


def _rglru(xl, gl, xc, gc, conv_w, conv_b, wa_bd, ba, wx_bd, bx, lam, ct):
    bsz, seq, _ = xl.shape
    ctx_len = xc.shape[1]
    seg_rows = SCAN_SEGS * SCAN_SEG_PITCH
    col = lambda rows: pl.BlockSpec((1, rows, ct), lambda b, c: (b, 0, c))
    vec = lambda rows: pl.BlockSpec((rows, ct), lambda b, c: (0, c))
    bd = pl.BlockSpec((2, ct, ct), lambda b, c: (0, c, c))
    return pl.pallas_call(
        functools.partial(_lru_kernel, seq, ctx_len, ct),
        grid=(bsz, LRU_WIDTH // ct),
        in_specs=[col(seq), col(seq), col(ctx_len), col(ctx_len),
                  vec(LRU_CONV), vec(1), bd, vec(2), bd, vec(2), vec(2)],
        out_specs=[col(seq), col(ctx_len)],
        out_shape=[jax.ShapeDtypeStruct((bsz, seq, LRU_WIDTH), bf16),
                   jax.ShapeDtypeStruct((bsz, ctx_len, LRU_WIDTH), bf16)],
        scratch_shapes=[pltpu.VMEM((seq + 2 * SCAN_PAD, ct), f32), pltpu.VMEM((seq, ct), f32),
                        pltpu.VMEM((seq, ct), f32), pltpu.VMEM((2, 2, 4, seg_rows, ct), f32)],
        compiler_params=_params(("arbitrary", "arbitrary")),
        name="rglru",
    )(xl, gl, xc, gc, conv_w, conv_b.reshape(1, LRU_WIDTH), wa_bd, ba, wx_bd, bx, lam)


def _tail_kernel(final, tiles_per_seq, fixed_row,
                 h_ref, ya_ref, yb_ref, yc_ref, mod_ref, g1_ref, g2_ref, gf_ref,
                 wg_ref, woa_ref, wob_ref, woc_ref, wout_ref, wup_ref, wdn_ref, o_ref, act_ref):
    row = fixed_row if fixed_row is not None else pl.program_id(0) // tiles_per_seq
    mod = lambda i: mod_ref[pl.ds(row, 1), i * D_MODEL:(i + 1) * D_MODEL]
    x = h_ref[...]
    a = _norm_mod(x, g1_ref[...], mod(0), mod(1)).astype(bf16)
    merged = None
    for i, (y_ref, w_ref) in enumerate(((ya_ref, woa_ref), (yb_ref, wob_ref), (yc_ref, woc_ref))):
        gate = jax.nn.sigmoid(_mm(a, wg_ref[:, i * D_MODEL:(i + 1) * D_MODEL]))
        term = gate * _mm(y_ref[...], w_ref[...])
        merged = term if merged is None else merged + term
    h1 = x + mod(2) * _mm(merged.astype(bf16), wout_ref[...])
    a2 = _norm_mod(h1, g2_ref[...], mod(3), mod(4)).astype(bf16)

    for c0 in range(0, FFN_HIDDEN, FFN_CHUNK):
        up = _mm(a2, wup_ref[:, c0:c0 + FFN_CHUNK])
        gate = _mm(a2, wup_ref[:, FFN_HIDDEN + c0:FFN_HIDDEN + c0 + FFN_CHUNK])
        act_ref[:, c0:c0 + FFN_CHUNK] = ((gate * jax.nn.sigmoid(gate)) * up).astype(bf16)
    h2 = h1 + mod(5) * _mm(act_ref[...], wdn_ref[...])
    if final:
        h2 = h2 * lax.rsqrt(jnp.mean(h2 * h2, axis=-1, keepdims=True) + EPS) * gf_ref[...]
    o_ref[...] = h2


def _tail(h, ya, yb, yc, mod, g1, g2, gf, wts, tm, tiles_per_seq, fixed_row, final):
    rows = h.shape[0]
    row_spec = lambda width: pl.BlockSpec((tm, width), lambda i: (i, 0))
    wg, woa, wob, woc, wout, wup, wdn = wts
    return pl.pallas_call(
        functools.partial(_tail_kernel, final, tiles_per_seq, fixed_row),
        grid=(rows // tm,),
        in_specs=[row_spec(D_MODEL), row_spec(ATTN_WIDTH), row_spec(CONV_WIDTH), row_spec(LRU_WIDTH),
                  _const_spec((MOD_ROWS, 6 * D_MODEL)),
                  _const_spec((1, D_MODEL)), _const_spec((1, D_MODEL)), _const_spec((1, D_MODEL)),
                  _const_spec(wg.shape), _const_spec(woa.shape), _const_spec(wob.shape),
                  _const_spec(woc.shape), _const_spec(wout.shape), _const_spec(wup.shape),
                  _const_spec(wdn.shape)],
        out_specs=row_spec(D_MODEL),
        out_shape=jax.ShapeDtypeStruct((rows, D_MODEL), f32),
        scratch_shapes=[pltpu.VMEM((tm, FFN_HIDDEN), bf16)],
        compiler_params=_params(("arbitrary",)),
        name="merge_out_swiglu",
    )(h, ya, yb, yc, mod, g1, g2, gf, wg, woa, wob, woc, wout, wup, wdn)


def _rope_tables(seq):
    rows = seq // GRID_W
    inv = jnp.power(ROPE_BASE, -jnp.arange(ROPE_FREQS, dtype=f32) / ROPE_FREQS)
    row_ang = jnp.arange(rows, dtype=f32)[:, None] * inv[None]
    col_ang = jnp.arange(GRID_W, dtype=f32)[:, None] * inv[None]

    def table(fn):
        by_row = jnp.broadcast_to(fn(row_ang)[:, None, :], (rows, GRID_W, ROPE_FREQS))
        by_col = jnp.broadcast_to(fn(col_ang)[None, :, :], (rows, GRID_W, ROPE_FREQS))
        return jnp.concatenate([by_row, by_col], axis=-1).reshape(seq, 2 * ROPE_FREQS)
    cos, sin = table(jnp.cos), table(jnp.sin)
    reps = LANES // HEAD_DIM
    return jnp.tile(jnp.concatenate([cos, cos], axis=-1), (1, reps)), \
        jnp.tile(jnp.concatenate([-sin, sin], axis=-1), (1, reps))


def _block_diag(w):
    two, nb, d, e = w.shape
    eye = jnp.eye(nb, dtype=w.dtype)
    return jnp.einsum('xnde,nm->xndme', w, eye).reshape(two, nb * d, nb * e)


def kernel(x, c, ctx, c_ctx, mod_w, mod_b, norm1_g, norm2_g, w_in, attn_sink, conv_dw_w, conv_dw_b, conv_ln_g,
           conv_ln_b, lru_conv_w, lru_conv_b, lru_wa, lru_ba, lru_wx, lru_bx, lru_lam, w_o_attn, w_o_conv,
           w_o_lru, w_out, ffn_w_up, ffn_w_down, final_norm_g):
    bsz, seq, _ = x.shape
    ctx_len = ctx.shape[1]
    depth = mod_w.shape[0]
    assert bsz + 1 <= MOD_ROWS and seq % BLOCK == 0 and ctx_len == SCAN_CHUNK
    tm = 512
    tiles_per_seq = seq // tm
    tm_ctx = ctx_len
    ctx_row = bsz

    cvec = jnp.zeros((MOD_ROWS, D_MODEL), f32).at[:bsz].set(c).at[ctx_row].set(c_ctx)
    mod_all = _modulation(cvec, mod_w, mod_b)
    cos, sin = _rope_tables(seq)
    row = lambda v: v.reshape(1, -1)

    h_lat = x.reshape(bsz * seq, D_MODEL)
    h_ctx = ctx.reshape(bsz * ctx_len, D_MODEL)
    for l in range(depth):
        need_ctx = l < depth - 1
        mod = mod_all[l]
        w_mix = w_in[l, :, :MIX_COLS].astype(bf16)
        wts = (
            w_in[l, :, MIX_COLS:].astype(bf16),
            w_o_attn[l].astype(bf16), w_o_conv[l].astype(bf16), w_o_lru[l].astype(bf16),
            w_out[l].astype(bf16),
            ffn_w_up[l].astype(bf16),
            ffn_w_down[l].astype(bf16),
        )
        g1, g2 = row(norm1_g[l]), row(norm2_g[l])

        q, k, v, hc, lx, gl = _inproj(h_lat, mod, g1, w_mix, tm, tiles_per_seq, None, cos, sin)
        qc, kc, vc, hcc, lxc, glc = _inproj(h_ctx, mod, g1, w_mix, tm_ctx, 1, ctx_row)
        shp = lambda t, n: t.reshape(bsz, n, t.shape[-1])
        kc3, vc3 = shp(kc, ctx_len), shp(vc, ctx_len)

        y_attn = _attention(attn_sink[l], shp(q, seq), shp(k, seq), shp(v, seq), kc3, vc3)
        y_conv = _conformer_conv(shp(hc, seq), conv_dw_w[l], conv_dw_b[l], conv_ln_g[l], conv_ln_b[l], 256)
        y_lru, y_lru_c = _rglru(shp(lx, seq), shp(gl, seq), shp(lxc, ctx_len), shp(glc, ctx_len),
                                lru_conv_w[l], lru_conv_b[l], _block_diag(lru_wa[l]).astype(bf16), lru_ba[l],
                                _block_diag(lru_wx[l]).astype(bf16), lru_bx[l], lru_lam[l], LANES)
        flat = lambda t: t.reshape(-1, t.shape[-1])
        h_lat = _tail(h_lat, flat(y_attn), flat(y_conv), flat(y_lru), mod, g1, g2, row(final_norm_g), wts,
                      tm, tiles_per_seq, None, final=not need_ctx)
        if need_ctx:
            y_attn_c = _attention(attn_sink[l], shp(qc, ctx_len), None, None, kc3, vc3)
            y_conv_c = _conformer_conv(shp(hcc, ctx_len), conv_dw_w[l], conv_dw_b[l], conv_ln_g[l],
                                       conv_ln_b[l], ctx_len)
            h_ctx = _tail(h_ctx, flat(y_attn_c), flat(y_conv_c), flat(y_lru_c), mod, g1, g2,
                          row(final_norm_g), wts, tm_ctx, 1, ctx_row, final=False)
    return h_lat.reshape(bsz, seq, D_MODEL)
```

```python
import functools

import jax
import jax.numpy as jnp
import numpy as np
from jax import lax
from jax.experimental import pallas as pl
from jax.experimental.pallas import tpu as pltpu

D_MODEL = 1024
GRID_W = 64
N_HEADS = 8
N_KV_HEADS = 2
GROUP = N_HEADS // N_KV_HEADS
HEAD_DIM = 64
ATTN_WIDTH = N_HEADS * HEAD_DIM
KV_WIDTH = N_KV_HEADS * HEAD_DIM
BLOCK = 128
ATTN_SCALE = HEAD_DIM ** -0.5
ROPE_BASE = 10000.0
ROPE_FREQS = HEAD_DIM // 4
CONV_WIDTH = 512
CONV_KERNEL = 31
CONV_PAD = (CONV_KERNEL - 1) // 2
LRU_WIDTH = 512
LRU_BLOCKS = 8
LRU_BLOCK_DIM = LRU_WIDTH // LRU_BLOCKS
LRU_CONV = 4
LRU_PAD_LEFT = 2
LRU_C = 8.0
FFN_HIDDEN = 2816
MIX_COLS = ATTN_WIDTH + 2 * KV_WIDTH + 2 * CONV_WIDTH + 2 * LRU_WIDTH
EPS = 1e-6
NEG_INF = -1e30

LANES = 128
SUBLANES = 8
VMEM_LIMIT_BYTES = 56 * 1024 * 1024

MOD_ROWS = 8
FFN_CHUNK = 256
CONV_HALO = 16
CONV_ROWS = 32
SCAN_CHUNK = 256
SCAN_SEGS = SUBLANES
SCAN_SEG = SCAN_CHUNK // SCAN_SEGS
SCAN_SEG_PITCH = SCAN_SEG + 8

f32 = jnp.float32
bf16 = jnp.bfloat16


def _mm(a, b):
    return jnp.dot(a, b, preferred_element_type=f32)


def _const_spec(shape):
    n = len(shape)
    return pl.BlockSpec(shape, lambda *_: (0,) * n, pipeline_mode=pl.Buffered(1))


def _params(sem):
    return pltpu.CompilerParams(dimension_semantics=sem, vmem_limit_bytes=VMEM_LIMIT_BYTES)


def _norm_mod(x, g, shift, scale):
    y = x * lax.rsqrt(jnp.mean(x * x, axis=-1, keepdims=True) + EPS)
    return (y * g) * (1.0 + scale) + shift


def _mod_kernel(c_ref, w_ref, b_ref, o_ref):
    c = c_ref[...]
    s = (c * jax.nn.sigmoid(c)).astype(bf16)
    o_ref[0] = _mm(s, w_ref[0].astype(bf16)) + b_ref[0]


def _modulation(cvec, mod_w, mod_b):
    depth, _, cols = mod_w.shape
    tn = 1024
    return pl.pallas_call(
        _mod_kernel,
        grid=(depth, cols // tn),
        in_specs=[
            pl.BlockSpec((MOD_ROWS, D_MODEL), lambda l, j: (0, 0)),
            pl.BlockSpec((1, D_MODEL, tn), lambda l, j: (l, 0, j)),
            pl.BlockSpec((1, 1, tn), lambda l, j: (l, 0, j)),
        ],
        out_specs=pl.BlockSpec((1, MOD_ROWS, tn), lambda l, j: (l, 0, j)),
        out_shape=jax.ShapeDtypeStruct((depth, MOD_ROWS, cols), f32),
        compiler_params=_params(("arbitrary", "arbitrary")),
        name="modulation",
    )(cvec, mod_w, mod_b.reshape(depth, 1, cols))


def _rope(z, cos, sin_signed):
    lane = lax.broadcasted_iota(jnp.int32, z.shape, 1)
    first_half = (lane & (HEAD_DIM - 1)) < HEAD_DIM // 2
    partner = jnp.where(first_half, pltpu.roll(z, LANES - HEAD_DIM // 2, axis=1),
                        pltpu.roll(z, HEAD_DIM // 2, axis=1))
    return z * cos + partner * sin_signed


def _dup_heads(z):
    low_half = lax.broadcasted_iota(jnp.int32, z.shape, 1) < HEAD_DIM
    swapped = pltpu.roll(z, HEAD_DIM, axis=1)
    return jnp.concatenate([jnp.where(low_half, z, swapped), jnp.where(low_half, swapped, z)], axis=1)


def _inproj_kernel(rope,tiles_per_seq, fixed_row, h_ref, mod_ref, g_ref, w_ref, *rest):
    if rope:
        cos_ref, sin_ref, q_ref, k_ref, v_ref, hc_ref, lx_ref, gl_ref = rest
    else:
        q_ref, k_ref, v_ref, hc_ref, lx_ref, gl_ref = rest
    row = fixed_row if fixed_row is not None else pl.program_id(0) // tiles_per_seq
    shift = mod_ref[pl.ds(row, 1), 0:D_MODEL]
    scale = mod_ref[pl.ds(row, 1), D_MODEL:2 * D_MODEL]
    a = _norm_mod(h_ref[...], g_ref[...], shift, scale).astype(bf16)

    if rope:
        cos, sin = cos_ref[...], sin_ref[...]
        fix = lambda z: _rope(z, cos, sin)
    else:
        fix = lambda z: z
    zq = _mm(a, w_ref[:, 0:ATTN_WIDTH])
    for j in range(ATTN_WIDTH // LANES):
        zj = fix(zq[:, j * LANES:(j + 1) * LANES]) * ATTN_SCALE
        q_ref[:, j * LANES:(j + 1) * LANES] = zj.astype(bf16)
    c0 = ATTN_WIDTH
    zkv = _mm(a, w_ref[:, c0:c0 + 2 * KV_WIDTH])
    k_ref[...] = _dup_heads(fix(zkv[:, 0:KV_WIDTH])).astype(bf16)
    v_ref[...] = _dup_heads(zkv[:, KV_WIDTH:]).astype(bf16)
    c0 += 2 * KV_WIDTH
    val = _mm(a, w_ref[:, c0:c0 + CONV_WIDTH])
    gate = _mm(a, w_ref[:, c0 + CONV_WIDTH:c0 + 2 * CONV_WIDTH])
    hc_ref[...] = val * jax.nn.sigmoid(gate)
    c0 += 2 * CONV_WIDTH
    lx_ref[...] = _mm(a, w_ref[:, c0:c0 + LRU_WIDTH])
    gl_ref[...] = jax.nn.gelu(_mm(a, w_ref[:, c0 + LRU_WIDTH:c0 + 2 * LRU_WIDTH]))


def _inproj(h, mod, g, w_mix, tm, tiles_per_seq, fixed_row, cos=None, sin=None):
    rows = h.shape[0]
    rope = cos is not None
    row_spec = lambda width: pl.BlockSpec((tm, width), lambda i: (i, 0))
    in_specs = [row_spec(D_MODEL), _const_spec((MOD_ROWS, 6 * D_MODEL)), _const_spec((1, D_MODEL)),
                _const_spec((D_MODEL, MIX_COLS))]
    args = [h, mod, g, w_mix]
    if rope:
        tab = pl.BlockSpec((tm, LANES), lambda i: (i % tiles_per_seq, 0))
        in_specs += [tab, tab]
        args += [cos, sin]
    widths = (ATTN_WIDTH, 2 * KV_WIDTH, 2 * KV_WIDTH, CONV_WIDTH, LRU_WIDTH, LRU_WIDTH)
    dtypes = (bf16, bf16, bf16, f32, f32, f32)
    return pl.pallas_call(
        functools.partial(_inproj_kernel, rope, tiles_per_seq, fixed_row),
        grid=(rows // tm,),
        in_specs=in_specs,
        out_specs=[row_spec(w) for w in widths],
        out_shape=[jax.ShapeDtypeStruct((rows, w), d) for w, d in zip(widths, dtypes)],
        compiler_params=_params(("arbitrary",)),
        name="inproj_rope" if rope else "inproj_ctx",
    )(*args)


def _attn_block(sink_ref, q, k_parts, v_parts, prev_ok, next_ok):
    tq = q.shape[0]
    low_half = lax.broadcasted_iota(jnp.int32, (tq, LANES), 1) < HEAD_DIM
    zero = jnp.zeros((tq, LANES), q.dtype)
    outs = []
    for g in range(N_KV_HEADS):
        gs = slice(g * LANES, (g + 1) * LANES)
        pairs = [q[:, (2 * g + i) * LANES:(2 * g + i + 1) * LANES] for i in range(GROUP // 2)]
        q4 = jnp.concatenate(
            [jnp.where(low_half if r % 2 == 0 else ~low_half, pairs[r // 2], zero) for r in range(GROUP)], axis=0)
        kcat = jnp.concatenate([p[:, gs] for p in k_parts], axis=0)
        vcat = jnp.concatenate([p[:, gs] for p in v_parts], axis=0)
        s = lax.dot_general(q4, kcat, (((1,), (1,)), ((), ())), preferred_element_type=f32)
        if prev_ok is not None:
            s = jnp.concatenate([
                jnp.where(prev_ok, s[:, 0:BLOCK], NEG_INF),
                s[:, BLOCK:2 * BLOCK],
                jnp.where(next_ok, s[:, 2 * BLOCK:3 * BLOCK], NEG_INF),
                s[:, 3 * BLOCK:]], axis=1)
        sink = jnp.concatenate(
            [jnp.full((tq, 1), sink_ref[g * GROUP + r], f32) for r in range(GROUP)], axis=0)
        m = jnp.maximum(jnp.max(s, axis=-1, keepdims=True), sink)
        p = jnp.exp(s - m)
        den = jnp.sum(p, axis=-1, keepdims=True) + jnp.exp(sink - m)
        o = _mm(p.astype(bf16), vcat) / den
        for i in range(GROUP // 2):
            outs.append(jnp.where(low_half, o[2 * i * tq:(2 * i + 1) * tq], o[(2 * i + 1) * tq:(2 * i + 2) * tq]))
    return jnp.concatenate(outs, axis=1)


def _attn_window_kernel(nsteps, sink_ref, q_ref, kp, ko, kn, vp, vo, vn, kc, vc, o_ref):
    j = pl.program_id(1)
    rows = GROUP * BLOCK
    qi = lax.broadcasted_iota(jnp.int32, (rows, BLOCK), 0) & (BLOCK - 1)
    kj = lax.broadcasted_iota(jnp.int32, (rows, BLOCK), 1)
    first_prev_ok = kj + jnp.where(j > 0, 0, -2 * BLOCK) >= qi
    last_next_ok = kj + jnp.where(j < nsteps - 1, 0, 2 * BLOCK) <= qi
    k_blocks = [kp[0], ko[0, 0:BLOCK], ko[0, BLOCK:2 * BLOCK], kn[0]]
    v_blocks = [vp[0], vo[0, 0:BLOCK], vo[0, BLOCK:2 * BLOCK], vn[0]]
    for b in range(2):
        out = _attn_block(sink_ref, q_ref[0, b * BLOCK:(b + 1) * BLOCK],
                          k_blocks[b:b + 3] + [kc[0]], v_blocks[b:b + 3] + [vc[0]],
                          first_prev_ok if b == 0 else kj >= qi,
                          kj <= qi if b == 0 else last_next_ok)
        o_ref[0, b * BLOCK:(b + 1) * BLOCK, :] = out.astype(bf16)


def _attn_ctx_kernel(sink_ref, q_ref, kc, vc, o_ref):
    o_ref[0] = _attn_block(sink_ref, q_ref[0], [kc[0]], [vc[0]], None, None).astype(bf16)


def _attention(sink, q, k, v, kc, vc):
    bsz, seq, _ = q.shape
    ctx_len = kc.shape[1]
    kvw = kc.shape[2]
    ctx_spec = pl.BlockSpec((1, ctx_len, kvw), lambda b, j: (b, 0, 0))
    sink_spec = pl.BlockSpec(memory_space=pltpu.SMEM)
    if k is None:
        tq, nsteps = seq, 1
        body = _attn_ctx_kernel
        in_specs, args = [ctx_spec, ctx_spec], [kc, vc]
    else:
        tq = 2 * BLOCK
        nsteps = seq // tq
        nblk = seq // BLOCK
        body = functools.partial(_attn_window_kernel, nsteps)
        prev = pl.BlockSpec((1, BLOCK, kvw), lambda b, j: (b, jnp.maximum(2 * j - 1, 0), 0))
        own = pl.BlockSpec((1, tq, kvw), lambda b, j: (b, j, 0))
        nxt = pl.BlockSpec((1, BLOCK, kvw), lambda b, j: (b, jnp.minimum(2 * j + 2, nblk - 1), 0))
        in_specs = [prev, own, nxt, prev, own, nxt, ctx_spec, ctx_spec]
        args = [k, k, k, v, v, v, kc, vc]
    return pl.pallas_call(
        body,
        grid=(bsz, nsteps),
        in_specs=[sink_spec, pl.BlockSpec((1, tq, ATTN_WIDTH), lambda b, j: (b, j, 0))] + in_specs,
        out_specs=pl.BlockSpec((1, tq, ATTN_WIDTH), lambda b, j: (b, j, 0)),
        out_shape=jax.ShapeDtypeStruct((bsz, seq, ATTN_WIDTH), bf16),
        compiler_params=_params(("arbitrary", "arbitrary")),
        name="attn_ctx" if k is None else "attn_window",
    )(sink, q, *args)


SCAN_PAD = 8


def _lru_kernel(seq, ctx_len, ct,
                xl_ref, gl_ref, xc_ref, gc_ref, cw_ref, cb_ref, wa_ref, ba_ref, wx_ref, bx_ref, lam_ref,
                yl_ref, yc_ref,
                xp_ref, u_ref, hf_ref, seg_ref):
    cw = [cw_ref[k:k + 1, :] for k in range(LRU_CONV)]
    cb = cb_ref[...]
    neg_lam = -lam_ref[...]
    softplus_neg_lam = jnp.maximum(neg_lam, 0.0) + jnp.log1p(jnp.exp(-jnp.abs(neg_lam)))

    def conv_into_u(src_ref, n):
        zero = jnp.zeros((SCAN_PAD, ct), f32)
        xp_ref[0:SCAN_PAD] = zero
        xp_ref[SCAN_PAD + n:2 * SCAN_PAD + n] = zero
        for r0 in range(0, n, SCAN_CHUNK):
            xp_ref[SCAN_PAD + r0:SCAN_PAD + r0 + SCAN_CHUNK] = src_ref[0, r0:r0 + SCAN_CHUNK, :]
        for r0 in range(0, n, SCAN_CHUNK):
            acc = jnp.broadcast_to(cb, (SCAN_CHUNK, ct))
            for k in range(LRU_CONV):
                off = SCAN_PAD + r0 + k - LRU_PAD_LEFT
                acc = acc + cw[k] * xp_ref[off:off + SCAN_CHUNK, :]
            u_ref[r0:r0 + SCAN_CHUNK] = acc

    half_log_a_scale = (-0.5 * LRU_C) * softplus_neg_lam

    def stage(buf, r0, d):
        uc = u_ref[pl.ds(r0, SCAN_CHUNK)]
        ub = uc.astype(bf16)
        tanh_r = jnp.tanh(0.5 * (_mm(ub, wa_ref[d]) + ba_ref[d:d + 1]))
        tanh_i = jnp.tanh(0.5 * (_mm(ub, wx_ref[d]) + bx_ref[d:d + 1]))
        log_a = (tanh_r + 1.0) * half_log_a_scale[d:d + 1]
        a = jnp.exp(log_a)
        b = jnp.sqrt(-jnp.tanh(log_a) * (a * a + 1.0)) * ((0.5 * tanh_i + 0.5) * uc)
        for s in range(SCAN_SEGS):
            lo = s * SCAN_SEG_PITCH
            seg_ref[buf, d, 0, lo:lo + SCAN_SEG] = a[s * SCAN_SEG:(s + 1) * SCAN_SEG]
            seg_ref[buf, d, 1, lo:lo + SCAN_SEG] = b[s * SCAN_SEG:(s + 1) * SCAN_SEG]

    def scan(buf, d, carry):
        hl = jnp.zeros((SCAN_SEGS, ct), f32)
        pa = jnp.ones((SCAN_SEGS, ct), f32)
        for i in range(SCAN_SEG):
            t = i if d == 0 else SCAN_SEG - 1 - i
            rows = pl.ds(t, SCAN_SEGS, stride=SCAN_SEG_PITCH)
            av = seg_ref[buf, d, 0, rows]
            hl = av * hl + seg_ref[buf, d, 1, rows]
            pa = av * pa
            seg_ref[buf, d, 2, rows] = hl
            seg_ref[buf, d, 3, rows] = pa
        order = range(SCAN_SEGS) if d == 0 else range(SCAN_SEGS - 1, -1, -1)
        pieces = [None] * SCAN_SEGS
        for s in order:
            lo = s * SCAN_SEG_PITCH
            pieces[s] = seg_ref[buf, d, 2, lo:lo + SCAN_SEG] + seg_ref[buf, d, 3, lo:lo + SCAN_SEG] * carry
            carry = pa[s:s + 1] * carry + hl[s:s + 1]
        return jnp.concatenate(pieces, axis=0), carry

    conv_into_u(xc_ref, ctx_len)
    zero_row = jnp.zeros((1, ct), f32)
    assert ctx_len == SCAN_CHUNK
    stage(0, 0, 0)
    stage(0, 0, 1)
    cf, carry_f = scan(0, 0, zero_row)
    cr, carry_r = scan(0, 1, zero_row)
    yc_ref[0] = ((cf + cr) * gc_ref[0]).astype(bf16)

    conv_into_u(xl_ref, seq)
    nchunk = seq // SCAN_CHUNK
    fwd_start = lambda c: pl.multiple_of(c * SCAN_CHUNK, SCAN_CHUNK)
    rev_start = lambda c: pl.multiple_of((nchunk - 1 - c) * SCAN_CHUNK, SCAN_CHUNK)
    stage(0, fwd_start(0), 0)
    stage(0, rev_start(0), 1)

    assert nchunk % 4 == 0

    def emit(first_visit, r0, states):
        if first_visit:
            hf_ref[pl.ds(r0, SCAN_CHUNK)] = states
        else:
            yl_ref[0, pl.ds(r0, SCAN_CHUNK), :] = (
                (states + hf_ref[pl.ds(r0, SCAN_CHUNK)]) * gl_ref[0, pl.ds(r0, SCAN_CHUNK), :]).astype(bf16)

    def pair_body(first_visit, it, carries):
        carry_f, carry_r = carries
        for buf in range(2):
            c = 2 * it + buf
            nxt = jnp.minimum(c + 1, nchunk - 1)
            stage(1 - buf, fwd_start(nxt), 0)
            stage(1 - buf, rev_start(nxt), 1)
            out_f, carry_f = scan(buf, 0, carry_f)
            out_r, carry_r = scan(buf, 1, carry_r)
            emit(first_visit, fwd_start(c), out_f)
            emit(first_visit, rev_start(c), out_r)
        return carry_f, carry_r

    carries = lax.fori_loop(0, nchunk // 4, functools.partial(pair_body, True), (carry_f, carry_r))
    lax.fori_loop(nchunk // 4, nchunk // 2, functools.partial(pair_body, False), carries)


def _rglru(xl, gl, xc, gc, conv_w, conv_b, wa_bd, ba, wx_bd, bx, lam, ct):
    bsz, seq, _ = xl.shape
    ctx_len = xc.shape[1]
    seg_rows = SCAN_SEGS * SCAN_SEG_PITCH
    col = lambda rows: pl.BlockSpec((1, rows, ct), lambda b, c: (b, 0, c))
    vec = lambda rows: pl.BlockSpec((rows, ct), lambda b, c: (0, c))
    bd = pl.BlockSpec((2, ct, ct), lambda b, c: (0, c, c))
    return pl.pallas_call(
        functools.partial(_lru_kernel, seq, ctx_len, ct),
        grid=(bsz, LRU_WIDTH // ct),
        in_specs=[col(seq), col(seq), col(ctx_len), col(ctx_len),
                  vec(LRU_CONV), vec(1), bd, vec(2), bd, vec(2), vec(2)],
        out_specs=[col(seq), col(ctx_len)],
        out_shape=[jax.ShapeDtypeStruct((bsz, seq, LRU_WIDTH), bf16),
                   jax.ShapeDtypeStruct((bsz, ctx_len, LRU_WIDTH), bf16)],
        scratch_shapes=[pltpu.VMEM((seq + 2 * SCAN_PAD, ct), f32), pltpu.VMEM((seq, ct), f32),
                        pltpu.VMEM((seq, ct), f32), pltpu.VMEM((2, 2, 4, seg_rows, ct), f32)],
        compiler_params=_params(("arbitrary", "arbitrary")),
        name="rglru",
    )(xl, gl, xc, gc, conv_w, conv_b.reshape(1, LRU_WIDTH), wa_bd, ba, wx_bd, bx, lam)


def _conv_ln_swish(tt, first, last, x_ref, prev_rows, next_rows, w_ref, b_ref, g_ref, beta_ref, win_ref, y_ref):
    lane_tiles = [slice(c * LANES, (c + 1) * LANES) for c in range(CONV_WIDTH // LANES)]
    zero = jnp.zeros((CONV_HALO, LANES), f32)
    for c, cs in enumerate(lane_tiles):
        win_ref[c, CONV_HALO:CONV_HALO + tt] = x_ref[:, cs]

    def fill(lo, rows):
        for c, cs in enumerate(lane_tiles):
            win_ref[c, lo:lo + CONV_HALO] = zero if rows is None else rows(cs)
    pl.when(jnp.logical_not(first))(lambda: fill(0, prev_rows))
    pl.when(first)(lambda: fill(0, None))
    pl.when(jnp.logical_not(last))(lambda: fill(CONV_HALO + tt, next_rows))
    pl.when(last)(lambda: fill(CONV_HALO + tt, None))

    gain, beta = g_ref[...], beta_ref[...]

    def rows(r0):
        pieces = []
        for c, cs in enumerate(lane_tiles):
            acc = jnp.broadcast_to(b_ref[:, cs], (CONV_ROWS, LANES))
            for k in range(CONV_KERNEL):
                off = r0 + CONV_HALO - CONV_PAD + k
                acc = acc + jnp.broadcast_to(w_ref[k:k + 1, cs], (CONV_ROWS, LANES)) * win_ref[c, off:off + CONV_ROWS, :]
            pieces.append(acc)
        acc = jnp.concatenate(pieces, axis=1)
        mu = jnp.mean(acc, axis=-1, keepdims=True)
        xc = acc - mu
        var = jnp.mean(xc * xc, axis=-1, keepdims=True)
        y = xc * lax.rsqrt(var + EPS) * gain + beta
        y = (y * jax.nn.sigmoid(y)).astype(bf16)
        y_ref[r0:r0 + CONV_ROWS, :] = y
        return y
    return rows


def _tail_kernel(final, tiles_per_seq, fixed_row,
                 one_ref, h_ref, ya_ref, hc_ref, hc_next_ref, hc_after_ref, yc_ref, mod_ref, g1_ref, g2_ref, gf_ref,
                 cw_ref, cb_ref, cg_ref, cbeta_ref,
                 wg_ref, woa_ref, wob_ref, woc_ref, wout_ref, wup_ref, wdn_ref, o_ref,
                 act_ref, win_ref, yb_ref, yb_next_ref):
    i = pl.program_id(0)
    tm = h_ref.shape[0]
    conv = functools.partial(_conv_ln_swish, tm, w_ref=cw_ref, b_ref=cb_ref, g_ref=cg_ref, beta_ref=cbeta_ref,
                             win_ref=win_ref, y_ref=yb_next_ref)
    head_of_next = lambda cs: hc_next_ref[0:CONV_HALO, cs]

    @pl.when(i == 0)
    def _():
        rows = conv(True, tiles_per_seq == 1, hc_ref, None, head_of_next)
        for r0 in range(0, tm, CONV_ROWS):
            rows(r0)
    yb_ref[...] = yb_next_ref[...]
    next_in_seq = (i + 1) % tiles_per_seq
    conv_rows = conv(next_in_seq == 0, next_in_seq == tiles_per_seq - 1, hc_next_ref,
                     lambda cs: hc_ref[tm - CONV_HALO:tm, cs], lambda cs: hc_after_ref[:, cs])
    row = fixed_row if fixed_row is not None else pl.program_id(0) // tiles_per_seq
    mod = lambda i: mod_ref[pl.ds(row, 1), i * D_MODEL:(i + 1) * D_MODEL]
    x = h_ref[...]
    a = _norm_mod(x, g1_ref[...], mod(0), mod(1)).astype(bf16)
    merged = None
    for i, (y_ref, w_ref) in enumerate(((ya_ref, woa_ref), (yb_ref, wob_ref), (yc_ref, woc_ref))):
        gate = jax.nn.sigmoid(_mm(a, wg_ref[:, i * D_MODEL:(i + 1) * D_MODEL]))
        term = gate * _mm(y_ref[...], w_ref[...])
        merged = term if merged is None else merged + term
    h1 = x + mod(2) * _mm(merged.astype(bf16), wout_ref[...])
    a2 = _norm_mod(h1, g2_ref[...], mod(3), mod(4)).astype(bf16)

    always = one_ref[0] == 1
    nffn = FFN_HIDDEN // FFN_CHUNK
    nconv = tm // CONV_ROWS
    for j, c0 in enumerate(range(0, FFN_HIDDEN, FFN_CHUNK)):
        up = _mm(a2, wup_ref[:, c0:c0 + FFN_CHUNK])
        gate = _mm(a2, wup_ref[:, FFN_HIDDEN + c0:FFN_HIDDEN + c0 + FFN_CHUNK])
        act = ((gate * jax.nn.sigmoid(gate)) * up).astype(bf16)
        pieces, done = [], 0
        for rc in range(j * nconv // nffn, (j + 1) * nconv // nffn):
            r0 = rc * CONV_ROWS
            y = conv_rows(r0)
            pieces += [act[done:r0]] * (r0 > done) + [jnp.where(always, act[r0:r0 + CONV_ROWS], y[:, 0:FFN_CHUNK])]
            done = r0 + CONV_ROWS
        if pieces:
            act = jnp.concatenate(pieces + [act[done:]] * (done < tm), axis=0)
        act_ref[:, c0:c0 + FFN_CHUNK] = act
    h2 = h1 + mod(5) * _mm(act_ref[...], wdn_ref[...])
    if final:
        h2 = h2 * lax.rsqrt(jnp.mean(h2 * h2, axis=-1, keepdims=True) + EPS) * gf_ref[...]
    o_ref[...] = h2


def _tail(h, ya, hc, yc, mod, g1, g2, gf, conv, wts, tm, tiles_per_seq, fixed_row, final):
    rows = h.shape[0]
    row_spec = lambda width: pl.BlockSpec((tm, width), lambda i: (i, 0))
    ntiles = rows // tm
    hpt = tm // CONV_HALO
    nhalo = rows // CONV_HALO
    wg, woa, wob, woc, wout, wup, wdn = wts
    cw, cb, cg, cbeta = conv
    vec = lambda v: v.reshape(1, -1)
    return pl.pallas_call(
        functools.partial(_tail_kernel, final, tiles_per_seq, fixed_row),
        grid=(rows // tm,),
        in_specs=[pl.BlockSpec(memory_space=pltpu.SMEM),
                  row_spec(D_MODEL), row_spec(ATTN_WIDTH), row_spec(CONV_WIDTH),
                  pl.BlockSpec((tm, CONV_WIDTH), lambda i: (jnp.minimum(i + 1, ntiles - 1), 0)),
                  pl.BlockSpec((CONV_HALO, CONV_WIDTH), lambda i: (jnp.minimum((i + 2) * hpt, nhalo - 1), 0)),
                  row_spec(LRU_WIDTH),
                  _const_spec((MOD_ROWS, 6 * D_MODEL)),
                  _const_spec((1, D_MODEL)), _const_spec((1, D_MODEL)), _const_spec((1, D_MODEL)),
                  _const_spec(cw.shape), _const_spec((1, CONV_WIDTH)), _const_spec((1, CONV_WIDTH)),
                  _const_spec((1, CONV_WIDTH)),
                  _const_spec(wg.shape), _const_spec(woa.shape), _const_spec(wob.shape),
                  _const_spec(woc.shape), _const_spec(wout.shape), _const_spec(wup.shape),
                  _const_spec(wdn.shape)],
        out_specs=row_spec(D_MODEL),
        out_shape=jax.ShapeDtypeStruct((rows, D_MODEL), f32),
        scratch_shapes=[pltpu.VMEM((tm, FFN_HIDDEN), bf16),
                        pltpu.VMEM((CONV_WIDTH // LANES, tm + 2 * CONV_HALO, LANES), f32),
                        pltpu.VMEM((tm, CONV_WIDTH), bf16), pltpu.VMEM((tm, CONV_WIDTH), bf16)],
        compiler_params=_params(("arbitrary",)),
        name="merge_out_swiglu",
    )(jnp.ones((1,), jnp.int32), h, ya, hc, hc, hc, yc, mod, g1, g2, gf, cw, vec(cb), vec(cg), vec(cbeta),
      wg, woa, wob, woc, wout, wup, wdn)


def _rope_tables(seq):
    rows = seq // GRID_W
    inv = jnp.power(ROPE_BASE, -jnp.arange(ROPE_FREQS, dtype=f32) / ROPE_FREQS)
    row_ang = jnp.arange(rows, dtype=f32)[:, None] * inv[None]
    col_ang = jnp.arange(GRID_W, dtype=f32)[:, None] * inv[None]

    def table(fn):
        by_row = jnp.broadcast_to(fn(row_ang)[:, None, :], (rows, GRID_W, ROPE_FREQS))
        by_col = jnp.broadcast_to(fn(col_ang)[None, :, :], (rows, GRID_W, ROPE_FREQS))
        return jnp.concatenate([by_row, by_col], axis=-1).reshape(seq, 2 * ROPE_FREQS)
    cos, sin = table(jnp.cos), table(jnp.sin)
    reps = LANES // HEAD_DIM
    return jnp.tile(jnp.concatenate([cos, cos], axis=-1), (1, reps)), \
        jnp.tile(jnp.concatenate([-sin, sin], axis=-1), (1, reps))


def _block_diag(w):
    two, nb, d, e = w.shape
    eye = jnp.eye(nb, dtype=w.dtype)
    return jnp.einsum('xnde,nm->xndme', w, eye).reshape(two, nb * d, nb * e)


def kernel(x, c, ctx, c_ctx, mod_w, mod_b, norm1_g, norm2_g, w_in, attn_sink, conv_dw_w, conv_dw_b, conv_ln_g,
           conv_ln_b, lru_conv_w, lru_conv_b, lru_wa, lru_ba, lru_wx, lru_bx, lru_lam, w_o_attn, w_o_conv,
           w_o_lru, w_out, ffn_w_up, ffn_w_down, final_norm_g):
    bsz, seq, _ = x.shape
    ctx_len = ctx.shape[1]
    depth = mod_w.shape[0]
    assert bsz + 1 <= MOD_ROWS and seq % BLOCK == 0 and ctx_len == SCAN_CHUNK
    tm = 512
    tiles_per_seq = seq // tm
    tm_ctx = ctx_len
    ctx_row = bsz

    cvec = jnp.zeros((MOD_ROWS, D_MODEL), f32).at[:bsz].set(c).at[ctx_row].set(c_ctx)
    mod_all = _modulation(cvec, mod_w, mod_b)
    cos, sin = _rope_tables(seq)
    row = lambda v: v.reshape(1, -1)

    h_lat = x.reshape(bsz * seq, D_MODEL)
    h_ctx = ctx.reshape(bsz * ctx_len, D_MODEL)
    for l in range(depth):
        need_ctx = l < depth - 1
        mod = mod_all[l]
        w_mix = w_in[l, :, :MIX_COLS].astype(bf16)
        wts = (
            w_in[l, :, MIX_COLS:].astype(bf16),
            w_o_attn[l].astype(bf16), w_o_conv[l].astype(bf16), w_o_lru[l].astype(bf16),
            w_out[l].astype(bf16),
            ffn_w_up[l].astype(bf16),
            ffn_w_down[l].astype(bf16),
        )
        g1, g2 = row(norm1_g[l]), row(norm2_g[l])

        q, k, v, hc, lx, gl = _inproj(h_lat, mod, g1, w_mix, tm, tiles_per_seq, None, cos, sin)
        qc, kc, vc, hcc, lxc, glc = _inproj(h_ctx, mod, g1, w_mix, tm_ctx, 1, ctx_row)
        shp = lambda t, n: t.reshape(bsz, n, t.shape[-1])
        kc3, vc3 = shp(kc, ctx_len), shp(vc, ctx_len)

        y_attn = _attention(attn_sink[l], shp(q, seq), shp(k, seq), shp(v, seq), kc3, vc3)
        conv = (conv_dw_w[l], conv_dw_b[l], conv_ln_g[l], conv_ln_b[l])
        y_lru, y_lru_c = _rglru(shp(lx, seq), shp(gl, seq), shp(lxc, ctx_len), shp(glc, ctx_len),
                                lru_conv_w[l], lru_conv_b[l], _block_diag(lru_wa[l]).astype(bf16), lru_ba[l],
                                _block_diag(lru_wx[l]).astype(bf16), lru_bx[l], lru_lam[l], LANES)
        flat = lambda t: t.reshape(-1, t.shape[-1])
        h_lat = _tail(h_lat, flat(y_attn), hc, flat(y_lru), mod, g1, g2, row(final_norm_g), conv, wts,
                      tm, tiles_per_seq, None, final=not need_ctx)
        if need_ctx:
            y_attn_c = _attention(attn_sink[l], shp(qc, ctx_len), None, None, kc3, vc3)
            h_ctx = _tail(h_ctx, flat(y_attn_c), hcc, flat(y_lru_c), mod, g1, g2,
                          row(final_norm_g), conv, wts, tm_ctx, 1, ctx_row, final=False)
    return h_lat.reshape(bsz, seq, D_MODEL)
```

```python
import functools

import jax
import jax.numpy as jnp
import numpy as np
from jax import lax
from jax.experimental import pallas as pl
from jax.experimental.pallas import tpu as pltpu

D_MODEL = 1024
GRID_W = 64
N_HEADS = 8
N_KV_HEADS = 2
GROUP = N_HEADS // N_KV_HEADS
HEAD_DIM = 64
ATTN_WIDTH = N_HEADS * HEAD_DIM
KV_WIDTH = N_KV_HEADS * HEAD_DIM
BLOCK = 128
ATTN_SCALE = HEAD_DIM ** -0.5
LOG2_E = 1.4426950408889634
ROPE_BASE = 10000.0
ROPE_FREQS = HEAD_DIM // 4
CONV_WIDTH = 512
CONV_KERNEL = 31
CONV_PAD = (CONV_KERNEL - 1) // 2
LRU_WIDTH = 512
LRU_BLOCKS = 8
LRU_BLOCK_DIM = LRU_WIDTH // LRU_BLOCKS
LRU_CONV = 4
LRU_PAD_LEFT = 2
LRU_C = 8.0
FFN_HIDDEN = 2816
MIX_COLS = ATTN_WIDTH + 2 * KV_WIDTH + 2 * CONV_WIDTH + 2 * LRU_WIDTH
EPS = 1e-6
NEG_INF = -1e30

LANES = 128
SUBLANES = 8
VMEM_LIMIT_BYTES = 56 * 1024 * 1024

MOD_ROWS = 8
FFN_CHUNK = 256
ATTN_ROWS = 32
CONV_HALO = 16
CONV_ROWS = 32
SCAN_CHUNK = 256
SCAN_SEGS = SUBLANES
SCAN_SEG = SCAN_CHUNK // SCAN_SEGS
SCAN_SEG_PITCH = SCAN_SEG + 8

f32 = jnp.float32
bf16 = jnp.bfloat16


def _mm(a, b):
    return jnp.dot(a, b, preferred_element_type=f32)


def _const_spec(shape):
    n = len(shape)
    return pl.BlockSpec(shape, lambda *_: (0,) * n, pipeline_mode=pl.Buffered(1))


def _params(sem):
    return pltpu.CompilerParams(dimension_semantics=sem, vmem_limit_bytes=VMEM_LIMIT_BYTES)


def _norm_mod(x, g, shift, scale):
    y = x * lax.rsqrt(jnp.mean(x * x, axis=-1, keepdims=True) + EPS)
    return (y * g) * (1.0 + scale) + shift


def _mod_kernel(c_ref, w_ref, b_ref, o_ref):
    c = c_ref[...]
    s = (c * jax.nn.sigmoid(c)).astype(bf16)
    o_ref[0] = _mm(s, w_ref[0].astype(bf16)) + b_ref[0]


def _modulation(cvec, mod_w, mod_b):
    depth, _, cols = mod_w.shape
    tn = 1024
    return pl.pallas_call(
        _mod_kernel,
        grid=(depth, cols // tn),
        in_specs=[
            pl.BlockSpec((MOD_ROWS, D_MODEL), lambda l, j: (0, 0)),
            pl.BlockSpec((1, D_MODEL, tn), lambda l, j: (l, 0, j)),
            pl.BlockSpec((1, 1, tn), lambda l, j: (l, 0, j)),
        ],
        out_specs=pl.BlockSpec((1, MOD_ROWS, tn), lambda l, j: (l, 0, j)),
        out_shape=jax.ShapeDtypeStruct((depth, MOD_ROWS, cols), f32),
        compiler_params=_params(("arbitrary", "arbitrary")),
        name="modulation",
    )(cvec, mod_w, mod_b.reshape(depth, 1, cols))


def _rope(z, cos, sin_signed):
    lane = lax.broadcasted_iota(jnp.int32, z.shape, 1)
    first_half = (lane & (HEAD_DIM - 1)) < HEAD_DIM // 2
    partner = jnp.where(first_half, pltpu.roll(z, LANES - HEAD_DIM // 2, axis=1),
                        pltpu.roll(z, HEAD_DIM // 2, axis=1))
    return z * cos + partner * sin_signed


def _dup_heads(z):
    low_half = lax.broadcasted_iota(jnp.int32, z.shape, 1) < HEAD_DIM
    swapped = pltpu.roll(z, HEAD_DIM, axis=1)
    return jnp.concatenate([jnp.where(low_half, z, swapped), jnp.where(low_half, swapped, z)], axis=1)


def _inproj_kernel(rope,tiles_per_seq, fixed_row, h_ref, mod_ref, g_ref, w_ref, *rest):
    if rope:
        cos_ref, sin_ref, q_ref, k_ref, v_ref, hc_ref, lx_ref, gl_ref = rest
    else:
        q_ref, k_ref, v_ref, hc_ref, lx_ref, gl_ref = rest
    row = fixed_row if fixed_row is not None else pl.program_id(0) // tiles_per_seq
    shift = mod_ref[pl.ds(row, 1), 0:D_MODEL]
    scale = mod_ref[pl.ds(row, 1), D_MODEL:2 * D_MODEL]
    a = _norm_mod(h_ref[...], g_ref[...], shift, scale).astype(bf16)

    if rope:
        cos, sin = cos_ref[...], sin_ref[...]
        fix = lambda z: _rope(z, cos, sin)
    else:
        fix = lambda z: z
    zq = _mm(a, w_ref[:, 0:ATTN_WIDTH])
    for j in range(ATTN_WIDTH // LANES):
        zj = fix(zq[:, j * LANES:(j + 1) * LANES]) * ATTN_SCALE
        q_ref[:, j * LANES:(j + 1) * LANES] = zj.astype(bf16)
    c0 = ATTN_WIDTH
    zkv = _mm(a, w_ref[:, c0:c0 + 2 * KV_WIDTH])
    k_ref[...] = _dup_heads(fix(zkv[:, 0:KV_WIDTH])).astype(bf16)
    v_ref[...] = _dup_heads(zkv[:, KV_WIDTH:]).astype(bf16)
    c0 += 2 * KV_WIDTH
    val = _mm(a, w_ref[:, c0:c0 + CONV_WIDTH])
    gate = _mm(a, w_ref[:, c0 + CONV_WIDTH:c0 + 2 * CONV_WIDTH])
    hc_ref[...] = val * jax.nn.sigmoid(gate)
    c0 += 2 * CONV_WIDTH
    lx_ref[...] = _mm(a, w_ref[:, c0:c0 + LRU_WIDTH])
    gl_ref[...] = jax.nn.gelu(_mm(a, w_ref[:, c0 + LRU_WIDTH:c0 + 2 * LRU_WIDTH]))


def _inproj(h, mod, g, w_mix, tm, tiles_per_seq, fixed_row, cos=None, sin=None):
    rows = h.shape[0]
    rope = cos is not None
    row_spec = lambda width: pl.BlockSpec((tm, width), lambda i: (i, 0))
    in_specs = [row_spec(D_MODEL), _const_spec((MOD_ROWS, 6 * D_MODEL)), _const_spec((1, D_MODEL)),
                _const_spec((D_MODEL, MIX_COLS))]
    args = [h, mod, g, w_mix]
    if rope:
        tab = pl.BlockSpec((tm, LANES), lambda i: (i % tiles_per_seq, 0))
        in_specs += [tab, tab]
        args += [cos, sin]
    widths = (ATTN_WIDTH, 2 * KV_WIDTH, 2 * KV_WIDTH, CONV_WIDTH, LRU_WIDTH, LRU_WIDTH)
    dtypes = (bf16, bf16, bf16, f32, f32, f32)
    return pl.pallas_call(
        functools.partial(_inproj_kernel, rope, tiles_per_seq, fixed_row),
        grid=(rows // tm,),
        in_specs=in_specs,
        out_specs=[row_spec(w) for w in widths],
        out_shape=[jax.ShapeDtypeStruct((rows, w), d) for w, d in zip(widths, dtypes)],
        compiler_params=_params(("arbitrary",)),
        name="inproj_rope" if rope else "inproj_ctx",
    )(*args)


def _attn_block(sink_ref, q, k_parts, v_parts, prev_ok, next_ok):
    tq = q.shape[0]
    low_half = lax.broadcasted_iota(jnp.int32, (tq, LANES), 1) < HEAD_DIM
    zero = jnp.zeros((tq, LANES), q.dtype)
    outs = []
    for g in range(N_KV_HEADS):
        gs = slice(g * LANES, (g + 1) * LANES)
        pairs = [q[:, (2 * g + i) * LANES:(2 * g + i + 1) * LANES] for i in range(GROUP // 2)]
        q4 = jnp.concatenate(
            [jnp.where(low_half if r % 2 == 0 else ~low_half, pairs[r // 2], zero) for r in range(GROUP)], axis=0)
        kcat = jnp.concatenate([p[:, gs] for p in k_parts], axis=0)
        vcat = jnp.concatenate([p[:, gs] for p in v_parts], axis=0)
        s = lax.dot_general(q4, kcat, (((1,), (1,)), ((), ())), preferred_element_type=f32)
        ps, dens = [], []
        for r0 in range(0, GROUP * tq, ATTN_ROWS):
            sc = s[r0:r0 + ATTN_ROWS]
            if prev_ok is not None:
                rs = slice(r0 % tq, r0 % tq + ATTN_ROWS)
                sc = jnp.concatenate([
                    jnp.where(prev_ok[rs], sc[:, 0:BLOCK], NEG_INF),
                    sc[:, BLOCK:2 * BLOCK],
                    jnp.where(next_ok[rs], sc[:, 2 * BLOCK:3 * BLOCK], NEG_INF),
                    sc[:, 3 * BLOCK:]], axis=1)
            sink = sink_ref[g * GROUP + r0 // tq]
            m = jnp.maximum(jnp.max(sc, axis=-1, keepdims=True), sink)
            p = jnp.exp(sc - m)
            dens.append(jnp.sum(p, axis=-1, keepdims=True) + jnp.exp(sink - m))
            ps.append(p.astype(bf16))
        o = _mm(jnp.concatenate(ps, axis=0), vcat) / jnp.concatenate(dens, axis=0)
        for i in range(GROUP // 2):
            outs.append(jnp.where(low_half, o[2 * i * tq:(2 * i + 1) * tq], o[(2 * i + 1) * tq:(2 * i + 2) * tq]))
    return jnp.concatenate(outs, axis=1)


def _attn_window_kernel(nsteps, sink_ref, q_ref, kp, ko, kn, vp, vo, vn, kc, vc, o_ref):
    j = pl.program_id(1)
    qi = lax.broadcasted_iota(jnp.int32, (BLOCK, BLOCK), 0)
    kj = lax.broadcasted_iota(jnp.int32, (BLOCK, BLOCK), 1)
    first_prev_ok = kj + jnp.where(j > 0, 0, -2 * BLOCK) >= qi
    last_next_ok = kj + jnp.where(j < nsteps - 1, 0, 2 * BLOCK) <= qi
    k_blocks = [kp[0], ko[0, 0:BLOCK], ko[0, BLOCK:2 * BLOCK], kn[0]]
    v_blocks = [vp[0], vo[0, 0:BLOCK], vo[0, BLOCK:2 * BLOCK], vn[0]]
    for b in range(2):
        out = _attn_block(sink_ref, q_ref[0, b * BLOCK:(b + 1) * BLOCK],
                          k_blocks[b:b + 3] + [kc[0]], v_blocks[b:b + 3] + [vc[0]],
                          first_prev_ok if b == 0 else kj >= qi,
                          kj <= qi if b == 0 else last_next_ok)
        o_ref[0, b * BLOCK:(b + 1) * BLOCK, :] = out.astype(bf16)


def _attn_ctx_kernel(sink_ref, q_ref, kc, vc, o_ref):
    o_ref[0] = _attn_block(sink_ref, q_ref[0], [kc[0]], [vc[0]], None, None).astype(bf16)


def _attention(sink, q, k, v, kc, vc):
    bsz, seq, _ = q.shape
    ctx_len = kc.shape[1]
    kvw = kc.shape[2]
    ctx_spec = pl.BlockSpec((1, ctx_len, kvw), lambda b, j: (b, 0, 0))
    sink_spec = pl.BlockSpec(memory_space=pltpu.SMEM)
    if k is None:
        tq, nsteps = seq, 1
        body = _attn_ctx_kernel
        in_specs, args = [ctx_spec, ctx_spec], [kc, vc]
    else:
        tq = 2 * BLOCK
        nsteps = seq // tq
        nblk = seq // BLOCK
        body = functools.partial(_attn_window_kernel, nsteps)
        prev = pl.BlockSpec((1, BLOCK, kvw), lambda b, j: (b, jnp.maximum(2 * j - 1, 0), 0))
        own = pl.BlockSpec((1, tq, kvw), lambda b, j: (b, j, 0))
        nxt = pl.BlockSpec((1, BLOCK, kvw), lambda b, j: (b, jnp.minimum(2 * j + 2, nblk - 1), 0))
        in_specs = [prev, own, nxt, prev, own, nxt, ctx_spec, ctx_spec]
        args = [k, k, k, v, v, v, kc, vc]
    return pl.pallas_call(
        body,
        grid=(bsz, nsteps),
        in_specs=[sink_spec, pl.BlockSpec((1, tq, ATTN_WIDTH), lambda b, j: (b, j, 0))] + in_specs,
        out_specs=pl.BlockSpec((1, tq, ATTN_WIDTH), lambda b, j: (b, j, 0)),
        out_shape=jax.ShapeDtypeStruct((bsz, seq, ATTN_WIDTH), bf16),
        compiler_params=_params(("arbitrary", "arbitrary")),
        name="attn_ctx" if k is None else "attn_window",
    )(sink, q, *args)


SCAN_PAD = 8


def _lru_kernel(seq, ctx_len, ct,
                xl_ref, gl_ref, xc_ref, gc_ref, cw_ref, cb_ref, wa_ref, ba_ref, wx_ref, bx_ref, lam_ref,
                yl_ref, yc_ref,
                xp_ref, u_ref, hf_ref, seg_ref):
    cw = [cw_ref[k:k + 1, :] for k in range(LRU_CONV)]
    cb = cb_ref[...]
    neg_lam = -lam_ref[...]
    softplus_neg_lam = jnp.maximum(neg_lam, 0.0) + jnp.log1p(jnp.exp(-jnp.abs(neg_lam)))

    def conv_into_u(src_ref, n):
        zero = jnp.zeros((SCAN_PAD, ct), f32)
        xp_ref[0:SCAN_PAD] = zero
        xp_ref[SCAN_PAD + n:2 * SCAN_PAD + n] = zero
        for r0 in range(0, n, SCAN_CHUNK):
            xp_ref[SCAN_PAD + r0:SCAN_PAD + r0 + SCAN_CHUNK] = src_ref[0, r0:r0 + SCAN_CHUNK, :]
        for r0 in range(0, n, SCAN_CHUNK):
            acc = jnp.broadcast_to(cb, (SCAN_CHUNK, ct))
            for k in range(LRU_CONV):
                off = SCAN_PAD + r0 + k - LRU_PAD_LEFT
                acc = acc + cw[k] * xp_ref[off:off + SCAN_CHUNK, :]
            u_ref[r0:r0 + SCAN_CHUNK] = acc

    half_scale = (0.5 * LRU_C) * softplus_neg_lam

    def stage(buf, r0, d):
        uc = u_ref[pl.ds(r0, SCAN_CHUNK)]
        ub = uc.astype(bf16)
        half_za = _mm(ub, wa_ref[d]) + ba_ref[d:d + 1]
        half_zx = _mm(ub, wx_ref[d]) + bx_ref[d:d + 1]
        for s in range(SCAN_SEGS):
            rs = slice(s * SCAN_SEG, (s + 1) * SCAN_SEG)
            neg_log_a = (jnp.tanh(half_za[rs]) + 1.0) * half_scale[d:d + 1]
            a = jnp.exp2(neg_log_a * (-LOG2_E))
            y = jnp.tanh(neg_log_a) * (a * a + 1.0)
            root = jnp.where(y > 0.0, y * lax.rsqrt(y), 0.0)
            b = root * ((0.5 * jnp.tanh(half_zx[rs]) + 0.5) * uc[rs])
            lo = s * SCAN_SEG_PITCH
            seg_ref[buf, d, 0, lo:lo + SCAN_SEG] = a
            seg_ref[buf, d, 1, lo:lo + SCAN_SEG] = b

    def scan(buf, d, carry):
        hl = jnp.zeros((SCAN_SEGS, ct), f32)
        pa = jnp.ones((SCAN_SEGS, ct), f32)
        for i in range(SCAN_SEG):
            t = i if d == 0 else SCAN_SEG - 1 - i
            rows = pl.ds(t, SCAN_SEGS, stride=SCAN_SEG_PITCH)
            av = seg_ref[buf, d, 0, rows]
            hl = av * hl + seg_ref[buf, d, 1, rows]
            pa = av * pa
            seg_ref[buf, d, 2, rows] = hl
            seg_ref[buf, d, 3, rows] = pa
        order = range(SCAN_SEGS) if d == 0 else range(SCAN_SEGS - 1, -1, -1)
        pieces = [None] * SCAN_SEGS
        for s in order:
            lo = s * SCAN_SEG_PITCH
            pieces[s] = seg_ref[buf, d, 2, lo:lo + SCAN_SEG] + seg_ref[buf, d, 3, lo:lo + SCAN_SEG] * carry
            carry = pa[s:s + 1] * carry + hl[s:s + 1]
        return jnp.concatenate(pieces, axis=0), carry

    conv_into_u(xc_ref, ctx_len)
    zero_row = jnp.zeros((1, ct), f32)
    assert ctx_len == SCAN_CHUNK
    stage(0, 0, 0)
    stage(0, 0, 1)
    cf, carry_f = scan(0, 0, zero_row)
    cr, carry_r = scan(0, 1, zero_row)
    yc_ref[0] = ((cf + cr) * gc_ref[0]).astype(bf16)

    conv_into_u(xl_ref, seq)
    nchunk = seq // SCAN_CHUNK
    fwd_start = lambda c: pl.multiple_of(c * SCAN_CHUNK, SCAN_CHUNK)
    rev_start = lambda c: pl.multiple_of((nchunk - 1 - c) * SCAN_CHUNK, SCAN_CHUNK)
    stage(0, fwd_start(0), 0)
    stage(0, rev_start(0), 1)

    assert nchunk % 4 == 0

    def emit(first_visit, r0, states):
        if first_visit:
            hf_ref[pl.ds(r0, SCAN_CHUNK)] = states
        else:
            yl_ref[0, pl.ds(r0, SCAN_CHUNK), :] = (
                (states + hf_ref[pl.ds(r0, SCAN_CHUNK)]) * gl_ref[0, pl.ds(r0, SCAN_CHUNK), :]).astype(bf16)

    def pair_body(first_visit, it, carries):
        carry_f, carry_r = carries
        for buf in range(2):
            c = 2 * it + buf
            nxt = jnp.minimum(c + 1, nchunk - 1)
            stage(1 - buf, fwd_start(nxt), 0)
            stage(1 - buf, rev_start(nxt), 1)
            out_f, carry_f = scan(buf, 0, carry_f)
            out_r, carry_r = scan(buf, 1, carry_r)
            emit(first_visit, fwd_start(c), out_f)
            emit(first_visit, rev_start(c), out_r)
        return carry_f, carry_r

    carries = lax.fori_loop(0, nchunk // 4, functools.partial(pair_body, True), (carry_f, carry_r))
    lax.fori_loop(nchunk // 4, nchunk // 2, functools.partial(pair_body, False), carries)


def _rglru(xl, gl, xc, gc, conv_w, conv_b, wa_bd, ba, wx_bd, bx, lam, ct):
    bsz, seq, _ = xl.shape
    ctx_len = xc.shape[1]
    seg_rows = SCAN_SEGS * SCAN_SEG_PITCH
    col = lambda rows: pl.BlockSpec((1, rows, ct), lambda b, c: (b, 0, c))
    vec = lambda rows: pl.BlockSpec((rows, ct), lambda b, c: (0, c))
    bd = pl.BlockSpec((2, ct, ct), lambda b, c: (0, c, c))
    return pl.pallas_call(
        functools.partial(_lru_kernel, seq, ctx_len, ct),
        grid=(bsz, LRU_WIDTH // ct),
        in_specs=[col(seq), col(seq), col(ctx_len), col(ctx_len),
                  vec(LRU_CONV), vec(1), bd, vec(2), bd, vec(2), vec(2)],
        out_specs=[col(seq), col(ctx_len)],
        out_shape=[jax.ShapeDtypeStruct((bsz, seq, LRU_WIDTH), bf16),
                   jax.ShapeDtypeStruct((bsz, ctx_len, LRU_WIDTH), bf16)],
        scratch_shapes=[pltpu.VMEM((seq + 2 * SCAN_PAD, ct), f32), pltpu.VMEM((seq, ct), f32),
                        pltpu.VMEM((seq, ct), f32), pltpu.VMEM((2, 2, 4, seg_rows, ct), f32)],
        compiler_params=_params(("arbitrary", "arbitrary")),
        name="rglru",
    )(xl, gl, xc, gc, conv_w, conv_b.reshape(1, LRU_WIDTH), wa_bd, ba, wx_bd, bx, lam)


def _conv_ln_swish(tt, first, last, x_ref, prev_rows, next_rows, w_ref, b_ref, g_ref, beta_ref, win_ref, y_ref):
    lane_tiles = [slice(c * LANES, (c + 1) * LANES) for c in range(CONV_WIDTH // LANES)]
    zero = jnp.zeros((CONV_HALO, LANES), f32)
    for c, cs in enumerate(lane_tiles):
        win_ref[c, CONV_HALO:CONV_HALO + tt] = x_ref[:, cs]

    def fill(lo, rows):
        for c, cs in enumerate(lane_tiles):
            win_ref[c, lo:lo + CONV_HALO] = zero if rows is None else rows(cs)
    pl.when(jnp.logical_not(first))(lambda: fill(0, prev_rows))
    pl.when(first)(lambda: fill(0, None))
    pl.when(jnp.logical_not(last))(lambda: fill(CONV_HALO + tt, next_rows))
    pl.when(last)(lambda: fill(CONV_HALO + tt, None))

    gain, beta = g_ref[...], beta_ref[...]

    def rows(r0):
        pieces = []
        for c, cs in enumerate(lane_tiles):
            acc = jnp.broadcast_to(b_ref[:, cs], (CONV_ROWS, LANES))
            for k in range(CONV_KERNEL):
                off = r0 + CONV_HALO - CONV_PAD + k
                acc = acc + jnp.broadcast_to(w_ref[k:k + 1, cs], (CONV_ROWS, LANES)) * win_ref[c, off:off + CONV_ROWS, :]
            pieces.append(acc)
        acc = jnp.concatenate(pieces, axis=1)
        mu = jnp.mean(acc, axis=-1, keepdims=True)
        xc = acc - mu
        var = jnp.mean(xc * xc, axis=-1, keepdims=True)
        y = xc * lax.rsqrt(var + EPS) * gain + beta
        y = (y * jax.nn.sigmoid(y)).astype(bf16)
        y_ref[r0:r0 + CONV_ROWS, :] = y
        return y
    return rows


def _tail_kernel(final, tiles_per_seq, fixed_row,
                 one_ref, h_ref, ya_ref, hc_ref, hc_next_ref, hc_after_ref, yc_ref, mod_ref, g1_ref, g2_ref, gf_ref,
                 cw_ref, cb_ref, cg_ref, cbeta_ref,
                 wg_ref, woa_ref, wob_ref, woc_ref, wout_ref, wup_ref, wdn_ref, o_ref,
                 act_ref, win_ref, yb_ref, yb_next_ref):
    i = pl.program_id(0)
    tm = h_ref.shape[0]
    conv = functools.partial(_conv_ln_swish, tm, w_ref=cw_ref, b_ref=cb_ref, g_ref=cg_ref, beta_ref=cbeta_ref,
                             win_ref=win_ref, y_ref=yb_next_ref)
    head_of_next = lambda cs: hc_next_ref[0:CONV_HALO, cs]

    @pl.when(i == 0)
    def _():
        rows = conv(True, tiles_per_seq == 1, hc_ref, None, head_of_next)
        for r0 in range(0, tm, CONV_ROWS):
            rows(r0)
    yb_ref[...] = yb_next_ref[...]
    next_in_seq = (i + 1) % tiles_per_seq
    conv_rows = conv(next_in_seq == 0, next_in_seq == tiles_per_seq - 1, hc_next_ref,
                     lambda cs: hc_ref[tm - CONV_HALO:tm, cs], lambda cs: hc_after_ref[:, cs])
    row = fixed_row if fixed_row is not None else pl.program_id(0) // tiles_per_seq
    mod = lambda i: mod_ref[pl.ds(row, 1), i * D_MODEL:(i + 1) * D_MODEL]
    x = h_ref[...]
    a = _norm_mod(x, g1_ref[...], mod(0), mod(1)).astype(bf16)
    merged = None
    for i, (y_ref, w_ref) in enumerate(((ya_ref, woa_ref), (yb_ref, wob_ref), (yc_ref, woc_ref))):
        gate = jax.nn.sigmoid(_mm(a, wg_ref[:, i * D_MODEL:(i + 1) * D_MODEL]))
        term = gate * _mm(y_ref[...], w_ref[...])
        merged = term if merged is None else merged + term
    h1 = x + mod(2) * _mm(merged.astype(bf16), wout_ref[...])
    a2 = _norm_mod(h1, g2_ref[...], mod(3), mod(4)).astype(bf16)

    always = one_ref[0] == 1
    nffn = FFN_HIDDEN // FFN_CHUNK
    nconv = tm // CONV_ROWS
    for j, c0 in enumerate(range(0, FFN_HIDDEN, FFN_CHUNK)):
        up = _mm(a2, wup_ref[:, c0:c0 + FFN_CHUNK])
        gate = _mm(a2, wup_ref[:, FFN_HIDDEN + c0:FFN_HIDDEN + c0 + FFN_CHUNK])
        act = ((gate * jax.nn.sigmoid(gate)) * up).astype(bf16)
        pieces, done = [], 0
        for rc in range(j * nconv // nffn, (j + 1) * nconv // nffn):
            r0 = rc * CONV_ROWS
            y = conv_rows(r0)
            pieces += [act[done:r0]] * (r0 > done) + [jnp.where(always, act[r0:r0 + CONV_ROWS], y[:, 0:FFN_CHUNK])]
            done = r0 + CONV_ROWS
        if pieces:
            act = jnp.concatenate(pieces + [act[done:]] * (done < tm), axis=0)
        act_ref[:, c0:c0 + FFN_CHUNK] = act
    h2 = h1 + mod(5) * _mm(act_ref[...], wdn_ref[...])
    if final:
        h2 = h2 * lax.rsqrt(jnp.mean(h2 * h2, axis=-1, keepdims=True) + EPS) * gf_ref[...]
    o_ref[...] = h2


def _tail(h, ya, hc, yc, mod, g1, g2, gf, conv, wts, tm, tiles_per_seq, fixed_row, final):
    rows = h.shape[0]
    row_spec = lambda width: pl.BlockSpec((tm, width), lambda i: (i, 0))
    ntiles = rows // tm
    hpt = tm // CONV_HALO
    nhalo = rows // CONV_HALO
    wg, woa, wob, woc, wout, wup, wdn = wts
    cw, cb, cg, cbeta = conv
    vec = lambda v: v.reshape(1, -1)
    return pl.pallas_call(
        functools.partial(_tail_kernel, final, tiles_per_seq, fixed_row),
        grid=(rows // tm,),
        in_specs=[pl.BlockSpec(memory_space=pltpu.SMEM),
                  row_spec(D_MODEL), row_spec(ATTN_WIDTH), row_spec(CONV_WIDTH),
                  pl.BlockSpec((tm, CONV_WIDTH), lambda i: (jnp.minimum(i + 1, ntiles - 1), 0)),
                  pl.BlockSpec((CONV_HALO, CONV_WIDTH), lambda i: (jnp.minimum((i + 2) * hpt, nhalo - 1), 0)),
                  row_spec(LRU_WIDTH),
                  _const_spec((MOD_ROWS, 6 * D_MODEL)),
                  _const_spec((1, D_MODEL)), _const_spec((1, D_MODEL)), _const_spec((1, D_MODEL)),
                  _const_spec(cw.shape), _const_spec((1, CONV_WIDTH)), _const_spec((1, CONV_WIDTH)),
                  _const_spec((1, CONV_WIDTH)),
                  _const_spec(wg.shape), _const_spec(woa.shape), _const_spec(wob.shape),
                  _const_spec(woc.shape), _const_spec(wout.shape), _const_spec(wup.shape),
                  _const_spec(wdn.shape)],
        out_specs=row_spec(D_MODEL),
        out_shape=jax.ShapeDtypeStruct((rows, D_MODEL), f32),
        scratch_shapes=[pltpu.VMEM((tm, FFN_HIDDEN), bf16),
                        pltpu.VMEM((CONV_WIDTH // LANES, tm + 2 * CONV_HALO, LANES), f32),
                        pltpu.VMEM((tm, CONV_WIDTH), bf16), pltpu.VMEM((tm, CONV_WIDTH), bf16)],
        compiler_params=_params(("arbitrary",)),
        name="merge_out_swiglu",
    )(jnp.ones((1,), jnp.int32), h, ya, hc, hc, hc, yc, mod, g1, g2, gf, cw, vec(cb), vec(cg), vec(cbeta),
      wg, woa, wob, woc, wout, wup, wdn)


def _rope_tables(seq):
    rows = seq // GRID_W
    inv = jnp.power(ROPE_BASE, -jnp.arange(ROPE_FREQS, dtype=f32) / ROPE_FREQS)
    row_ang = jnp.arange(rows, dtype=f32)[:, None] * inv[None]
    col_ang = jnp.arange(GRID_W, dtype=f32)[:, None] * inv[None]

    def table(fn):
        by_row = jnp.broadcast_to(fn(row_ang)[:, None, :], (rows, GRID_W, ROPE_FREQS))
        by_col = jnp.broadcast_to(fn(col_ang)[None, :, :], (rows, GRID_W, ROPE_FREQS))
        return jnp.concatenate([by_row, by_col], axis=-1).reshape(seq, 2 * ROPE_FREQS)
    cos, sin = table(jnp.cos), table(jnp.sin)
    reps = LANES // HEAD_DIM
    return jnp.tile(jnp.concatenate([cos, cos], axis=-1), (1, reps)), \
        jnp.tile(jnp.concatenate([-sin, sin], axis=-1), (1, reps))


def _block_diag(w):
    two, nb, d, e = w.shape
    eye = jnp.eye(nb, dtype=w.dtype)
    return jnp.einsum('xnde,nm->xndme', w, eye).reshape(two, nb * d, nb * e)


def kernel(x, c, ctx, c_ctx, mod_w, mod_b, norm1_g, norm2_g, w_in, attn_sink, conv_dw_w, conv_dw_b, conv_ln_g,
           conv_ln_b, lru_conv_w, lru_conv_b, lru_wa, lru_ba, lru_wx, lru_bx, lru_lam, w_o_attn, w_o_conv,
           w_o_lru, w_out, ffn_w_up, ffn_w_down, final_norm_g):
    bsz, seq, _ = x.shape
    ctx_len = ctx.shape[1]
    depth = mod_w.shape[0]
    assert bsz + 1 <= MOD_ROWS and seq % BLOCK == 0 and ctx_len == SCAN_CHUNK
    tm = 512
    tiles_per_seq = seq // tm
    tm_ctx = ctx_len
    ctx_row = bsz

    cvec = jnp.zeros((MOD_ROWS, D_MODEL), f32).at[:bsz].set(c).at[ctx_row].set(c_ctx)
    mod_all = _modulation(cvec, mod_w, mod_b)
    cos, sin = _rope_tables(seq)
    row = lambda v: v.reshape(1, -1)

    h_lat = x.reshape(bsz * seq, D_MODEL)
    h_ctx = ctx.reshape(bsz * ctx_len, D_MODEL)
    for l in range(depth):
        need_ctx = l < depth - 1
        mod = mod_all[l]
        w_mix = w_in[l, :, :MIX_COLS].astype(bf16)
        wts = (
            w_in[l, :, MIX_COLS:].astype(bf16),
            w_o_attn[l].astype(bf16), w_o_conv[l].astype(bf16), w_o_lru[l].astype(bf16),
            w_out[l].astype(bf16),
            ffn_w_up[l].astype(bf16),
            ffn_w_down[l].astype(bf16),
        )
        g1, g2 = row(norm1_g[l]), row(norm2_g[l])

        q, k, v, hc, lx, gl = _inproj(h_lat, mod, g1, w_mix, tm, tiles_per_seq, None, cos, sin)
        qc, kc, vc, hcc, lxc, glc = _inproj(h_ctx, mod, g1, w_mix, tm_ctx, 1, ctx_row)
        shp = lambda t, n: t.reshape(bsz, n, t.shape[-1])
        kc3, vc3 = shp(kc, ctx_len), shp(vc, ctx_len)

        y_attn = _attention(attn_sink[l], shp(q, seq), shp(k, seq), shp(v, seq), kc3, vc3)
        conv = (conv_dw_w[l], conv_dw_b[l], conv_ln_g[l], conv_ln_b[l])
        y_lru, y_lru_c = _rglru(shp(lx, seq), shp(gl, seq), shp(lxc, ctx_len), shp(glc, ctx_len),
                                lru_conv_w[l], lru_conv_b[l],
                                _block_diag(0.5 * lru_wa[l]).astype(bf16), 0.5 * lru_ba[l],
                                _block_diag(0.5 * lru_wx[l]).astype(bf16), 0.5 * lru_bx[l], lru_lam[l], LANES)
        flat = lambda t: t.reshape(-1, t.shape[-1])
        h_lat = _tail(h_lat, flat(y_attn), hc, flat(y_lru), mod, g1, g2, row(final_norm_g), conv, wts,
                      tm, tiles_per_seq, None, final=not need_ctx)
        if need_ctx:
            y_attn_c = _attention(attn_sink[l], shp(qc, ctx_len), None, None, kc3, vc3)
            h_ctx = _tail(h_ctx, flat(y_attn_c), hcc, flat(y_lru_c), mod, g1, g2,
                          row(final_norm_g), conv, wts, tm_ctx, 1, ctx_row, final=False)
    return h_lat.reshape(bsz, seq, D_MODEL)
```

```python
import functools

import jax
import jax.numpy as jnp
import numpy as np
from jax import lax
from jax.experimental import pallas as pl
from jax.experimental.pallas import tpu as pltpu

D_MODEL = 1024
GRID_W = 64
N_HEADS = 8
N_KV_HEADS = 2
GROUP = N_HEADS // N_KV_HEADS
HEAD_DIM = 64
ATTN_WIDTH = N_HEADS * HEAD_DIM
KV_WIDTH = N_KV_HEADS * HEAD_DIM
BLOCK = 128
ATTN_SCALE = HEAD_DIM ** -0.5
LOG2_E = 1.4426950408889634
ROPE_BASE = 10000.0
ROPE_FREQS = HEAD_DIM // 4
CONV_WIDTH = 512
CONV_KERNEL = 31
CONV_PAD = (CONV_KERNEL - 1) // 2
LRU_WIDTH = 512
LRU_BLOCKS = 8
LRU_BLOCK_DIM = LRU_WIDTH // LRU_BLOCKS
LRU_CONV = 4
LRU_PAD_LEFT = 2
LRU_C = 8.0
FFN_HIDDEN = 2816
MIX_COLS = ATTN_WIDTH + 2 * KV_WIDTH + 2 * CONV_WIDTH + 2 * LRU_WIDTH
EPS = 1e-6
NEG_INF = -1e30

LANES = 128
SUBLANES = 8
VMEM_LIMIT_BYTES = 58 * 1024 * 1024

MOD_ROWS = 8
FFN_CHUNK = 256
NORM_ROWS = 64
ATTN_ROWS = 32
CONV_HALO = 16
CONV_ROWS = 32
SCAN_CHUNK = 256
SCAN_SEGS = SUBLANES
SCAN_SEG = SCAN_CHUNK // SCAN_SEGS
SCAN_SEG_PITCH = SCAN_SEG + 8

f32 = jnp.float32
bf16 = jnp.bfloat16


def _mm(a, b):
    return jnp.dot(a, b, preferred_element_type=f32)


def _const_spec(shape):
    n = len(shape)
    return pl.BlockSpec(shape, lambda *_: (0,) * n, pipeline_mode=pl.Buffered(1))


def _params(sem):
    return pltpu.CompilerParams(dimension_semantics=sem, vmem_limit_bytes=VMEM_LIMIT_BYTES)


def _norm_mod(x, g, shift, scale):
    y = x * lax.rsqrt(jnp.mean(x * x, axis=-1, keepdims=True) + EPS)
    return (y * g) * (1.0 + scale) + shift


def _mod_kernel(c_ref, w_ref, b_ref, o_ref):
    c = c_ref[...]
    s = (c * jax.nn.sigmoid(c)).astype(bf16)
    o_ref[0] = _mm(s, w_ref[0].astype(bf16)) + b_ref[0]


def _modulation(cvec, mod_w, mod_b):
    depth, _, cols = mod_w.shape
    tn = 1024
    return pl.pallas_call(
        _mod_kernel,
        grid=(depth, cols // tn),
        in_specs=[
            pl.BlockSpec((MOD_ROWS, D_MODEL), lambda l, j: (0, 0)),
            pl.BlockSpec((1, D_MODEL, tn), lambda l, j: (l, 0, j)),
            pl.BlockSpec((1, 1, tn), lambda l, j: (l, 0, j)),
        ],
        out_specs=pl.BlockSpec((1, MOD_ROWS, tn), lambda l, j: (l, 0, j)),
        out_shape=jax.ShapeDtypeStruct((depth, MOD_ROWS, cols), f32),
        compiler_params=_params(("arbitrary", "arbitrary")),
        name="modulation",
    )(cvec, mod_w, mod_b.reshape(depth, 1, cols))


def _rope(z, cos, sin_signed):
    lane = lax.broadcasted_iota(jnp.int32, z.shape, 1)
    first_half = (lane & (HEAD_DIM - 1)) < HEAD_DIM // 2
    partner = jnp.where(first_half, pltpu.roll(z, LANES - HEAD_DIM // 2, axis=1),
                        pltpu.roll(z, HEAD_DIM // 2, axis=1))
    return z * cos + partner * sin_signed


def _dup_heads(z):
    low_half = lax.broadcasted_iota(jnp.int32, z.shape, 1) < HEAD_DIM
    swapped = pltpu.roll(z, HEAD_DIM, axis=1)
    return jnp.concatenate([jnp.where(low_half, z, swapped), jnp.where(low_half, swapped, z)], axis=1)


def _inproj_kernel(rope, tiles_per_seq, fixed_row, one_ref, h_ref, h_next_ref, mod_ref, g_ref, w_ref, *rest):
    if rope:
        cos_ref, sin_ref, a_ref, q_ref, k_ref, v_ref, hc_ref, lx_ref, gl_ref, a_scr = rest
    else:
        a_ref, q_ref, k_ref, v_ref, hc_ref, lx_ref, gl_ref, a_scr = rest
    i = pl.program_id(0)
    tm = h_ref.shape[0]
    slot = i % 2

    def norm_rows(src_ref, dst_ref, step):
        row = fixed_row if fixed_row is not None else step // tiles_per_seq
        shift = mod_ref[pl.ds(row, 1), 0:D_MODEL]
        scale = mod_ref[pl.ds(row, 1), D_MODEL:2 * D_MODEL]

        def rows(r0):
            piece = _norm_mod(src_ref[r0:r0 + NORM_ROWS, :], g_ref[...], shift, scale).astype(bf16)
            dst_ref[r0:r0 + NORM_ROWS, :] = piece
            return piece
        return rows

    @pl.when(i == 0)
    def _():
        rows = norm_rows(h_ref, a_scr.at[0], 0)
        for r0 in range(0, tm, NORM_ROWS):
            rows(r0)
    next_rows = norm_rows(h_next_ref, a_scr.at[1 - slot], jnp.minimum(i + 1, pl.num_programs(0) - 1))
    pending = list(range(0, tm, NORM_ROWS))
    always = one_ref[0] == 1

    def tied(val):
        if not pending:
            return val
        r0 = pending.pop(0)
        other = next_rows(r0)[:, 0:val.shape[1]].astype(val.dtype)
        mid = jnp.where(always, val[r0:r0 + NORM_ROWS], other)
        return jnp.concatenate([val[:r0]] * (r0 > 0) + [mid] + [val[r0 + NORM_ROWS:]] * (r0 + NORM_ROWS < tm),
                               axis=0)

    a = a_scr[slot]
    a_ref[...] = a
    if rope:
        cos, sin = cos_ref[...], sin_ref[...]
        fix = lambda z: _rope(z, cos, sin)
    else:
        fix = lambda z: z
    zq = _mm(a, w_ref[:, 0:ATTN_WIDTH])
    for j in range(ATTN_WIDTH // LANES):
        zj = fix(zq[:, j * LANES:(j + 1) * LANES]) * ATTN_SCALE
        q_ref[:, j * LANES:(j + 1) * LANES] = tied(zj.astype(bf16))
    c0 = ATTN_WIDTH
    zkv = _mm(a, w_ref[:, c0:c0 + 2 * KV_WIDTH])
    k_ref[...] = tied(_dup_heads(fix(zkv[:, 0:KV_WIDTH])).astype(bf16))
    v_ref[...] = tied(_dup_heads(zkv[:, KV_WIDTH:]).astype(bf16))
    c0 += 2 * KV_WIDTH
    val = _mm(a, w_ref[:, c0:c0 + CONV_WIDTH])
    gate = _mm(a, w_ref[:, c0 + CONV_WIDTH:c0 + 2 * CONV_WIDTH])
    hc_ref[...] = tied(val * jax.nn.sigmoid(gate))
    c0 += 2 * CONV_WIDTH
    gl_ref[...] = tied(jax.nn.gelu(_mm(a, w_ref[:, c0 + LRU_WIDTH:c0 + 2 * LRU_WIDTH])))
    lx_ref[...] = _mm(a, w_ref[:, c0:c0 + LRU_WIDTH])
    assert not pending


def _inproj(h, mod, g, w_mix, tm, tiles_per_seq, fixed_row, cos=None, sin=None):
    rows = h.shape[0]
    rope = cos is not None
    ntiles = rows // tm
    row_spec = lambda width: pl.BlockSpec((tm, width), lambda i: (i, 0))
    in_specs = [pl.BlockSpec(memory_space=pltpu.SMEM), row_spec(D_MODEL),
                pl.BlockSpec((tm, D_MODEL), lambda i: (jnp.minimum(i + 1, ntiles - 1), 0)),
                _const_spec((MOD_ROWS, 6 * D_MODEL)), _const_spec((1, D_MODEL)),
                _const_spec((D_MODEL, MIX_COLS))]
    args = [jnp.ones((1,), jnp.int32), h, h, mod, g, w_mix]
    if rope:
        tab = pl.BlockSpec((tm, LANES), lambda i: (i % tiles_per_seq, 0))
        in_specs += [tab, tab]
        args += [cos, sin]
    widths = (D_MODEL, ATTN_WIDTH, 2 * KV_WIDTH, 2 * KV_WIDTH, CONV_WIDTH, LRU_WIDTH, LRU_WIDTH)
    dtypes = (bf16, bf16, bf16, bf16, f32, f32, f32)
    return pl.pallas_call(
        functools.partial(_inproj_kernel, rope, tiles_per_seq, fixed_row),
        grid=(ntiles,),
        in_specs=in_specs,
        out_specs=[row_spec(w) for w in widths],
        out_shape=[jax.ShapeDtypeStruct((rows, w), d) for w, d in zip(widths, dtypes)],
        scratch_shapes=[pltpu.VMEM((2, tm, D_MODEL), bf16)],
        compiler_params=_params(("arbitrary",)),
        name="inproj_rope" if rope else "inproj_ctx",
    )(*args)


def _attn_block(sink_ref, q, k_parts, v_parts, prev_ok, next_ok):
    tq = q.shape[0]
    low_half = lax.broadcasted_iota(jnp.int32, (tq, LANES), 1) < HEAD_DIM
    zero = jnp.zeros((tq, LANES), q.dtype)
    outs = []
    for g in range(N_KV_HEADS):
        gs = slice(g * LANES, (g + 1) * LANES)
        pairs = [q[:, (2 * g + i) * LANES:(2 * g + i + 1) * LANES] for i in range(GROUP // 2)]
        q4 = jnp.concatenate(
            [jnp.where(low_half if r % 2 == 0 else ~low_half, pairs[r // 2], zero) for r in range(GROUP)], axis=0)
        kcat = jnp.concatenate([p[:, gs] for p in k_parts], axis=0)
        vcat = jnp.concatenate([p[:, gs] for p in v_parts], axis=0)
        s = lax.dot_general(q4, kcat, (((1,), (1,)), ((), ())), preferred_element_type=f32)
        ps, dens = [], []
        for r0 in range(0, GROUP * tq, ATTN_ROWS):
            sc = s[r0:r0 + ATTN_ROWS]
            if prev_ok is not None:
                rs = slice(r0 % tq, r0 % tq + ATTN_ROWS)
                sc = jnp.concatenate([
                    jnp.where(prev_ok[rs], sc[:, 0:BLOCK], NEG_INF),
                    sc[:, BLOCK:2 * BLOCK],
                    jnp.where(next_ok[rs], sc[:, 2 * BLOCK:3 * BLOCK], NEG_INF),
                    sc[:, 3 * BLOCK:]], axis=1)
            sink = sink_ref[g * GROUP + r0 // tq]
            m = jnp.maximum(jnp.max(sc, axis=-1, keepdims=True), sink)
            p = jnp.exp(sc - m)
            dens.append(jnp.sum(p, axis=-1, keepdims=True) + jnp.exp(sink - m))
            ps.append(p.astype(bf16))
        o = _mm(jnp.concatenate(ps, axis=0), vcat) / jnp.concatenate(dens, axis=0)
        for i in range(GROUP // 2):
            outs.append(jnp.where(low_half, o[2 * i * tq:(2 * i + 1) * tq], o[(2 * i + 1) * tq:(2 * i + 2) * tq]))
    return jnp.concatenate(outs, axis=1)


def _attn_window_kernel(nsteps, sink_ref, q_ref, kp, ko, kn, vp, vo, vn, kc, vc, o_ref):
    j = pl.program_id(1)
    qi = lax.broadcasted_iota(jnp.int32, (BLOCK, BLOCK), 0)
    kj = lax.broadcasted_iota(jnp.int32, (BLOCK, BLOCK), 1)
    first_prev_ok = kj + jnp.where(j > 0, 0, -2 * BLOCK) >= qi
    last_next_ok = kj + jnp.where(j < nsteps - 1, 0, 2 * BLOCK) <= qi
    k_blocks = [kp[0], ko[0, 0:BLOCK], ko[0, BLOCK:2 * BLOCK], kn[0]]
    v_blocks = [vp[0], vo[0, 0:BLOCK], vo[0, BLOCK:2 * BLOCK], vn[0]]
    for b in range(2):
        out = _attn_block(sink_ref, q_ref[0, b * BLOCK:(b + 1) * BLOCK],
                          k_blocks[b:b + 3] + [kc[0]], v_blocks[b:b + 3] + [vc[0]],
                          first_prev_ok if b == 0 else kj >= qi,
                          kj <= qi if b == 0 else last_next_ok)
        o_ref[0, b * BLOCK:(b + 1) * BLOCK, :] = out.astype(bf16)


def _attn_ctx_kernel(sink_ref, q_ref, kc, vc, o_ref):
    o_ref[0] = _attn_block(sink_ref, q_ref[0], [kc[0]], [vc[0]], None, None).astype(bf16)


def _attention(sink, q, k, v, kc, vc):
    bsz, seq, _ = q.shape
    ctx_len = kc.shape[1]
    kvw = kc.shape[2]
    ctx_spec = pl.BlockSpec((1, ctx_len, kvw), lambda b, j: (b, 0, 0))
    sink_spec = pl.BlockSpec(memory_space=pltpu.SMEM)
    if k is None:
        tq, nsteps = seq, 1
        body = _attn_ctx_kernel
        in_specs, args = [ctx_spec, ctx_spec], [kc, vc]
    else:
        tq = 2 * BLOCK
        nsteps = seq // tq
        nblk = seq // BLOCK
        body = functools.partial(_attn_window_kernel, nsteps)
        prev = pl.BlockSpec((1, BLOCK, kvw), lambda b, j: (b, jnp.maximum(2 * j - 1, 0), 0))
        own = pl.BlockSpec((1, tq, kvw), lambda b, j: (b, j, 0))
        nxt = pl.BlockSpec((1, BLOCK, kvw), lambda b, j: (b, jnp.minimum(2 * j + 2, nblk - 1), 0))
        in_specs = [prev, own, nxt, prev, own, nxt, ctx_spec, ctx_spec]
        args = [k, k, k, v, v, v, kc, vc]
    return pl.pallas_call(
        body,
        grid=(bsz, nsteps),
        in_specs=[sink_spec, pl.BlockSpec((1, tq, ATTN_WIDTH), lambda b, j: (b, j, 0))] + in_specs,
        out_specs=pl.BlockSpec((1, tq, ATTN_WIDTH), lambda b, j: (b, j, 0)),
        out_shape=jax.ShapeDtypeStruct((bsz, seq, ATTN_WIDTH), bf16),
        compiler_params=_params(("arbitrary", "arbitrary")),
        name="attn_ctx" if k is None else "attn_window",
    )(sink, q, *args)


SCAN_PAD = 8


def _lru_kernel(seq, ctx_len, ct,
                xl_ref, gl_ref, xc_ref, gc_ref, cw_ref, cb_ref, wa_ref, ba_ref, wx_ref, bx_ref, lam_ref,
                yl_ref, yc_ref,
                xp_ref, u_ref, hf_ref, seg_ref):
    cw = [cw_ref[k:k + 1, :] for k in range(LRU_CONV)]
    cb = cb_ref[...]
    neg_lam = -lam_ref[...]
    softplus_neg_lam = jnp.maximum(neg_lam, 0.0) + jnp.log1p(jnp.exp(-jnp.abs(neg_lam)))

    def conv_into_u(src_ref, n):
        zero = jnp.zeros((SCAN_PAD, ct), f32)
        xp_ref[0:SCAN_PAD] = zero
        xp_ref[SCAN_PAD + n:2 * SCAN_PAD + n] = zero
        for r0 in range(0, n, SCAN_CHUNK):
            xp_ref[SCAN_PAD + r0:SCAN_PAD + r0 + SCAN_CHUNK] = src_ref[0, r0:r0 + SCAN_CHUNK, :]
        for r0 in range(0, n, SCAN_CHUNK):
            acc = jnp.broadcast_to(cb, (SCAN_CHUNK, ct))
            for k in range(LRU_CONV):
                off = SCAN_PAD + r0 + k - LRU_PAD_LEFT
                acc = acc + cw[k] * xp_ref[off:off + SCAN_CHUNK, :]
            u_ref[r0:r0 + SCAN_CHUNK] = acc

    half_scale = (0.5 * LRU_C) * softplus_neg_lam

    def stage(buf, r0, d):
        uc = u_ref[pl.ds(r0, SCAN_CHUNK)]
        ub = uc.astype(bf16)
        half_za = _mm(ub, wa_ref[d]) + ba_ref[d:d + 1]
        half_zx = _mm(ub, wx_ref[d]) + bx_ref[d:d + 1]
        for s in range(SCAN_SEGS):
            rs = slice(s * SCAN_SEG, (s + 1) * SCAN_SEG)
            neg_log_a = (jnp.tanh(half_za[rs]) + 1.0) * half_scale[d:d + 1]
            a = jnp.exp2(neg_log_a * (-LOG2_E))
            y = jnp.tanh(neg_log_a) * (a * a + 1.0)
            root = jnp.where(y > 0.0, y * lax.rsqrt(y), 0.0)
            b = root * ((0.5 * jnp.tanh(half_zx[rs]) + 0.5) * uc[rs])
            lo = s * SCAN_SEG_PITCH
            seg_ref[buf, d, 0, lo:lo + SCAN_SEG] = a
            seg_ref[buf, d, 1, lo:lo + SCAN_SEG] = b

    def scan(buf, d, carry):
        hl = jnp.zeros((SCAN_SEGS, ct), f32)
        pa = jnp.ones((SCAN_SEGS, ct), f32)
        for i in range(SCAN_SEG):
            t = i if d == 0 else SCAN_SEG - 1 - i
            rows = pl.ds(t, SCAN_SEGS, stride=SCAN_SEG_PITCH)
            av = seg_ref[buf, d, 0, rows]
            hl = av * hl + seg_ref[buf, d, 1, rows]
            pa = av * pa
            seg_ref[buf, d, 2, rows] = hl
            seg_ref[buf, d, 3, rows] = pa
        order = range(SCAN_SEGS) if d == 0 else range(SCAN_SEGS - 1, -1, -1)
        pieces = [None] * SCAN_SEGS
        for s in order:
            lo = s * SCAN_SEG_PITCH
            pieces[s] = seg_ref[buf, d, 2, lo:lo + SCAN_SEG] + seg_ref[buf, d, 3, lo:lo + SCAN_SEG] * carry
            carry = pa[s:s + 1] * carry + hl[s:s + 1]
        return jnp.concatenate(pieces, axis=0), carry

    conv_into_u(xc_ref, ctx_len)
    zero_row = jnp.zeros((1, ct), f32)
    assert ctx_len == SCAN_CHUNK
    stage(0, 0, 0)
    stage(0, 0, 1)
    cf, carry_f = scan(0, 0, zero_row)
    cr, carry_r = scan(0, 1, zero_row)
    yc_ref[0] = ((cf + cr) * gc_ref[0]).astype(bf16)

    conv_into_u(xl_ref, seq)
    nchunk = seq // SCAN_CHUNK
    fwd_start = lambda c: pl.multiple_of(c * SCAN_CHUNK, SCAN_CHUNK)
    rev_start = lambda c: pl.multiple_of((nchunk - 1 - c) * SCAN_CHUNK, SCAN_CHUNK)
    stage(0, fwd_start(0), 0)
    stage(0, rev_start(0), 1)

    assert nchunk % 4 == 0

    def emit(first_visit, r0, states):
        if first_visit:
            hf_ref[pl.ds(r0, SCAN_CHUNK)] = states
        else:
            yl_ref[0, pl.ds(r0, SCAN_CHUNK), :] = (
                (states + hf_ref[pl.ds(r0, SCAN_CHUNK)]) * gl_ref[0, pl.ds(r0, SCAN_CHUNK), :]).astype(bf16)

    def pair_body(first_visit, it, carries):
        carry_f, carry_r = carries
        for buf in range(2):
            c = 2 * it + buf
            nxt = jnp.minimum(c + 1, nchunk - 1)
            stage(1 - buf, fwd_start(nxt), 0)
            stage(1 - buf, rev_start(nxt), 1)
            out_f, carry_f = scan(buf, 0, carry_f)
            out_r, carry_r = scan(buf, 1, carry_r)
            emit(first_visit, fwd_start(c), out_f)
            emit(first_visit, rev_start(c), out_r)
        return carry_f, carry_r

    carries = lax.fori_loop(0, nchunk // 4, functools.partial(pair_body, True), (carry_f, carry_r))
    lax.fori_loop(nchunk // 4, nchunk // 2, functools.partial(pair_body, False), carries)


def _rglru(xl, gl, xc, gc, conv_w, conv_b, wa_bd, ba, wx_bd, bx, lam, ct):
    bsz, seq, _ = xl.shape
    ctx_len = xc.shape[1]
    seg_rows = SCAN_SEGS * SCAN_SEG_PITCH
    col = lambda rows: pl.BlockSpec((1, rows, ct), lambda b, c: (b, 0, c))
    vec = lambda rows: pl.BlockSpec((rows, ct), lambda b, c: (0, c))
    bd = pl.BlockSpec((2, ct, ct), lambda b, c: (0, c, c))
    return pl.pallas_call(
        functools.partial(_lru_kernel, seq, ctx_len, ct),
        grid=(bsz, LRU_WIDTH // ct),
        in_specs=[col(seq), col(seq), col(ctx_len), col(ctx_len),
                  vec(LRU_CONV), vec(1), bd, vec(2), bd, vec(2), vec(2)],
        out_specs=[col(seq), col(ctx_len)],
        out_shape=[jax.ShapeDtypeStruct((bsz, seq, LRU_WIDTH), bf16),
                   jax.ShapeDtypeStruct((bsz, ctx_len, LRU_WIDTH), bf16)],
        scratch_shapes=[pltpu.VMEM((seq + 2 * SCAN_PAD, ct), f32), pltpu.VMEM((seq, ct), f32),
                        pltpu.VMEM((seq, ct), f32), pltpu.VMEM((2, 2, 4, seg_rows, ct), f32)],
        compiler_params=_params(("arbitrary", "arbitrary")),
        name="rglru",
    )(xl, gl, xc, gc, conv_w, conv_b.reshape(1, LRU_WIDTH), wa_bd, ba, wx_bd, bx, lam)


def _conv_ln_swish(tt, first, last, x_ref, prev_rows, next_rows, w_ref, b_ref, g_ref, beta_ref, win_ref, y_ref):
    lane_tiles = [slice(c * LANES, (c + 1) * LANES) for c in range(CONV_WIDTH // LANES)]
    zero = jnp.zeros((CONV_HALO, LANES), f32)
    for c, cs in enumerate(lane_tiles):
        win_ref[c, CONV_HALO:CONV_HALO + tt] = x_ref[:, cs]

    def fill(lo, rows):
        for c, cs in enumerate(lane_tiles):
            win_ref[c, lo:lo + CONV_HALO] = zero if rows is None else rows(cs)
    pl.when(jnp.logical_not(first))(lambda: fill(0, prev_rows))
    pl.when(first)(lambda: fill(0, None))
    pl.when(jnp.logical_not(last))(lambda: fill(CONV_HALO + tt, next_rows))
    pl.when(last)(lambda: fill(CONV_HALO + tt, None))

    gain, beta = g_ref[...], beta_ref[...]

    def rows(r0):
        pieces = []
        for c, cs in enumerate(lane_tiles):
            acc = jnp.broadcast_to(b_ref[:, cs], (CONV_ROWS, LANES))
            for k in range(CONV_KERNEL):
                off = r0 + CONV_HALO - CONV_PAD + k
                acc = acc + jnp.broadcast_to(w_ref[k:k + 1, cs], (CONV_ROWS, LANES)) * win_ref[c, off:off + CONV_ROWS, :]
            pieces.append(acc)
        acc = jnp.concatenate(pieces, axis=1)
        mu = jnp.mean(acc, axis=-1, keepdims=True)
        xc = acc - mu
        var = jnp.mean(xc * xc, axis=-1, keepdims=True)
        y = xc * lax.rsqrt(var + EPS) * gain + beta
        y = (y * jax.nn.sigmoid(y)).astype(bf16)
        y_ref[r0:r0 + CONV_ROWS, :] = y
        return y
    return rows


def _tail_kernel(final, tiles_per_seq, fixed_row,
                 one_ref, h_ref, a_ref, ya_ref, hc_ref, hc_next_ref, hc_after_ref, yc_ref,
                 mod_ref, g2_ref, gf_ref, cw_ref, cb_ref, cg_ref, cbeta_ref,
                 wg_ref, woa_ref, wob_ref, woc_ref, wout_ref, wup_ref, wdn_ref, o_ref,
                 act_ref, win_ref, yb_ref):
    i = pl.program_id(0)
    tm = h_ref.shape[0]
    slot = i % 2
    conv = functools.partial(_conv_ln_swish, tm, w_ref=cw_ref, b_ref=cb_ref, g_ref=cg_ref, beta_ref=cbeta_ref,
                             win_ref=win_ref)
    head_of_next = lambda cs: hc_next_ref[0:CONV_HALO, cs]

    @pl.when(i == 0)
    def _():
        rows = conv(True, tiles_per_seq == 1, hc_ref, None, head_of_next, y_ref=yb_ref.at[0])
        for r0 in range(0, tm, CONV_ROWS):
            rows(r0)
    next_in_seq = (i + 1) % tiles_per_seq
    conv_rows = conv(next_in_seq == 0, next_in_seq == tiles_per_seq - 1, hc_next_ref,
                     lambda cs: hc_ref[tm - CONV_HALO:tm, cs], lambda cs: hc_after_ref[:, cs],
                     y_ref=yb_ref.at[1 - slot])
    row = fixed_row if fixed_row is not None else i // tiles_per_seq
    mod = lambda i: mod_ref[pl.ds(row, 1), i * D_MODEL:(i + 1) * D_MODEL]
    x = h_ref[...]
    a = a_ref[...]
    merged = None
    for n, (y, w_ref) in enumerate(((ya_ref[...], woa_ref), (yb_ref[slot], wob_ref), (yc_ref[...], woc_ref))):
        gate = jax.nn.sigmoid(_mm(a, wg_ref[:, n * D_MODEL:(n + 1) * D_MODEL]))
        term = gate * _mm(y, w_ref[...])
        merged = term if merged is None else merged + term
    h1 = x + mod(2) * _mm(merged.astype(bf16), wout_ref[...])
    a2 = _norm_mod(h1, g2_ref[...], mod(3), mod(4)).astype(bf16)

    always = one_ref[0] == 1
    nffn = FFN_HIDDEN // FFN_CHUNK

    def tie(act, r0, other):
        nrows = other.shape[0]
        mid = jnp.where(always, act[r0:r0 + nrows], other[:, 0:FFN_CHUNK])
        return jnp.concatenate([act[:r0]] * (r0 > 0) + [mid] + [act[r0 + nrows:]] * (r0 + nrows < tm), axis=0)

    for j, c0 in enumerate(range(0, FFN_HIDDEN, FFN_CHUNK)):
        up = _mm(a2, wup_ref[:, c0:c0 + FFN_CHUNK])
        gate = _mm(a2, wup_ref[:, FFN_HIDDEN + c0:FFN_HIDDEN + c0 + FFN_CHUNK])
        act = ((gate * jax.nn.sigmoid(gate)) * up).astype(bf16)
        nconv = tm // CONV_ROWS
        for rc in range(j * nconv // nffn, (j + 1) * nconv // nffn):
            act = tie(act, rc * CONV_ROWS, conv_rows(rc * CONV_ROWS))
        act_ref[:, c0:c0 + FFN_CHUNK] = act
    h2 = h1 + mod(5) * _mm(act_ref[...], wdn_ref[...])
    if final:
        h2 = h2 * lax.rsqrt(jnp.mean(h2 * h2, axis=-1, keepdims=True) + EPS) * gf_ref[...]
    o_ref[...] = h2


def _tail(h, a, ya, hc, yc, mod, g2, gf, conv, wts, tm, tiles_per_seq, fixed_row, final):
    rows = h.shape[0]
    row_spec = lambda width: pl.BlockSpec((tm, width), lambda i: (i, 0))
    ntiles = rows // tm
    hpt = tm // CONV_HALO
    nhalo = rows // CONV_HALO
    wg, woa, wob, woc, wout, wup, wdn = wts
    cw, cb, cg, cbeta = conv
    vec = lambda v: v.reshape(1, -1)
    return pl.pallas_call(
        functools.partial(_tail_kernel, final, tiles_per_seq, fixed_row),
        grid=(rows // tm,),
        in_specs=[pl.BlockSpec(memory_space=pltpu.SMEM),
                  row_spec(D_MODEL), row_spec(D_MODEL), row_spec(ATTN_WIDTH), row_spec(CONV_WIDTH),
                  pl.BlockSpec((tm, CONV_WIDTH), lambda i: (jnp.minimum(i + 1, ntiles - 1), 0)),
                  pl.BlockSpec((CONV_HALO, CONV_WIDTH), lambda i: (jnp.minimum((i + 2) * hpt, nhalo - 1), 0)),
                  row_spec(LRU_WIDTH),
                  _const_spec((MOD_ROWS, 6 * D_MODEL)),
                  _const_spec((1, D_MODEL)), _const_spec((1, D_MODEL)),
                  _const_spec(cw.shape), _const_spec((1, CONV_WIDTH)), _const_spec((1, CONV_WIDTH)),
                  _const_spec((1, CONV_WIDTH)),
                  _const_spec(wg.shape), _const_spec(woa.shape), _const_spec(wob.shape),
                  _const_spec(woc.shape), _const_spec(wout.shape), _const_spec(wup.shape),
                  _const_spec(wdn.shape)],
        out_specs=row_spec(D_MODEL),
        out_shape=jax.ShapeDtypeStruct((rows, D_MODEL), f32),
        scratch_shapes=[pltpu.VMEM((tm, FFN_HIDDEN), bf16),
                        pltpu.VMEM((CONV_WIDTH // LANES, tm + 2 * CONV_HALO, LANES), f32),
                        pltpu.VMEM((2, tm, CONV_WIDTH), bf16)],
        compiler_params=_params(("arbitrary",)),
        name="merge_out_swiglu",
    )(jnp.ones((1,), jnp.int32), h, a, ya, hc, hc, hc, yc, mod, g2, gf, cw, vec(cb), vec(cg), vec(cbeta),
      wg, woa, wob, woc, wout, wup, wdn)


def _rope_tables(seq):
    rows = seq // GRID_W
    inv = jnp.power(ROPE_BASE, -jnp.arange(ROPE_FREQS, dtype=f32) / ROPE_FREQS)
    row_ang = jnp.arange(rows, dtype=f32)[:, None] * inv[None]
    col_ang = jnp.arange(GRID_W, dtype=f32)[:, None] * inv[None]

    def table(fn):
        by_row = jnp.broadcast_to(fn(row_ang)[:, None, :], (rows, GRID_W, ROPE_FREQS))
        by_col = jnp.broadcast_to(fn(col_ang)[None, :, :], (rows, GRID_W, ROPE_FREQS))
        return jnp.concatenate([by_row, by_col], axis=-1).reshape(seq, 2 * ROPE_FREQS)
    cos, sin = table(jnp.cos), table(jnp.sin)
    reps = LANES // HEAD_DIM
    return jnp.tile(jnp.concatenate([cos, cos], axis=-1), (1, reps)), \
        jnp.tile(jnp.concatenate([-sin, sin], axis=-1), (1, reps))


def _block_diag(w):
    two, nb, d, e = w.shape
    eye = jnp.eye(nb, dtype=w.dtype)
    return jnp.einsum('xnde,nm->xndme', w, eye).reshape(two, nb * d, nb * e)


def kernel(x, c, ctx, c_ctx, mod_w, mod_b, norm1_g, norm2_g, w_in, attn_sink, conv_dw_w, conv_dw_b, conv_ln_g,
           conv_ln_b, lru_conv_w, lru_conv_b, lru_wa, lru_ba, lru_wx, lru_bx, lru_lam, w_o_attn, w_o_conv,
           w_o_lru, w_out, ffn_w_up, ffn_w_down, final_norm_g):
    bsz, seq, _ = x.shape
    ctx_len = ctx.shape[1]
    depth = mod_w.shape[0]
    assert bsz + 1 <= MOD_ROWS and seq % BLOCK == 0 and ctx_len == SCAN_CHUNK
    tm = 512
    tiles_per_seq = seq // tm
    tm_ctx = ctx_len
    ctx_row = bsz

    cvec = jnp.zeros((MOD_ROWS, D_MODEL), f32).at[:bsz].set(c).at[ctx_row].set(c_ctx)
    mod_all = _modulation(cvec, mod_w, mod_b)
    cos, sin = _rope_tables(seq)
    row = lambda v: v.reshape(1, -1)

    h_lat = x.reshape(bsz * seq, D_MODEL)
    h_ctx = ctx.reshape(bsz * ctx_len, D_MODEL)
    for l in range(depth):
        need_ctx = l < depth - 1
        mod = mod_all[l]
        w_mix = w_in[l, :, :MIX_COLS].astype(bf16)
        wts = (
            w_in[l, :, MIX_COLS:].astype(bf16),
            w_o_attn[l].astype(bf16), w_o_conv[l].astype(bf16), w_o_lru[l].astype(bf16),
            w_out[l].astype(bf16),
            ffn_w_up[l].astype(bf16),
            ffn_w_down[l].astype(bf16),
        )
        g1, g2 = row(norm1_g[l]), row(norm2_g[l])

        a_lat, q, k, v, hc, lx, gl = _inproj(h_lat, mod, g1, w_mix, tm, tiles_per_seq, None, cos, sin)
        a_ctx, qc, kc, vc, hcc, lxc, glc = _inproj(h_ctx, mod, g1, w_mix, tm_ctx, 1, ctx_row)
        shp = lambda t, n: t.reshape(bsz, n, t.shape[-1])
        kc3, vc3 = shp(kc, ctx_len), shp(vc, ctx_len)

        y_attn = _attention(attn_sink[l], shp(q, seq), shp(k, seq), shp(v, seq), kc3, vc3)
        conv = (conv_dw_w[l], conv_dw_b[l], conv_ln_g[l], conv_ln_b[l])
        y_lru, y_lru_c = _rglru(shp(lx, seq), shp(gl, seq), shp(lxc, ctx_len), shp(glc, ctx_len),
                                lru_conv_w[l], lru_conv_b[l],
                                _block_diag(0.5 * lru_wa[l]).astype(bf16), 0.5 * lru_ba[l],
                                _block_diag(0.5 * lru_wx[l]).astype(bf16), 0.5 * lru_bx[l], lru_lam[l], LANES)
        flat = lambda t: t.reshape(-1, t.shape[-1])
        h_lat = _tail(h_lat, a_lat, flat(y_attn), hc, flat(y_lru), mod, g2, row(final_norm_g), conv, wts,
                      tm, tiles_per_seq, None, final=not need_ctx)
        if need_ctx:
            y_attn_c = _attention(attn_sink[l], shp(qc, ctx_len), None, None, kc3, vc3)
            h_ctx = _tail(h_ctx, a_ctx, flat(y_attn_c), hcc, flat(y_lru_c), mod, g2,
                          row(final_norm_g), conv, wts, tm_ctx, 1, ctx_row, final=False)
    return h_lat.reshape(bsz, seq, D_MODEL)
```

```python
import functools

import jax
import jax.numpy as jnp
from jax import lax
from jax.experimental import pallas as pl
from jax.experimental.pallas import tpu as pltpu

D_MODEL = 1024
GRID_W = 64
N_HEADS = 8
N_KV_HEADS = 2
GROUP = N_HEADS // N_KV_HEADS
HEAD_DIM = 64
ATTN_WIDTH = N_HEADS * HEAD_DIM
KV_WIDTH = N_KV_HEADS * HEAD_DIM
BLOCK = 128
ATTN_SCALE = HEAD_DIM ** -0.5
LOG2_E = 1.4426950408889634
ROPE_BASE = 10000.0
ROPE_FREQS = HEAD_DIM // 4
CONV_WIDTH = 512
CONV_KERNEL = 31
CONV_PAD = (CONV_KERNEL - 1) // 2
LRU_WIDTH = 512
LRU_BLOCKS = 8
LRU_BLOCK_DIM = LRU_WIDTH // LRU_BLOCKS
LRU_CONV = 4
LRU_PAD_LEFT = 2
LRU_C = 8.0
FFN_HIDDEN = 2816
MIX_COLS = ATTN_WIDTH + 2 * KV_WIDTH + 2 * CONV_WIDTH + 2 * LRU_WIDTH
EPS = 1e-6
NEG_INF = -1e30

LANES = 128
SUBLANES = 8
VMEM_LIMIT_BYTES = 56 * 1024 * 1024

MOD_ROWS = 8
FFN_CHUNK = 256
OUT_CHUNK = 256
ATTN_ROWS = 32
ATTN_QBLOCKS = 4
CONV_HALO = 16
CONV_ROWS = 32
SCAN_CHUNK = 256
SCAN_SEGS = SUBLANES
SCAN_SEG = SCAN_CHUNK // SCAN_SEGS
SCAN_SEG_PITCH = SCAN_SEG + 8

f32 = jnp.float32
bf16 = jnp.bfloat16


def _mm(a, b):
    return jnp.dot(a, b, preferred_element_type=f32)


def _const_spec(shape):
    n = len(shape)
    return pl.BlockSpec(shape, lambda *_: (0,) * n, pipeline_mode=pl.Buffered(1))


def _params(sem):
    return pltpu.CompilerParams(dimension_semantics=sem, vmem_limit_bytes=VMEM_LIMIT_BYTES)


def _norm_mod(x, g, shift, scale):
    y = x * lax.rsqrt(jnp.mean(x * x, axis=-1, keepdims=True) + EPS)
    return (y * g) * (1.0 + scale) + shift


def _mod_kernel(c_ref, w_ref, b_ref, o_ref):
    c = c_ref[...]
    s = (c * jax.nn.sigmoid(c)).astype(bf16)
    o_ref[0] = _mm(s, w_ref[0].astype(bf16)) + b_ref[0]


def _modulation(cvec, mod_w, mod_b):
    depth, _, cols = mod_w.shape
    tn = 1024
    return pl.pallas_call(
        _mod_kernel,
        grid=(depth, cols // tn),
        in_specs=[
            pl.BlockSpec((MOD_ROWS, D_MODEL), lambda l, j: (0, 0)),
            pl.BlockSpec((1, D_MODEL, tn), lambda l, j: (l, 0, j)),
            pl.BlockSpec((1, 1, tn), lambda l, j: (l, 0, j)),
        ],
        out_specs=pl.BlockSpec((1, MOD_ROWS, tn), lambda l, j: (l, 0, j)),
        out_shape=jax.ShapeDtypeStruct((depth, MOD_ROWS, cols), f32),
        compiler_params=_params(("arbitrary", "arbitrary")),
        name="modulation",
    )(cvec, mod_w, mod_b.reshape(depth, 1, cols))


def _rope(z, cos, sin_signed):
    lane = lax.broadcasted_iota(jnp.int32, z.shape, 1)
    first_half = (lane & (HEAD_DIM - 1)) < HEAD_DIM // 2
    partner = jnp.where(first_half, pltpu.roll(z, LANES - HEAD_DIM // 2, axis=1),
                        pltpu.roll(z, HEAD_DIM // 2, axis=1))
    return z * cos + partner * sin_signed


def _dup_heads(z):
    low_half = lax.broadcasted_iota(jnp.int32, z.shape, 1) < HEAD_DIM
    swapped = pltpu.roll(z, HEAD_DIM, axis=1)
    return jnp.concatenate([jnp.where(low_half, z, swapped), jnp.where(low_half, swapped, z)], axis=1)


def _inproj_kernel(rope, tiles_per_seq, fixed_row, h_ref, mod_ref, g_ref, w_ref, *rest):
    if rope:
        cos_ref, sin_ref, q_ref, k_ref, v_ref, hc_ref, lx_ref, gl_ref = rest
    else:
        q_ref, k_ref, v_ref, hc_ref, lx_ref, gl_ref = rest
    row = fixed_row if fixed_row is not None else pl.program_id(0) // tiles_per_seq
    shift = mod_ref[pl.ds(row, 1), 0:D_MODEL]
    scale = mod_ref[pl.ds(row, 1), D_MODEL:2 * D_MODEL]
    a = _norm_mod(h_ref[...], g_ref[...], shift, scale).astype(bf16)

    if rope:
        cos, sin = cos_ref[...], sin_ref[...]
        fix = lambda z: _rope(z, cos, sin)
    else:
        fix = lambda z: z
    zq = _mm(a, w_ref[:, 0:ATTN_WIDTH])
    for j in range(ATTN_WIDTH // LANES):
        zj = fix(zq[:, j * LANES:(j + 1) * LANES]) * ATTN_SCALE
        q_ref[:, j * LANES:(j + 1) * LANES] = zj.astype(bf16)
    c0 = ATTN_WIDTH
    zkv = _mm(a, w_ref[:, c0:c0 + 2 * KV_WIDTH])
    k_ref[...] = _dup_heads(fix(zkv[:, 0:KV_WIDTH])).astype(bf16)
    v_ref[...] = _dup_heads(zkv[:, KV_WIDTH:]).astype(bf16)
    c0 += 2 * KV_WIDTH
    val = _mm(a, w_ref[:, c0:c0 + CONV_WIDTH])
    gate = _mm(a, w_ref[:, c0 + CONV_WIDTH:c0 + 2 * CONV_WIDTH])
    hc_ref[...] = val * jax.nn.sigmoid(gate)
    c0 += 2 * CONV_WIDTH
    lx_ref[...] = _mm(a, w_ref[:, c0:c0 + LRU_WIDTH])
    gl_ref[...] = jax.nn.gelu(_mm(a, w_ref[:, c0 + LRU_WIDTH:c0 + 2 * LRU_WIDTH]))


def _inproj(h, mod, g, w_mix, tm, tiles_per_seq, fixed_row, cos=None, sin=None):
    rows = h.shape[0]
    rope = cos is not None
    row_spec = lambda width: pl.BlockSpec((tm, width), lambda i: (i, 0))
    in_specs = [row_spec(D_MODEL), _const_spec((MOD_ROWS, 6 * D_MODEL)), _const_spec((1, D_MODEL)),
                _const_spec((D_MODEL, MIX_COLS))]
    args = [h, mod, g, w_mix]
    if rope:
        tab = pl.BlockSpec((tm, LANES), lambda i: (i % tiles_per_seq, 0))
        in_specs += [tab, tab]
        args += [cos, sin]
    widths = (ATTN_WIDTH, 2 * KV_WIDTH, 2 * KV_WIDTH, CONV_WIDTH, LRU_WIDTH, LRU_WIDTH)
    dtypes = (bf16, bf16, bf16, f32, f32, f32)
    return pl.pallas_call(
        functools.partial(_inproj_kernel, rope, tiles_per_seq, fixed_row),
        grid=(rows // tm,),
        in_specs=in_specs,
        out_specs=[row_spec(w) for w in widths],
        out_shape=[jax.ShapeDtypeStruct((rows, w), d) for w, d in zip(widths, dtypes)],
        compiler_params=_params(("arbitrary",)),
        name="inproj_rope" if rope else "inproj_ctx",
    )(*args)


def _attn_block(sink_ref, q, k_parts, v_parts, prev_ok, next_ok):
    tq = q.shape[0]
    low_half = lax.broadcasted_iota(jnp.int32, (tq, LANES), 1) < HEAD_DIM
    zero = jnp.zeros((tq, LANES), q.dtype)
    outs = []
    for g in range(N_KV_HEADS):
        gs = slice(g * LANES, (g + 1) * LANES)
        pairs = [q[:, (2 * g + i) * LANES:(2 * g + i + 1) * LANES] for i in range(GROUP // 2)]
        q4 = jnp.concatenate(
            [jnp.where(low_half if r % 2 == 0 else ~low_half, pairs[r // 2], zero) for r in range(GROUP)], axis=0)
        kcat = jnp.concatenate([p[:, gs] for p in k_parts], axis=0)
        vcat = jnp.concatenate([p[:, gs] for p in v_parts], axis=0)
        s = lax.dot_general(q4, kcat, (((1,), (1,)), ((), ())), preferred_element_type=f32)
        ps, dens = [], []
        for r0 in range(0, GROUP * tq, ATTN_ROWS):
            sc = s[r0:r0 + ATTN_ROWS]
            if prev_ok is not None:
                rs = slice(r0 % tq, r0 % tq + ATTN_ROWS)
                sc = jnp.concatenate([
                    jnp.where(prev_ok[rs], sc[:, 0:BLOCK], NEG_INF),
                    sc[:, BLOCK:2 * BLOCK],
                    jnp.where(next_ok[rs], sc[:, 2 * BLOCK:3 * BLOCK], NEG_INF),
                    sc[:, 3 * BLOCK:]], axis=1)
            sink = sink_ref[g * GROUP + r0 // tq]
            m = jnp.maximum(jnp.max(sc, axis=-1, keepdims=True), sink)
            p = jnp.exp(sc - m)
            dens.append(jnp.sum(p, axis=-1, keepdims=True) + jnp.exp(sink - m))
            ps.append(p.astype(bf16))
        o = _mm(jnp.concatenate(ps, axis=0), vcat) / jnp.concatenate(dens, axis=0)
        for i in range(GROUP // 2):
            outs.append(jnp.where(low_half, o[2 * i * tq:(2 * i + 1) * tq], o[(2 * i + 1) * tq:(2 * i + 2) * tq]))
    return jnp.concatenate(outs, axis=1)


def _attn_window_kernel(nsteps, sink_ref, q_ref, kp, ko, kn, vp, vo, vn, kc, vc, o_ref):
    j = pl.program_id(1)
    qi = lax.broadcasted_iota(jnp.int32, (BLOCK, BLOCK), 0)
    kj = lax.broadcasted_iota(jnp.int32, (BLOCK, BLOCK), 1)
    first_prev_ok = kj + jnp.where(j > 0, 0, -2 * BLOCK) >= qi
    last_next_ok = kj + jnp.where(j < nsteps - 1, 0, 2 * BLOCK) <= qi
    own = lambda ref: [ref[0, b * BLOCK:(b + 1) * BLOCK] for b in range(ATTN_QBLOCKS)]
    k_blocks = [kp[0]] + own(ko) + [kn[0]]
    v_blocks = [vp[0]] + own(vo) + [vn[0]]
    for b in range(ATTN_QBLOCKS):
        out = _attn_block(sink_ref, q_ref[0, b * BLOCK:(b + 1) * BLOCK],
                          k_blocks[b:b + 3] + [kc[0]], v_blocks[b:b + 3] + [vc[0]],
                          first_prev_ok if b == 0 else kj >= qi,
                          last_next_ok if b == ATTN_QBLOCKS - 1 else kj <= qi)
        o_ref[0, b * BLOCK:(b + 1) * BLOCK, :] = out.astype(bf16)


def _attn_ctx_kernel(sink_ref, q_ref, kc, vc, o_ref):
    o_ref[0] = _attn_block(sink_ref, q_ref[0], [kc[0]], [vc[0]], None, None).astype(bf16)


def _attention(sink, q, k, v, kc, vc):
    bsz, seq, _ = q.shape
    ctx_len = kc.shape[1]
    kvw = kc.shape[2]
    ctx_spec = pl.BlockSpec((1, ctx_len, kvw), lambda b, j: (b, 0, 0))
    sink_spec = pl.BlockSpec(memory_space=pltpu.SMEM)
    if k is None:
        tq, nsteps = seq, 1
        body = _attn_ctx_kernel
        in_specs, args = [ctx_spec, ctx_spec], [kc, vc]
    else:
        tq = ATTN_QBLOCKS * BLOCK
        nsteps = seq // tq
        nblk = seq // BLOCK
        body = functools.partial(_attn_window_kernel, nsteps)
        prev = pl.BlockSpec((1, BLOCK, kvw), lambda b, j: (b, jnp.maximum(ATTN_QBLOCKS * j - 1, 0), 0))
        own = pl.BlockSpec((1, tq, kvw), lambda b, j: (b, j, 0))
        nxt = pl.BlockSpec((1, BLOCK, kvw), lambda b, j: (b, jnp.minimum(ATTN_QBLOCKS * (j + 1), nblk - 1), 0))
        in_specs = [prev, own, nxt, prev, own, nxt, ctx_spec, ctx_spec]
        args = [k, k, k, v, v, v, kc, vc]
    return pl.pallas_call(
        body,
        grid=(bsz, nsteps),
        in_specs=[sink_spec, pl.BlockSpec((1, tq, ATTN_WIDTH), lambda b, j: (b, j, 0))] + in_specs,
        out_specs=pl.BlockSpec((1, tq, ATTN_WIDTH), lambda b, j: (b, j, 0)),
        out_shape=jax.ShapeDtypeStruct((bsz, seq, ATTN_WIDTH), bf16),
        compiler_params=_params(("arbitrary", "arbitrary")),
        name="attn_ctx" if k is None else "attn_window",
    )(sink, q, *args)


SCAN_PAD = 8


def _lru_kernel(seq, ctx_len, ct,
                xl_ref, gl_ref, xc_ref, gc_ref, cw_ref, cb_ref, wa_ref, ba_ref, wx_ref, bx_ref, lam_ref,
                yl_ref, yc_ref,
                xp_ref, u_ref, hf_ref, seg_ref):
    cw = [cw_ref[k:k + 1, :] for k in range(LRU_CONV)]
    cb = cb_ref[...]
    neg_lam = -lam_ref[...]
    softplus_neg_lam = jnp.maximum(neg_lam, 0.0) + jnp.log1p(jnp.exp(-jnp.abs(neg_lam)))

    def conv_into_u(src_ref, n):
        zero = jnp.zeros((SCAN_PAD, ct), f32)
        xp_ref[0:SCAN_PAD] = zero
        xp_ref[SCAN_PAD + n:2 * SCAN_PAD + n] = zero
        for r0 in range(0, n, SCAN_CHUNK):
            xp_ref[SCAN_PAD + r0:SCAN_PAD + r0 + SCAN_CHUNK] = src_ref[0, r0:r0 + SCAN_CHUNK, :]
        for r0 in range(0, n, SCAN_CHUNK):
            acc = jnp.broadcast_to(cb, (SCAN_CHUNK, ct))
            for k in range(LRU_CONV):
                off = SCAN_PAD + r0 + k - LRU_PAD_LEFT
                acc = acc + cw[k] * xp_ref[off:off + SCAN_CHUNK, :]
            u_ref[r0:r0 + SCAN_CHUNK] = acc

    half_scale = (0.5 * LRU_C) * softplus_neg_lam

    def stage(buf, r0, d):
        uc = u_ref[pl.ds(r0, SCAN_CHUNK)]
        ub = uc.astype(bf16)
        half_za = _mm(ub, wa_ref[d]) + ba_ref[d:d + 1]
        half_zx = _mm(ub, wx_ref[d]) + bx_ref[d:d + 1]
        for s in range(SCAN_SEGS):
            rs = slice(s * SCAN_SEG, (s + 1) * SCAN_SEG)
            neg_log_a = (jnp.tanh(half_za[rs]) + 1.0) * half_scale[d:d + 1]
            a = jnp.exp2(neg_log_a * (-LOG2_E))
            y = jnp.tanh(neg_log_a) * (a * a + 1.0)
            root = jnp.where(y > 0.0, y * lax.rsqrt(y), 0.0)
            b = root * ((0.5 * jnp.tanh(half_zx[rs]) + 0.5) * uc[rs])
            lo = s * SCAN_SEG_PITCH
            seg_ref[buf, d, 0, lo:lo + SCAN_SEG] = a
            seg_ref[buf, d, 1, lo:lo + SCAN_SEG] = b

    def scan(buf, d, carry):
        hl = jnp.zeros((SCAN_SEGS, ct), f32)
        pa = jnp.ones((SCAN_SEGS, ct), f32)
        for i in range(SCAN_SEG):
            t = i if d == 0 else SCAN_SEG - 1 - i
            rows = pl.ds(t, SCAN_SEGS, stride=SCAN_SEG_PITCH)
            av = seg_ref[buf, d, 0, rows]
            hl = av * hl + seg_ref[buf, d, 1, rows]
            pa = av * pa
            seg_ref[buf, d, 2, rows] = hl
            seg_ref[buf, d, 3, rows] = pa
        order = range(SCAN_SEGS) if d == 0 else range(SCAN_SEGS - 1, -1, -1)
        pieces = [None] * SCAN_SEGS
        for s in order:
            lo = s * SCAN_SEG_PITCH
            pieces[s] = seg_ref[buf, d, 2, lo:lo + SCAN_SEG] + seg_ref[buf, d, 3, lo:lo + SCAN_SEG] * carry
            carry = pa[s:s + 1] * carry + hl[s:s + 1]
        return jnp.concatenate(pieces, axis=0), carry

    conv_into_u(xc_ref, ctx_len)
    zero_row = jnp.zeros((1, ct), f32)
    assert ctx_len == SCAN_CHUNK
    stage(0, 0, 0)
    stage(0, 0, 1)
    cf, carry_f = scan(0, 0, zero_row)
    cr, carry_r = scan(0, 1, zero_row)
    yc_ref[0] = ((cf + cr) * gc_ref[0]).astype(bf16)

    conv_into_u(xl_ref, seq)
    nchunk = seq // SCAN_CHUNK
    fwd_start = lambda c: pl.multiple_of(c * SCAN_CHUNK, SCAN_CHUNK)
    rev_start = lambda c: pl.multiple_of((nchunk - 1 - c) * SCAN_CHUNK, SCAN_CHUNK)
    stage(0, fwd_start(0), 0)
    stage(0, rev_start(0), 1)

    assert nchunk % 4 == 0

    def emit(first_visit, r0, states):
        if first_visit:
            hf_ref[pl.ds(r0, SCAN_CHUNK)] = states
        else:
            yl_ref[0, pl.ds(r0, SCAN_CHUNK), :] = (
                (states + hf_ref[pl.ds(r0, SCAN_CHUNK)]) * gl_ref[0, pl.ds(r0, SCAN_CHUNK), :]).astype(bf16)

    def pair_body(first_visit, it, carries):
        carry_f, carry_r = carries
        for buf in range(2):
            c = 2 * it + buf
            nxt = jnp.minimum(c + 1, nchunk - 1)
            stage(1 - buf, fwd_start(nxt), 0)
            stage(1 - buf, rev_start(nxt), 1)
            out_f, carry_f = scan(buf, 0, carry_f)
            out_r, carry_r = scan(buf, 1, carry_r)
            emit(first_visit, fwd_start(c), out_f)
            emit(first_visit, rev_start(c), out_r)
        return carry_f, carry_r

    carries = lax.fori_loop(0, nchunk // 4, functools.partial(pair_body, True), (carry_f, carry_r))
    lax.fori_loop(nchunk // 4, nchunk // 2, functools.partial(pair_body, False), carries)


def _rglru(xl, gl, xc, gc, conv_w, conv_b, wa_bd, ba, wx_bd, bx, lam, ct):
    bsz, seq, _ = xl.shape
    ctx_len = xc.shape[1]
    seg_rows = SCAN_SEGS * SCAN_SEG_PITCH
    col = lambda rows: pl.BlockSpec((1, rows, ct), lambda b, c: (b, 0, c))
    vec = lambda rows: pl.BlockSpec((rows, ct), lambda b, c: (0, c))
    bd = pl.BlockSpec((2, ct, ct), lambda b, c: (0, c, c))
    return pl.pallas_call(
        functools.partial(_lru_kernel, seq, ctx_len, ct),
        grid=(bsz, LRU_WIDTH // ct),
        in_specs=[col(seq), col(seq), col(ctx_len), col(ctx_len),
                  vec(LRU_CONV), vec(1), bd, vec(2), bd, vec(2), vec(2)],
        out_specs=[col(seq), col(ctx_len)],
        out_shape=[jax.ShapeDtypeStruct((bsz, seq, LRU_WIDTH), bf16),
                   jax.ShapeDtypeStruct((bsz, ctx_len, LRU_WIDTH), bf16)],
        scratch_shapes=[pltpu.VMEM((seq + 2 * SCAN_PAD, ct), f32), pltpu.VMEM((seq, ct), f32),
                        pltpu.VMEM((seq, ct), f32), pltpu.VMEM((2, 2, 4, seg_rows, ct), f32)],
        compiler_params=_params(("arbitrary", "arbitrary")),
        name="rglru",
    )(xl, gl, xc, gc, conv_w, conv_b.reshape(1, LRU_WIDTH), wa_bd, ba, wx_bd, bx, lam)


def _conv_ln_swish(tt, first, last, x_ref, prev_rows, next_rows, w_ref, b_ref, g_ref, beta_ref, win_ref, y_ref):
    lane_tiles = [slice(c * LANES, (c + 1) * LANES) for c in range(CONV_WIDTH // LANES)]
    zero = jnp.zeros((CONV_HALO, LANES), f32)
    for c, cs in enumerate(lane_tiles):
        win_ref[c, CONV_HALO:CONV_HALO + tt] = x_ref[:, cs]

    def fill(lo, rows):
        for c, cs in enumerate(lane_tiles):
            win_ref[c, lo:lo + CONV_HALO] = zero if rows is None else rows(cs)
    pl.when(jnp.logical_not(first))(lambda: fill(0, prev_rows))
    pl.when(first)(lambda: fill(0, None))
    pl.when(jnp.logical_not(last))(lambda: fill(CONV_HALO + tt, next_rows))
    pl.when(last)(lambda: fill(CONV_HALO + tt, None))

    gain, beta = g_ref[...], beta_ref[...]

    def rows(r0):
        pieces = []
        for c, cs in enumerate(lane_tiles):
            acc = jnp.broadcast_to(b_ref[:, cs], (CONV_ROWS, LANES))
            for k in range(CONV_KERNEL):
                off = r0 + CONV_HALO - CONV_PAD + k
                acc = acc + jnp.broadcast_to(w_ref[k:k + 1, cs], (CONV_ROWS, LANES)) * win_ref[c, off:off + CONV_ROWS, :]
            pieces.append(acc)
        acc = jnp.concatenate(pieces, axis=1)
        mu = jnp.mean(acc, axis=-1, keepdims=True)
        xc = acc - mu
        var = jnp.mean(xc * xc, axis=-1, keepdims=True)
        y = xc * lax.rsqrt(var + EPS) * gain + beta
        y = (y * jax.nn.sigmoid(y)).astype(bf16)
        y_ref[r0:r0 + CONV_ROWS, :] = y
        return y
    return rows


def _tail_kernel(final, tiles_per_seq, fixed_row,
                 one_ref, h_ref, ya_ref, hc_ref, hc_next_ref, hc_after_ref, yc_ref, mod_ref, g1_ref, g2_ref, gf_ref,
                 cw_ref, cb_ref, cg_ref, cbeta_ref,
                 wg_ref, woa_ref, wob_ref, woc_ref, wout_ref, wup_ref, wdn_ref, o_ref,
                 act_ref, win_ref, yb_ref, yb_next_ref):
    i = pl.program_id(0)
    tm = h_ref.shape[0]
    conv = functools.partial(_conv_ln_swish, tm, w_ref=cw_ref, b_ref=cb_ref, g_ref=cg_ref, beta_ref=cbeta_ref,
                             win_ref=win_ref, y_ref=yb_next_ref)
    head_of_next = lambda cs: hc_next_ref[0:CONV_HALO, cs]

    @pl.when(i == 0)
    def _():
        rows = conv(True, tiles_per_seq == 1, hc_ref, None, head_of_next)
        for r0 in range(0, tm, CONV_ROWS):
            rows(r0)
    yb_ref[...] = yb_next_ref[...]
    next_in_seq = (i + 1) % tiles_per_seq
    conv_rows = conv(next_in_seq == 0, next_in_seq == tiles_per_seq - 1, hc_next_ref,
                     lambda cs: hc_ref[tm - CONV_HALO:tm, cs], lambda cs: hc_after_ref[:, cs])
    row = fixed_row if fixed_row is not None else i // tiles_per_seq
    mod = lambda n: mod_ref[pl.ds(row, 1), n * D_MODEL:(n + 1) * D_MODEL]
    x = h_ref[...]
    a = _norm_mod(x, g1_ref[...], mod(0), mod(1)).astype(bf16)
    col_blocks = [slice(n0, n0 + OUT_CHUNK) for n0 in range(0, D_MODEL, OUT_CHUNK)]
    branches = ((ya_ref, woa_ref), (yb_ref, wob_ref), (yc_ref, woc_ref))
    merged = []
    for cs in col_blocks:
        total = None
        for n, (y_ref, w_ref) in enumerate(branches):
            gate = jax.nn.sigmoid(_mm(a, wg_ref[:, n * D_MODEL + cs.start:n * D_MODEL + cs.stop]))
            term = gate * _mm(y_ref[...], w_ref[:, cs])
            total = term if total is None else total + term
        merged.append(total.astype(bf16))
    h1 = x + mod(2) * _mm(jnp.concatenate(merged, axis=1), wout_ref[...])
    a2 = _norm_mod(h1, g2_ref[...], mod(3), mod(4)).astype(bf16)

    always = one_ref[0] == 1
    nffn = FFN_HIDDEN // FFN_CHUNK
    nconv = tm // CONV_ROWS
    for j, c0 in enumerate(range(0, FFN_HIDDEN, FFN_CHUNK)):
        up = _mm(a2, wup_ref[:, c0:c0 + FFN_CHUNK])
        gate = _mm(a2, wup_ref[:, FFN_HIDDEN + c0:FFN_HIDDEN + c0 + FFN_CHUNK])
        act = ((gate * jax.nn.sigmoid(gate)) * up).astype(bf16)
        pieces, done = [], 0
        for rc in range(j * nconv // nffn, (j + 1) * nconv // nffn):
            r0 = rc * CONV_ROWS
            y = conv_rows(r0)
            pieces += [act[done:r0]] * (r0 > done) + [jnp.where(always, act[r0:r0 + CONV_ROWS], y[:, 0:FFN_CHUNK])]
            done = r0 + CONV_ROWS
        if pieces:
            act = jnp.concatenate(pieces + [act[done:]] * (done < tm), axis=0)
        act_ref[:, c0:c0 + FFN_CHUNK] = act
    gate2 = mod(5)
    h2 =[h1[:, cs] + gate2[:, cs] * _mm(act_ref[...], wdn_ref[:, cs]) for cs in col_blocks]
    if final:
        sum_sq = functools.reduce(lambda u, v: u + v, [jnp.sum(p * p, axis=-1, keepdims=True) for p in h2])
        inv_rms = lax.rsqrt(sum_sq * (1.0 / D_MODEL) + EPS)
        h2 = [p * inv_rms * gf_ref[:, cs] for p, cs in zip(h2, col_blocks)]
    for p, cs in zip(h2, col_blocks):
        o_ref[:, cs] = p


def _tail(h, ya, hc, yc, mod, g1, g2, gf, conv, wts, tm, tiles_per_seq, fixed_row, final):
    rows = h.shape[0]
    row_spec = lambda width: pl.BlockSpec((tm, width), lambda i: (i, 0))
    ntiles = rows // tm
    hpt = tm // CONV_HALO
    nhalo = rows // CONV_HALO
    wg, woa, wob, woc, wout, wup, wdn = wts
    cw, cb, cg, cbeta = conv
    vec = lambda v: v.reshape(1, -1)
    return pl.pallas_call(
        functools.partial(_tail_kernel, final, tiles_per_seq, fixed_row),
        grid=(rows // tm,),
        in_specs=[pl.BlockSpec(memory_space=pltpu.SMEM),
                  row_spec(D_MODEL), row_spec(ATTN_WIDTH), row_spec(CONV_WIDTH),
                  pl.BlockSpec((tm, CONV_WIDTH), lambda i: (jnp.minimum(i + 1, ntiles - 1), 0)),
                  pl.BlockSpec((CONV_HALO, CONV_WIDTH), lambda i: (jnp.minimum((i + 2) * hpt, nhalo - 1), 0)),
                  row_spec(LRU_WIDTH),
                  _const_spec((MOD_ROWS, 6 * D_MODEL)),
                  _const_spec((1, D_MODEL)), _const_spec((1, D_MODEL)), _const_spec((1, D_MODEL)),
                  _const_spec(cw.shape), _const_spec((1, CONV_WIDTH)), _const_spec((1, CONV_WIDTH)),
                  _const_spec((1, CONV_WIDTH)),
                  _const_spec(wg.shape), _const_spec(woa.shape), _const_spec(wob.shape),
                  _const_spec(woc.shape), _const_spec(wout.shape), _const_spec(wup.shape),
                  _const_spec(wdn.shape)],
        out_specs=row_spec(D_MODEL),
        out_shape=jax.ShapeDtypeStruct((rows, D_MODEL), f32),
        scratch_shapes=[pltpu.VMEM((tm, FFN_HIDDEN), bf16),
                        pltpu.VMEM((CONV_WIDTH // LANES, tm + 2 * CONV_HALO, LANES), f32),
                        pltpu.VMEM((tm, CONV_WIDTH), bf16), pltpu.VMEM((tm, CONV_WIDTH), bf16)],
        compiler_params=_params(("arbitrary",)),
        name="merge_out_swiglu",
    )(jnp.ones((1,), jnp.int32), h, ya, hc, hc, hc, yc, mod, g1, g2, gf, cw, vec(cb), vec(cg), vec(cbeta),
      wg, woa, wob, woc, wout, wup, wdn)


def _rope_tables(seq):
    rows = seq // GRID_W
    inv = jnp.power(ROPE_BASE, -jnp.arange(ROPE_FREQS, dtype=f32) / ROPE_FREQS)
    row_ang = jnp.arange(rows, dtype=f32)[:, None] * inv[None]
    col_ang = jnp.arange(GRID_W, dtype=f32)[:, None] * inv[None]

    def table(fn):
        by_row = jnp.broadcast_to(fn(row_ang)[:, None, :], (rows, GRID_W, ROPE_FREQS))
        by_col = jnp.broadcast_to(fn(col_ang)[None, :, :], (rows, GRID_W, ROPE_FREQS))
        return jnp.concatenate([by_row, by_col], axis=-1).reshape(seq, 2 * ROPE_FREQS)
    cos, sin = table(jnp.cos), table(jnp.sin)
    reps = LANES // HEAD_DIM
    return jnp.tile(jnp.concatenate([cos, cos], axis=-1), (1, reps)), \
        jnp.tile(jnp.concatenate([-sin, sin], axis=-1), (1, reps))


def _block_diag(w):
    two, nb, d, e = w.shape
    eye = jnp.eye(nb, dtype=w.dtype)
    return jnp.einsum('xnde,nm->xndme', w, eye).reshape(two, nb * d, nb * e)


def kernel(x, c, ctx, c_ctx, mod_w, mod_b, norm1_g, norm2_g, w_in, attn_sink, conv_dw_w, conv_dw_b, conv_ln_g,
           conv_ln_b, lru_conv_w, lru_conv_b, lru_wa, lru_ba, lru_wx, lru_bx, lru_lam, w_o_attn, w_o_conv,
           w_o_lru, w_out, ffn_w_up, ffn_w_down, final_norm_g):
    bsz, seq, _ = x.shape
    ctx_len = ctx.shape[1]
    depth = mod_w.shape[0]
    assert bsz + 1 <= MOD_ROWS and seq % (ATTN_QBLOCKS * BLOCK) == 0 and ctx_len == SCAN_CHUNK
    tm = 512
    tiles_per_seq = seq // tm
    tm_ctx = ctx_len
    ctx_row = bsz

    cvec = jnp.zeros((MOD_ROWS, D_MODEL), f32).at[:bsz].set(c).at[ctx_row].set(c_ctx)
    mod_all = _modulation(cvec, mod_w, mod_b)
    cos, sin = _rope_tables(seq)
    row = lambda v: v.reshape(1, -1)

    h_lat = x.reshape(bsz * seq, D_MODEL)
    h_ctx = ctx.reshape(bsz * ctx_len, D_MODEL)
    for l in range(depth):
        need_ctx = l < depth - 1
        mod = mod_all[l]
        w_mix = w_in[l, :, :MIX_COLS].astype(bf16)
        wts = (
            w_in[l, :, MIX_COLS:].astype(bf16),
            w_o_attn[l].astype(bf16), w_o_conv[l].astype(bf16), w_o_lru[l].astype(bf16),
            w_out[l].astype(bf16),
            ffn_w_up[l].astype(bf16),
            ffn_w_down[l].astype(bf16),
        )
        g1, g2 = row(norm1_g[l]), row(norm2_g[l])

        q, k, v, hc, lx, gl = _inproj(h_lat, mod, g1, w_mix, tm, tiles_per_seq, None, cos, sin)
        qc, kc, vc, hcc, lxc, glc = _inproj(h_ctx, mod, g1, w_mix, tm_ctx, 1, ctx_row)
        shp = lambda t, n: t.reshape(bsz, n, t.shape[-1])
        kc3, vc3 = shp(kc, ctx_len), shp(vc, ctx_len)

        y_attn = _attention(attn_sink[l], shp(q, seq), shp(k, seq), shp(v, seq), kc3, vc3)
        conv = (conv_dw_w[l], conv_dw_b[l], conv_ln_g[l], conv_ln_b[l])
        y_lru, y_lru_c = _rglru(shp(lx, seq), shp(gl, seq), shp(lxc, ctx_len), shp(glc, ctx_len),
                                lru_conv_w[l], lru_conv_b[l],
                                _block_diag(0.5 * lru_wa[l]).astype(bf16), 0.5 * lru_ba[l],
                                _block_diag(0.5 * lru_wx[l]).astype(bf16), 0.5 * lru_bx[l], lru_lam[l], LANES)
        flat = lambda t: t.reshape(-1, t.shape[-1])
        h_lat = _tail(h_lat, flat(y_attn), hc, flat(y_lru), mod, g1, g2, row(final_norm_g), conv, wts,
                      tm, tiles_per_seq, None, final=not need_ctx)
        if need_ctx:
            y_attn_c = _attention(attn_sink[l], shp(qc, ctx_len), None, None, kc3, vc3)
            h_ctx = _tail(h_ctx, flat(y_attn_c), hcc, flat(y_lru_c), mod, g1, g2,
                          row(final_norm_g), conv, wts, tm_ctx, 1, ctx_row, final=False)
    return h_lat.reshape(bsz, seq, D_MODEL)
```

```python
import functools

import jax
import jax.numpy as jnp
from jax import lax
from jax.experimental import pallas as pl
from jax.experimental.pallas import tpu as pltpu

D_MODEL = 1024
GRID_W = 64
N_HEADS = 8
N_KV_HEADS = 2
GROUP = N_HEADS // N_KV_HEADS
HEAD_DIM = 64
ATTN_WIDTH = N_HEADS * HEAD_DIM
KV_WIDTH = N_KV_HEADS * HEAD_DIM
BLOCK = 128
ATTN_SCALE = HEAD_DIM ** -0.5
LOG2_E = 1.4426950408889634
ROPE_BASE = 10000.0
ROPE_FREQS = HEAD_DIM // 4
CONV_WIDTH = 512
CONV_KERNEL = 31
CONV_PAD = (CONV_KERNEL - 1) // 2
LRU_WIDTH = 512
LRU_BLOCKS = 8
LRU_BLOCK_DIM = LRU_WIDTH // LRU_BLOCKS
LRU_CONV = 4
LRU_PAD_LEFT = 2
LRU_C = 8.0
FFN_HIDDEN = 2816
MIX_COLS = ATTN_WIDTH + 2 * KV_WIDTH + 2 * CONV_WIDTH + 2 * LRU_WIDTH
EPS = 1e-6
NEG_INF = -1e30

LANES = 128
SUBLANES = 8
VMEM_LIMIT_BYTES = 56 * 1024 * 1024

MOD_ROWS = 8
FFN_CHUNK = 256
OUT_CHUNK = 256
ATTN_ROWS = 32
ATTN_QBLOCKS = 4
CONV_HALO = 16
CONV_ROWS = 32
LRU_TILE = 128
SCAN_CHUNK = 256
SCAN_SEGS = SUBLANES
SCAN_SEG = SCAN_CHUNK // SCAN_SEGS
SCAN_SEG_PITCH = SCAN_SEG + 8

f32 = jnp.float32
bf16 = jnp.bfloat16


def _mm(a, b):
    return jnp.dot(a, b, preferred_element_type=f32)


def _const_spec(shape):
    n = len(shape)
    return pl.BlockSpec(shape, lambda *_: (0,) * n, pipeline_mode=pl.Buffered(1))


def _params(sem):
    return pltpu.CompilerParams(dimension_semantics=sem, vmem_limit_bytes=VMEM_LIMIT_BYTES)


def _norm_mod(x, g, shift, scale):
    y = x * lax.rsqrt(jnp.mean(x * x, axis=-1, keepdims=True) + EPS)
    return (y * g) * (1.0 + scale) + shift


def _mod_kernel(c_ref, w_ref, b_ref, o_ref):
    c = c_ref[...]
    s = (c * jax.nn.sigmoid(c)).astype(bf16)
    o_ref[0] = _mm(s, w_ref[0].astype(bf16)) + b_ref[0]


def _modulation(cvec, mod_w, mod_b):
    depth, _, cols = mod_w.shape
    tn = 1024
    return pl.pallas_call(
        _mod_kernel,
        grid=(depth, cols // tn),
        in_specs=[
            pl.BlockSpec((MOD_ROWS, D_MODEL), lambda l, j: (0, 0)),
            pl.BlockSpec((1, D_MODEL, tn), lambda l, j: (l, 0, j)),
            pl.BlockSpec((1, 1, tn), lambda l, j: (l, 0, j)),
        ],
        out_specs=pl.BlockSpec((1, MOD_ROWS, tn), lambda l, j: (l, 0, j)),
        out_shape=jax.ShapeDtypeStruct((depth, MOD_ROWS, cols), f32),
        compiler_params=_params(("arbitrary", "arbitrary")),
        name="modulation",
    )(cvec, mod_w, mod_b.reshape(depth, 1, cols))


def _rope(z, cos, sin_signed):
    lane = lax.broadcasted_iota(jnp.int32, z.shape, 1)
    first_half = (lane & (HEAD_DIM - 1)) < HEAD_DIM // 2
    partner = jnp.where(first_half, pltpu.roll(z, LANES - HEAD_DIM // 2, axis=1),
                        pltpu.roll(z, HEAD_DIM // 2, axis=1))
    return z * cos + partner * sin_signed


def _dup_heads(z):
    low_half = lax.broadcasted_iota(jnp.int32, z.shape, 1) < HEAD_DIM
    swapped = pltpu.roll(z, HEAD_DIM, axis=1)
    return jnp.concatenate([jnp.where(low_half, z, swapped), jnp.where(low_half, swapped, z)], axis=1)


def _inproj_kernel(rope, tiles_per_seq, fixed_row, h_ref, mod_ref, g_ref, w_ref, *rest):
    if rope:
        cos_ref, sin_ref, q_ref, k_ref, v_ref, hc_ref, lx_ref, gl_ref = rest
    else:
        q_ref, k_ref, v_ref, hc_ref, lx_ref, gl_ref = rest
    row = fixed_row if fixed_row is not None else pl.program_id(0) // tiles_per_seq
    shift = mod_ref[pl.ds(row, 1), 0:D_MODEL]
    scale = mod_ref[pl.ds(row, 1), D_MODEL:2 * D_MODEL]
    a = _norm_mod(h_ref[...], g_ref[...], shift, scale).astype(bf16)

    if rope:
        cos, sin = cos_ref[...], sin_ref[...]
        fix = lambda z: _rope(z, cos, sin)
    else:
        fix = lambda z: z
    zq = _mm(a, w_ref[:, 0:ATTN_WIDTH])
    for j in range(ATTN_WIDTH // LANES):
        zj = fix(zq[:, j * LANES:(j + 1) * LANES]) * ATTN_SCALE
        q_ref[:, j * LANES:(j + 1) * LANES] = zj.astype(bf16)
    c0 = ATTN_WIDTH
    zkv = _mm(a, w_ref[:, c0:c0 + 2 * KV_WIDTH])
    k_ref[...] = _dup_heads(fix(zkv[:, 0:KV_WIDTH])).astype(bf16)
    v_ref[...] = _dup_heads(zkv[:, KV_WIDTH:]).astype(bf16)
    c0 += 2 * KV_WIDTH
    val = _mm(a, w_ref[:, c0:c0 + CONV_WIDTH])
    gate = _mm(a, w_ref[:, c0 + CONV_WIDTH:c0 + 2 * CONV_WIDTH])
    hc_ref[...] = (val * jax.nn.sigmoid(gate)).astype(bf16)
    c0 += 2 * CONV_WIDTH
    lx_ref[...] = _mm(a, w_ref[:, c0:c0 + LRU_WIDTH]).astype(bf16)
    gl_ref[...] = jax.nn.gelu(_mm(a, w_ref[:, c0 + LRU_WIDTH:c0 + 2 * LRU_WIDTH])).astype(bf16)


def _inproj(h, mod, g, w_mix, tm, tiles_per_seq, fixed_row, cos=None, sin=None):
    rows = h.shape[0]
    rope = cos is not None
    row_spec = lambda width: pl.BlockSpec((tm, width), lambda i: (i, 0))
    in_specs = [row_spec(D_MODEL), _const_spec((MOD_ROWS, 6 * D_MODEL)), _const_spec((1, D_MODEL)),
                _const_spec((D_MODEL, MIX_COLS))]
    args = [h, mod, g, w_mix]
    if rope:
        tab = pl.BlockSpec((tm, LANES), lambda i: (i % tiles_per_seq, 0))
        in_specs += [tab, tab]
        args += [cos, sin]
    widths = (ATTN_WIDTH, 2 * KV_WIDTH, 2 * KV_WIDTH, CONV_WIDTH, LRU_WIDTH, LRU_WIDTH)
    return pl.pallas_call(
        functools.partial(_inproj_kernel, rope, tiles_per_seq, fixed_row),
        grid=(rows // tm,),
        in_specs=in_specs,
        out_specs=[row_spec(w) for w in widths],
        out_shape=[jax.ShapeDtypeStruct((rows, w), bf16) for w in widths],
        compiler_params=_params(("arbitrary",)),
        name="inproj_rope" if rope else "inproj_ctx",
    )(*args)


def _attn_block(sink_ref, q, k_parts, v_parts, prev_ok, next_ok):
    tq = q.shape[0]
    low_half = lax.broadcasted_iota(jnp.int32, (tq, LANES), 1) < HEAD_DIM
    zero = jnp.zeros((tq, LANES), q.dtype)
    outs = []
    for g in range(N_KV_HEADS):
        gs = slice(g * LANES, (g + 1) * LANES)
        pairs = [q[:, (2 * g + i) * LANES:(2 * g + i + 1) * LANES] for i in range(GROUP // 2)]
        q4 = jnp.concatenate(
            [jnp.where(low_half if r % 2 == 0 else ~low_half, pairs[r // 2], zero) for r in range(GROUP)], axis=0)
        kcat = jnp.concatenate([p[:, gs] for p in k_parts], axis=0)
        vcat = jnp.concatenate([p[:, gs] for p in v_parts], axis=0)
        s = lax.dot_general(q4, kcat, (((1,), (1,)), ((), ())), preferred_element_type=f32)
        ps, dens = [], []
        for r0 in range(0, GROUP * tq, ATTN_ROWS):
            sc = s[r0:r0 + ATTN_ROWS]
            if prev_ok is not None:
                rs = slice(r0 % tq, r0 % tq + ATTN_ROWS)
                sc = jnp.concatenate([
                    jnp.where(prev_ok[rs], sc[:, 0:BLOCK], NEG_INF),
                    sc[:, BLOCK:2 * BLOCK],
                    jnp.where(next_ok[rs], sc[:, 2 * BLOCK:3 * BLOCK], NEG_INF),
                    sc[:, 3 * BLOCK:]], axis=1)
            sink = sink_ref[g * GROUP + r0 // tq]
            m = jnp.maximum(jnp.max(sc, axis=-1, keepdims=True), sink)
            p = jnp.exp(sc - m)
            dens.append(jnp.sum(p, axis=-1, keepdims=True) + jnp.exp(sink - m))
            ps.append(p.astype(bf16))
        o = _mm(jnp.concatenate(ps, axis=0), vcat) / jnp.concatenate(dens, axis=0)
        for i in range(GROUP // 2):
            outs.append(jnp.where(low_half, o[2 * i * tq:(2 * i + 1) * tq], o[(2 * i + 1) * tq:(2 * i + 2) * tq]))
    return jnp.concatenate(outs, axis=1)


def _attn_window_kernel(nsteps, sink_ref, q_ref, kp, ko, kn, vp, vo, vn, kc, vc, o_ref):
    j = pl.program_id(1)
    qi = lax.broadcasted_iota(jnp.int32, (BLOCK, BLOCK), 0)
    kj = lax.broadcasted_iota(jnp.int32, (BLOCK, BLOCK), 1)
    first_prev_ok = kj + jnp.where(j > 0, 0, -2 * BLOCK) >= qi
    last_next_ok = kj + jnp.where(j < nsteps - 1, 0, 2 * BLOCK) <= qi
    own = lambda ref: [ref[0, b * BLOCK:(b + 1) * BLOCK] for b in range(ATTN_QBLOCKS)]
    k_blocks = [kp[0]] + own(ko) + [kn[0]]
    v_blocks = [vp[0]] + own(vo) + [vn[0]]
    for b in range(ATTN_QBLOCKS):
        out = _attn_block(sink_ref, q_ref[0, b * BLOCK:(b + 1) * BLOCK],
                          k_blocks[b:b + 3] + [kc[0]], v_blocks[b:b + 3] + [vc[0]],
                          first_prev_ok if b == 0 else kj >= qi,
                          last_next_ok if b == ATTN_QBLOCKS - 1 else kj <= qi)
        o_ref[0, b * BLOCK:(b + 1) * BLOCK, :] = out.astype(bf16)


def _attn_ctx_kernel(sink_ref, q_ref, kc, vc, o_ref):
    o_ref[0] = _attn_block(sink_ref, q_ref[0], [kc[0]], [vc[0]], None, None).astype(bf16)


def _attention(sink, q, k, v, kc, vc):
    bsz, seq, _ = q.shape
    ctx_len = kc.shape[1]
    kvw = kc.shape[2]
    ctx_spec = pl.BlockSpec((1, ctx_len, kvw), lambda b, j: (b, 0, 0))
    sink_spec = pl.BlockSpec(memory_space=pltpu.SMEM)
    if k is None:
        tq, nsteps = seq, 1
        body = _attn_ctx_kernel
        in_specs, args = [ctx_spec, ctx_spec], [kc, vc]
    else:
        tq = ATTN_QBLOCKS * BLOCK
        nsteps = seq // tq
        nblk = seq // BLOCK
        body = functools.partial(_attn_window_kernel, nsteps)
        prev = pl.BlockSpec((1, BLOCK, kvw), lambda b, j: (b, jnp.maximum(ATTN_QBLOCKS * j - 1, 0), 0))
        own = pl.BlockSpec((1, tq, kvw), lambda b, j: (b, j, 0))
        nxt = pl.BlockSpec((1, BLOCK, kvw), lambda b, j: (b, jnp.minimum(ATTN_QBLOCKS * (j + 1), nblk - 1), 0))
        in_specs = [prev, own, nxt, prev, own, nxt, ctx_spec, ctx_spec]
        args = [k, k, k, v, v, v, kc, vc]
    return pl.pallas_call(
        body,
        grid=(bsz, nsteps),
        in_specs=[sink_spec, pl.BlockSpec((1, tq, ATTN_WIDTH), lambda b, j: (b, j, 0))] + in_specs,
        out_specs=pl.BlockSpec((1, tq, ATTN_WIDTH), lambda b, j: (b, j, 0)),
        out_shape=jax.ShapeDtypeStruct((bsz, seq, ATTN_WIDTH), bf16),
        compiler_params=_params(("arbitrary", "arbitrary")),
        name="attn_ctx" if k is None else "attn_window",
    )(sink, q, *args)


SCAN_PAD = 8


def _lru_kernel(seq, ctx_len, ct,
                xl_ref, gl_ref, xc_ref, gc_ref, cw_ref, cb_ref, wa_ref, ba_ref, wx_ref, bx_ref, lam_ref,
                yl_ref, yc_ref,
                xp_ref, u_ref, hf_ref, seg_ref, run_ref):
    lane_tiles = [slice(c * LANES, (c + 1) * LANES) for c in range(ct // LANES)]
    neg_lam = -lam_ref[...]
    softplus_neg_lam = jnp.maximum(neg_lam, 0.0) + jnp.log1p(jnp.exp(-jnp.abs(neg_lam)))

    def conv_into_u(src_ref, n):
        zero = jnp.zeros((SCAN_PAD, LANES), f32)
        for c in range(ct // LANES):
            cs = slice(c * LANES, (c + 1) * LANES)
            xp_ref[c, 0:SCAN_PAD] = zero
            xp_ref[c, SCAN_PAD + n:2 * SCAN_PAD + n] = zero
            for r0 in range(0, n, SCAN_CHUNK):
                xp_ref[c, SCAN_PAD + r0:SCAN_PAD + r0 + SCAN_CHUNK] = src_ref[0, r0:r0 + SCAN_CHUNK, cs].astype(f32)
            for r0 in range(0, n, SCAN_CHUNK):
                acc = jnp.broadcast_to(cb_ref[:, cs], (SCAN_CHUNK, LANES))
                for k in range(LRU_CONV):
                    off = SCAN_PAD + r0 + k - LRU_PAD_LEFT
                    acc = acc + cw_ref[k:k + 1, cs] * xp_ref[c, off:off + SCAN_CHUNK, :]
                u_ref[r0:r0 + SCAN_CHUNK, cs] = acc

    half_scale = (0.5 * LRU_C) * softplus_neg_lam

    def stage(buf, r0, d):
        uc = u_ref[pl.ds(r0, SCAN_CHUNK)]
        ub = uc.astype(bf16)
        half_za = _mm(ub, wa_ref[d]) + ba_ref[d:d + 1]
        half_zx = _mm(ub, wx_ref[d]) + bx_ref[d:d + 1]
        for s in range(SCAN_SEGS):
            rs = slice(s * SCAN_SEG, (s + 1) * SCAN_SEG)
            neg_log_a = (jnp.tanh(half_za[rs]) + 1.0) * half_scale[d:d + 1]
            a = jnp.exp2(neg_log_a * (-LOG2_E))
            y = jnp.tanh(neg_log_a) * (a * a + 1.0)
            root = jnp.where(y > 0.0, y * lax.rsqrt(y), 0.0)
            b = root * ((0.5 * jnp.tanh(half_zx[rs]) + 0.5) * uc[rs])
            lo = s * SCAN_SEG_PITCH
            for c, cs in enumerate(lane_tiles):
                seg_ref[buf, d, 0, c, lo:lo + SCAN_SEG] = a[:, cs]
                seg_ref[buf, d, 1, c, lo:lo + SCAN_SEG] = b[:, cs]

    def scan(buf, d, carry):
        hl = [jnp.zeros((SCAN_SEGS, LANES), f32) for _ in lane_tiles]
        pa = [jnp.ones((SCAN_SEGS, LANES), f32) for _ in lane_tiles]
        for i in range(SCAN_SEG):
            t = i if d == 0 else SCAN_SEG - 1 - i
            rows = pl.ds(t, SCAN_SEGS, stride=SCAN_SEG_PITCH)
            for c in range(len(lane_tiles)):
                av = seg_ref[buf, d, 0, c, rows]
                hl[c] = av * hl[c] + seg_ref[buf, d, 1, c, rows]
                pa[c] = av * pa[c]
                run_ref[d, 0, c, rows] = hl[c]
                run_ref[d, 1, c, rows] = pa[c]
        order = range(SCAN_SEGS) if d == 0 else range(SCAN_SEGS - 1, -1, -1)
        states, carries = [], []
        for c, cs in enumerate(lane_tiles):
            pieces, cur = [None] * SCAN_SEGS, carry[:, cs]
            for s in order:
                lo = s * SCAN_SEG_PITCH
                pieces[s] = run_ref[d, 0, c, lo:lo + SCAN_SEG] + run_ref[d, 1, c, lo:lo + SCAN_SEG] * cur
                cur = pa[c][s:s + 1] * cur + hl[c][s:s + 1]
            states.append(jnp.concatenate(pieces, axis=0))
            carries.append(cur)
        return jnp.concatenate(states, axis=1), jnp.concatenate(carries, axis=1)

    conv_into_u(xc_ref, ctx_len)
    zero_row = jnp.zeros((1, ct), f32)
    assert ctx_len == SCAN_CHUNK
    stage(0, 0, 0)
    stage(0, 0, 1)
    cf, carry_f = scan(0, 0, zero_row)
    cr, carry_r = scan(0, 1, zero_row)
    yc_ref[0] = ((cf + cr) * gc_ref[0].astype(f32)).astype(bf16)

    conv_into_u(xl_ref, seq)
    nchunk = seq // SCAN_CHUNK
    fwd_start = lambda c: pl.multiple_of(c * SCAN_CHUNK, SCAN_CHUNK)
    rev_start = lambda c: pl.multiple_of((nchunk - 1 - c) * SCAN_CHUNK, SCAN_CHUNK)
    stage(0, fwd_start(0), 0)
    stage(0, rev_start(0), 1)

    assert nchunk % 4 == 0

    def emit(first_visit, r0, states):
        if first_visit:
            hf_ref[pl.ds(r0, SCAN_CHUNK)] = states
        else:
            yl_ref[0, pl.ds(r0, SCAN_CHUNK), :] = (
                (states + hf_ref[pl.ds(r0, SCAN_CHUNK)])
                * gl_ref[0, pl.ds(r0, SCAN_CHUNK), :].astype(f32)).astype(bf16)

    def pair_body(first_visit, it, carries):
        carry_f, carry_r = carries
        for buf in range(2):
            c = 2 * it + buf
            nxt = jnp.minimum(c + 1, nchunk - 1)
            stage(1 - buf, fwd_start(nxt), 0)
            stage(1 - buf, rev_start(nxt), 1)
            out_f, carry_f = scan(buf, 0, carry_f)
            out_r, carry_r = scan(buf, 1, carry_r)
            emit(first_visit, fwd_start(c), out_f)
            emit(first_visit, rev_start(c), out_r)
        return carry_f, carry_r

    carries = lax.fori_loop(0, nchunk // 4, functools.partial(pair_body, True), (carry_f, carry_r))
    lax.fori_loop(nchunk // 4, nchunk // 2, functools.partial(pair_body, False), carries)


def _rglru(xl, gl, xc, gc, conv_w, conv_b, wa_bd, ba, wx_bd, bx, lam, ct):
    bsz, seq, _ = xl.shape
    ctx_len = xc.shape[1]
    seg_rows = SCAN_SEGS * SCAN_SEG_PITCH
    col = lambda rows: pl.BlockSpec((1, rows, ct), lambda b, c: (b, 0, c))
    vec = lambda rows: pl.BlockSpec((rows, ct), lambda b, c: (0, c))
    bd = pl.BlockSpec((2, ct, ct), lambda b, c: (0, c, c))
    return pl.pallas_call(
        functools.partial(_lru_kernel, seq, ctx_len, ct),
        grid=(bsz, LRU_WIDTH // ct),
        in_specs=[col(seq), col(seq), col(ctx_len), col(ctx_len),
                  vec(LRU_CONV), vec(1), bd, vec(2), bd, vec(2), vec(2)],
        out_specs=[col(seq), col(ctx_len)],
        out_shape=[jax.ShapeDtypeStruct((bsz, seq, LRU_WIDTH), bf16),
                   jax.ShapeDtypeStruct((bsz, ctx_len, LRU_WIDTH), bf16)],
        scratch_shapes=[pltpu.VMEM((ct // LANES, seq + 2 * SCAN_PAD, LANES), f32), pltpu.VMEM((seq, ct), f32),
                        pltpu.VMEM((seq, ct), f32), pltpu.VMEM((2, 2, 2, ct // LANES, seg_rows, LANES), f32),
                        pltpu.VMEM((2, 2, ct // LANES, seg_rows, LANES), f32)],
        compiler_params=_params(("arbitrary", "arbitrary")),
        name="rglru",
    )(xl, gl, xc, gc, conv_w, conv_b.reshape(1, LRU_WIDTH), wa_bd, ba, wx_bd, bx, lam)


def _conv_ln_swish(tt, first, last, x_ref, prev_rows, next_rows, w_ref, b_ref, g_ref, beta_ref, win_ref, y_ref):
    lane_tiles = [slice(c * LANES, (c + 1) * LANES) for c in range(CONV_WIDTH // LANES)]
    zero = jnp.zeros((CONV_HALO, LANES), f32)
    for c, cs in enumerate(lane_tiles):
        win_ref[c, CONV_HALO:CONV_HALO + tt] = x_ref[:, cs].astype(f32)

    def fill(lo, rows):
        for c, cs in enumerate(lane_tiles):
            win_ref[c, lo:lo + CONV_HALO] = zero if rows is None else rows(cs).astype(f32)
    pl.when(jnp.logical_not(first))(lambda: fill(0, prev_rows))
    pl.when(first)(lambda: fill(0, None))
    pl.when(jnp.logical_not(last))(lambda: fill(CONV_HALO + tt, next_rows))
    pl.when(last)(lambda: fill(CONV_HALO + tt, None))

    gain, beta = g_ref[...], beta_ref[...]

    def rows(r0):
        pieces = []
        for c, cs in enumerate(lane_tiles):
            acc = jnp.broadcast_to(b_ref[:, cs], (CONV_ROWS, LANES))
            for k in range(CONV_KERNEL):
                off = r0 + CONV_HALO - CONV_PAD + k
                acc = acc + jnp.broadcast_to(w_ref[k:k + 1, cs], (CONV_ROWS, LANES)) * win_ref[c, off:off + CONV_ROWS, :]
            pieces.append(acc)
        acc = jnp.concatenate(pieces, axis=1)
        mu = jnp.mean(acc, axis=-1, keepdims=True)
        xc = acc - mu
        var = jnp.mean(xc * xc, axis=-1, keepdims=True)
        y = xc * lax.rsqrt(var + EPS) * gain + beta
        y = (y * jax.nn.sigmoid(y)).astype(bf16)
        y_ref[r0:r0 + CONV_ROWS, :] = y
        return y
    return rows


def _tail_kernel(final, tiles_per_seq, fixed_row,
                 one_ref, h_ref, ya_ref, hc_ref, hc_next_ref, hc_after_ref, yc_ref, mod_ref, g1_ref, g2_ref, gf_ref,
                 cw_ref, cb_ref, cg_ref, cbeta_ref,
                 wg_ref, woa_ref, wob_ref, woc_ref, wout_ref, wup_ref, wdn_ref, o_ref,
                 act_ref, win_ref, yb_ref, yb_next_ref):
    i = pl.program_id(0)
    tm = h_ref.shape[0]
    conv = functools.partial(_conv_ln_swish, tm, w_ref=cw_ref, b_ref=cb_ref, g_ref=cg_ref, beta_ref=cbeta_ref,
                             win_ref=win_ref, y_ref=yb_next_ref)
    head_of_next = lambda cs: hc_next_ref[0:CONV_HALO, cs]

    @pl.when(i == 0)
    def _():
        rows = conv(True, tiles_per_seq == 1, hc_ref, None, head_of_next)
        for r0 in range(0, tm, CONV_ROWS):
            rows(r0)
    yb_ref[...] = yb_next_ref[...]
    next_in_seq = (i + 1) % tiles_per_seq
    conv_rows = conv(next_in_seq == 0, next_in_seq == tiles_per_seq - 1, hc_next_ref,
                     lambda cs: hc_ref[tm - CONV_HALO:tm, cs], lambda cs: hc_after_ref[:, cs])
    row = fixed_row if fixed_row is not None else i // tiles_per_seq
    mod = lambda n: mod_ref[pl.ds(row, 1), n * D_MODEL:(n + 1) * D_MODEL]
    x = h_ref[...]
    a = _norm_mod(x, g1_ref[...], mod(0), mod(1)).astype(bf16)
    col_blocks = [slice(n0, n0 + OUT_CHUNK) for n0 in range(0, D_MODEL, OUT_CHUNK)]
    branches = ((ya_ref, woa_ref), (yb_ref, wob_ref), (yc_ref, woc_ref))
    merged = []
    for cs in col_blocks:
        total = None
        for n, (y_ref, w_ref) in enumerate(branches):
            gate = jax.nn.sigmoid(_mm(a, wg_ref[:, n * D_MODEL + cs.start:n * D_MODEL + cs.stop]))
            term = gate * _mm(y_ref[...], w_ref[:, cs])
            total = term if total is None else total + term
        merged.append(total.astype(bf16))
    h1 = x + mod(2) * _mm(jnp.concatenate(merged, axis=1), wout_ref[...])
    a2 = _norm_mod(h1, g2_ref[...], mod(3), mod(4)).astype(bf16)

    always = one_ref[0] == 1
    nffn = FFN_HIDDEN // FFN_CHUNK
    nconv = tm // CONV_ROWS
    for j, c0 in enumerate(range(0, FFN_HIDDEN, FFN_CHUNK)):
        up = _mm(a2, wup_ref[:, c0:c0 + FFN_CHUNK])
        gate = _mm(a2, wup_ref[:, FFN_HIDDEN + c0:FFN_HIDDEN + c0 + FFN_CHUNK])
        act = ((gate * jax.nn.sigmoid(gate)) * up).astype(bf16)
        pieces, done = [], 0
        for rc in range(j * nconv // nffn, (j + 1) * nconv // nffn):
            r0 = rc * CONV_ROWS
            y = conv_rows(r0)
            pieces += [act[done:r0]] * (r0 > done) + [jnp.where(always, act[r0:r0 + CONV_ROWS], y[:, 0:FFN_CHUNK])]
            done = r0 + CONV_ROWS
        if pieces:
            act = jnp.concatenate(pieces + [act[done:]] * (done < tm), axis=0)
        act_ref[:, c0:c0 + FFN_CHUNK] = act
    gate2 = mod(5)
    h2 =[h1[:, cs] + gate2[:, cs] * _mm(act_ref[...], wdn_ref[:, cs]) for cs in col_blocks]
    if final:
        sum_sq = functools.reduce(lambda u, v: u + v, [jnp.sum(p * p, axis=-1, keepdims=True) for p in h2])
        inv_rms = lax.rsqrt(sum_sq * (1.0 / D_MODEL) + EPS)
        h2 = [p * inv_rms * gf_ref[:, cs] for p, cs in zip(h2, col_blocks)]
    for p, cs in zip(h2, col_blocks):
        o_ref[:, cs] = p


def _tail(h, ya, hc, yc, mod, g1, g2, gf, conv, wts, tm, tiles_per_seq, fixed_row, final):
    rows = h.shape[0]
    row_spec = lambda width: pl.BlockSpec((tm, width), lambda i: (i, 0))
    ntiles = rows // tm
    hpt = tm // CONV_HALO
    nhalo = rows // CONV_HALO
    wg, woa, wob, woc, wout, wup, wdn = wts
    cw, cb, cg, cbeta = conv
    vec = lambda v: v.reshape(1, -1)
    return pl.pallas_call(
        functools.partial(_tail_kernel, final, tiles_per_seq, fixed_row),
        grid=(rows // tm,),
        in_specs=[pl.BlockSpec(memory_space=pltpu.SMEM),
                  row_spec(D_MODEL), row_spec(ATTN_WIDTH), row_spec(CONV_WIDTH),
                  pl.BlockSpec((tm, CONV_WIDTH), lambda i: (jnp.minimum(i + 1, ntiles - 1), 0)),
                  pl.BlockSpec((CONV_HALO, CONV_WIDTH), lambda i: (jnp.minimum((i + 2) * hpt, nhalo - 1), 0)),
                  row_spec(LRU_WIDTH),
                  _const_spec((MOD_ROWS, 6 * D_MODEL)),
                  _const_spec((1, D_MODEL)), _const_spec((1, D_MODEL)), _const_spec((1, D_MODEL)),
                  _const_spec(cw.shape), _const_spec((1, CONV_WIDTH)), _const_spec((1, CONV_WIDTH)),
                  _const_spec((1, CONV_WIDTH)),
                  _const_spec(wg.shape), _const_spec(woa.shape), _const_spec(wob.shape),
                  _const_spec(woc.shape), _const_spec(wout.shape), _const_spec(wup.shape),
                  _const_spec(wdn.shape)],
        out_specs=row_spec(D_MODEL),
        out_shape=jax.ShapeDtypeStruct((rows, D_MODEL), f32),
        scratch_shapes=[pltpu.VMEM((tm, FFN_HIDDEN), bf16),
                        pltpu.VMEM((CONV_WIDTH // LANES, tm + 2 * CONV_HALO, LANES), f32),
                        pltpu.VMEM((tm, CONV_WIDTH), bf16), pltpu.VMEM((tm, CONV_WIDTH), bf16)],
        compiler_params=_params(("arbitrary",)),
        name="merge_out_swiglu",
    )(jnp.ones((1,), jnp.int32), h, ya, hc, hc, hc, yc, mod, g1, g2, gf, cw, vec(cb), vec(cg), vec(cbeta),
      wg, woa, wob, woc, wout, wup, wdn)


def _rope_tables(seq):
    rows = seq // GRID_W
    inv = jnp.power(ROPE_BASE, -jnp.arange(ROPE_FREQS, dtype=f32) / ROPE_FREQS)
    row_ang = jnp.arange(rows, dtype=f32)[:, None] * inv[None]
    col_ang = jnp.arange(GRID_W, dtype=f32)[:, None] * inv[None]

    def table(fn):
        by_row = jnp.broadcast_to(fn(row_ang)[:, None, :], (rows, GRID_W, ROPE_FREQS))
        by_col = jnp.broadcast_to(fn(col_ang)[None, :, :], (rows, GRID_W, ROPE_FREQS))
        return jnp.concatenate([by_row, by_col], axis=-1).reshape(seq, 2 * ROPE_FREQS)
    cos, sin = table(jnp.cos), table(jnp.sin)
    reps = LANES // HEAD_DIM
    return jnp.tile(jnp.concatenate([cos, cos], axis=-1), (1, reps)), \
        jnp.tile(jnp.concatenate([-sin, sin], axis=-1), (1, reps))


def _block_diag(w):
    two, nb, d, e = w.shape
    eye = jnp.eye(nb, dtype=w.dtype)
    return jnp.einsum('xnde,nm->xndme', w, eye).reshape(two, nb * d, nb * e)


def kernel(x, c, ctx, c_ctx, mod_w, mod_b, norm1_g, norm2_g, w_in, attn_sink, conv_dw_w, conv_dw_b, conv_ln_g,
           conv_ln_b, lru_conv_w, lru_conv_b, lru_wa, lru_ba, lru_wx, lru_bx, lru_lam, w_o_attn, w_o_conv,
           w_o_lru, w_out, ffn_w_up, ffn_w_down, final_norm_g):
    bsz, seq, _ = x.shape
    ctx_len = ctx.shape[1]
    depth = mod_w.shape[0]
    assert bsz + 1 <= MOD_ROWS and seq % (ATTN_QBLOCKS * BLOCK) == 0 and ctx_len == SCAN_CHUNK
    tm = 512
    tiles_per_seq = seq // tm
    tm_ctx = ctx_len
    ctx_row = bsz

    cvec = jnp.zeros((MOD_ROWS, D_MODEL), f32).at[:bsz].set(c).at[ctx_row].set(c_ctx)
    mod_all = _modulation(cvec, mod_w, mod_b)
    cos, sin = _rope_tables(seq)
    row = lambda v: v.reshape(1, -1)

    h_lat = x.reshape(bsz * seq, D_MODEL)
    h_ctx = ctx.reshape(bsz * ctx_len, D_MODEL)
    for l in range(depth):
        need_ctx = l < depth - 1
        mod = mod_all[l]
        w_mix = w_in[l, :, :MIX_COLS].astype(bf16)
        wts = (
            w_in[l, :, MIX_COLS:].astype(bf16),
            w_o_attn[l].astype(bf16), w_o_conv[l].astype(bf16), w_o_lru[l].astype(bf16),
            w_out[l].astype(bf16),
            ffn_w_up[l].astype(bf16),
            ffn_w_down[l].astype(bf16),
        )
        g1, g2 = row(norm1_g[l]), row(norm2_g[l])

        q, k, v, hc, lx, gl = _inproj(h_lat, mod, g1, w_mix, tm, tiles_per_seq, None, cos, sin)
        qc, kc, vc, hcc, lxc, glc = _inproj(h_ctx, mod, g1, w_mix, tm_ctx, 1, ctx_row)
        shp = lambda t, n: t.reshape(bsz, n, t.shape[-1])
        kc3, vc3 = shp(kc, ctx_len), shp(vc, ctx_len)

        y_attn = _attention(attn_sink[l], shp(q, seq), shp(k, seq), shp(v, seq), kc3, vc3)
        conv = (conv_dw_w[l], conv_dw_b[l], conv_ln_g[l], conv_ln_b[l])
        y_lru, y_lru_c = _rglru(shp(lx, seq), shp(gl, seq), shp(lxc, ctx_len), shp(glc, ctx_len),
                                lru_conv_w[l], lru_conv_b[l],
                                _block_diag(0.5 * lru_wa[l]).astype(bf16), 0.5 * lru_ba[l],
                                _block_diag(0.5 * lru_wx[l]).astype(bf16), 0.5 * lru_bx[l], lru_lam[l], LRU_TILE)
        flat = lambda t: t.reshape(-1, t.shape[-1])
        h_lat = _tail(h_lat, flat(y_attn), hc, flat(y_lru), mod, g1, g2, row(final_norm_g), conv, wts,
                      tm, tiles_per_seq, None, final=not need_ctx)
        if need_ctx:
            y_attn_c = _attention(attn_sink[l], shp(qc, ctx_len), None, None, kc3, vc3)
            h_ctx = _tail(h_ctx, flat(y_attn_c), hcc, flat(y_lru_c), mod, g1, g2,
                          row(final_norm_g), conv, wts, tm_ctx, 1, ctx_row, final=False)
    return h_lat.reshape(bsz, seq, D_MODEL)
```

```python
import functools

import jax
import jax.numpy as jnp
from jax import lax
from jax.experimental import pallas as pl
from jax.experimental.pallas import tpu as pltpu

D_MODEL = 1024
GRID_W = 64
N_HEADS = 8
N_KV_HEADS = 2
GROUP = N_HEADS // N_KV_HEADS
HEAD_DIM = 64
ATTN_WIDTH = N_HEADS * HEAD_DIM
KV_WIDTH = N_KV_HEADS * HEAD_DIM
BLOCK = 128
ATTN_SCALE = HEAD_DIM ** -0.5
LOG2_E = 1.4426950408889634
ROPE_BASE = 10000.0
ROPE_FREQS = HEAD_DIM // 4
CONV_WIDTH = 512
CONV_KERNEL = 31
CONV_PAD = (CONV_KERNEL - 1) // 2
LRU_WIDTH = 512
LRU_BLOCKS = 8
LRU_BLOCK_DIM = LRU_WIDTH // LRU_BLOCKS
LRU_CONV = 4
LRU_PAD_LEFT = 2
LRU_C = 8.0
FFN_HIDDEN = 2816
MIX_COLS = ATTN_WIDTH + 2 * KV_WIDTH + 2 * CONV_WIDTH + 2 * LRU_WIDTH
EPS = 1e-6
NEG_INF = -1e30

LANES = 128
SUBLANES = 8
VMEM_LIMIT_BYTES = 56 * 1024 * 1024

MOD_ROWS = 8
FFN_CHUNK = 256
OUT_CHUNK = 256
ATTN_ROWS = 32
ATTN_QBLOCKS = 4
CONV_HALO = 16
CONV_ROWS = 32
LRU_TILE = 128
SCAN_CHUNK = 256
SCAN_SEGS = SUBLANES
SCAN_SEG = SCAN_CHUNK // SCAN_SEGS
SCAN_SEG_PITCH = SCAN_SEG + 8

f32 = jnp.float32
bf16 = jnp.bfloat16


def _mm(a, b):
    return jnp.dot(a, b, preferred_element_type=f32)


def _const_spec(shape):
    n = len(shape)
    return pl.BlockSpec(shape, lambda *_: (0,) * n, pipeline_mode=pl.Buffered(1))


def _params(sem):
    return pltpu.CompilerParams(dimension_semantics=sem, vmem_limit_bytes=VMEM_LIMIT_BYTES)


def _norm_mod(x, g, shift, scale):
    y = x * lax.rsqrt(jnp.mean(x * x, axis=-1, keepdims=True) + EPS)
    return (y * g) * (1.0 + scale) + shift


def _mod_kernel(c_ref, w_ref, b_ref, o_ref):
    c = c_ref[...]
    s = (c * jax.nn.sigmoid(c)).astype(bf16)
    o_ref[0] = _mm(s, w_ref[0].astype(bf16)) + b_ref[0]


def _modulation(cvec, mod_w, mod_b):
    depth, _, cols = mod_w.shape
    tn = 1024
    return pl.pallas_call(
        _mod_kernel,
        grid=(depth, cols // tn),
        in_specs=[
            pl.BlockSpec((MOD_ROWS, D_MODEL), lambda l, j: (0, 0)),
            pl.BlockSpec((1, D_MODEL, tn), lambda l, j: (l, 0, j)),
            pl.BlockSpec((1, 1, tn), lambda l, j: (l, 0, j)),
        ],
        out_specs=pl.BlockSpec((1, MOD_ROWS, tn), lambda l, j: (l, 0, j)),
        out_shape=jax.ShapeDtypeStruct((depth, MOD_ROWS, cols), f32),
        compiler_params=_params(("arbitrary", "arbitrary")),
        name="modulation",
    )(cvec, mod_w, mod_b.reshape(depth, 1, cols))


def _rope(z, cos, sin_signed):
    lane = lax.broadcasted_iota(jnp.int32, z.shape, 1)
    first_half = (lane & (HEAD_DIM - 1)) < HEAD_DIM // 2
    partner = jnp.where(first_half, pltpu.roll(z, LANES - HEAD_DIM // 2, axis=1),
                        pltpu.roll(z, HEAD_DIM // 2, axis=1))
    return z * cos + partner * sin_signed


def _dup_heads(z):
    low_half = lax.broadcasted_iota(jnp.int32, z.shape, 1) < HEAD_DIM
    swapped = pltpu.roll(z, HEAD_DIM, axis=1)
    return jnp.concatenate([jnp.where(low_half, z, swapped), jnp.where(low_half, swapped, z)], axis=1)


def _inproj_kernel(rope, tiles_per_seq, fixed_row, h_ref, mod_ref, g_ref, w_ref, *rest):
    if rope:
        cos_ref, sin_ref, q_ref, k_ref, v_ref, hc_ref, lx_ref, gl_ref = rest
    else:
        q_ref, k_ref, v_ref, hc_ref, lx_ref, gl_ref = rest
    row = fixed_row if fixed_row is not None else pl.program_id(0) // tiles_per_seq
    shift = mod_ref[pl.ds(row, 1), 0:D_MODEL]
    scale = mod_ref[pl.ds(row, 1), D_MODEL:2 * D_MODEL]
    a = _norm_mod(h_ref[...], g_ref[...], shift, scale).astype(bf16)

    if rope:
        cos, sin = cos_ref[...], sin_ref[...]
        fix = lambda z: _rope(z, cos, sin)
    else:
        fix = lambda z: z
    zq = _mm(a, w_ref[:, 0:ATTN_WIDTH])
    for j in range(ATTN_WIDTH // LANES):
        zj = fix(zq[:, j * LANES:(j + 1) * LANES]) * ATTN_SCALE
        q_ref[:, j * LANES:(j + 1) * LANES] = zj.astype(bf16)
    c0 = ATTN_WIDTH
    zkv = _mm(a, w_ref[:, c0:c0 + 2 * KV_WIDTH])
    k_ref[...] = _dup_heads(fix(zkv[:, 0:KV_WIDTH])).astype(bf16)
    v_ref[...] = _dup_heads(zkv[:, KV_WIDTH:]).astype(bf16)
    c0 += 2 * KV_WIDTH
    val = _mm(a, w_ref[:, c0:c0 + CONV_WIDTH])
    gate = _mm(a, w_ref[:, c0 + CONV_WIDTH:c0 + 2 * CONV_WIDTH])
    hc_ref[...] = val * jax.nn.sigmoid(gate)
    c0 += 2 * CONV_WIDTH
    gl_ref[...] = jax.nn.gelu(_mm(a, w_ref[:, c0 + LRU_WIDTH:c0 + 2 * LRU_WIDTH]))
    lx_ref[...] = _mm(a, w_ref[:, c0:c0 + LRU_WIDTH])


def _inproj(h, mod, g, w_mix, tm, tiles_per_seq, fixed_row, cos=None, sin=None):
    rows = h.shape[0]
    rope = cos is not None
    row_spec = lambda width: pl.BlockSpec((tm, width), lambda i: (i, 0))
    in_specs = [row_spec(D_MODEL), _const_spec((MOD_ROWS, 6 * D_MODEL)), _const_spec((1, D_MODEL)),
                _const_spec((D_MODEL, MIX_COLS))]
    args = [h, mod, g, w_mix]
    if rope:
        tab = pl.BlockSpec((tm, LANES), lambda i: (i % tiles_per_seq, 0))
        in_specs += [tab, tab]
        args += [cos, sin]
    widths = (ATTN_WIDTH, 2 * KV_WIDTH, 2 * KV_WIDTH, CONV_WIDTH, LRU_WIDTH, LRU_WIDTH)
    dtypes = (bf16, bf16, bf16, f32, f32, f32)
    return pl.pallas_call(
        functools.partial(_inproj_kernel, rope, tiles_per_seq, fixed_row),
        grid=(rows // tm,),
        in_specs=in_specs,
        out_specs=[row_spec(w) for w in widths],
        out_shape=[jax.ShapeDtypeStruct((rows, w), d) for w, d in zip(widths, dtypes)],
        compiler_params=_params(("arbitrary",)),
        name="inproj_rope" if rope else "inproj_ctx",
    )(*args)


def _attn_block(sink_ref, q, k_parts, v_parts, prev_ok, next_ok):
    tq = q.shape[0]
    low_half = lax.broadcasted_iota(jnp.int32, (tq, LANES), 1) < HEAD_DIM
    zero = jnp.zeros((tq, LANES), q.dtype)
    outs = []
    for g in range(N_KV_HEADS):
        gs = slice(g * LANES, (g + 1) * LANES)
        pairs = [q[:, (2 * g + i) * LANES:(2 * g + i + 1) * LANES] for i in range(GROUP // 2)]
        q4 = jnp.concatenate(
            [jnp.where(low_half if r % 2 == 0 else ~low_half, pairs[r // 2], zero) for r in range(GROUP)], axis=0)
        kcat = jnp.concatenate([p[:, gs] for p in k_parts], axis=0)
        vcat = jnp.concatenate([p[:, gs] for p in v_parts], axis=0)
        s = lax.dot_general(q4, kcat, (((1,), (1,)), ((), ())), preferred_element_type=f32)
        ps, dens = [], []
        for r0 in range(0, GROUP * tq, ATTN_ROWS):
            sc = s[r0:r0 + ATTN_ROWS]
            if prev_ok is not None:
                rs = slice(r0 % tq, r0 % tq + ATTN_ROWS)
                sc = jnp.concatenate([
                    jnp.where(prev_ok[rs], sc[:, 0:BLOCK], NEG_INF),
                    sc[:, BLOCK:2 * BLOCK],
                    jnp.where(next_ok[rs], sc[:, 2 * BLOCK:3 * BLOCK], NEG_INF),
                    sc[:, 3 * BLOCK:]], axis=1)
            sink = sink_ref[g * GROUP + r0 // tq]
            m = jnp.maximum(jnp.max(sc, axis=-1, keepdims=True), sink)
            p = jnp.exp(sc - m)
            dens.append(jnp.sum(p, axis=-1, keepdims=True) + jnp.exp(sink - m))
            ps.append(p.astype(bf16))
        o = _mm(jnp.concatenate(ps, axis=0), vcat) / jnp.concatenate(dens, axis=0)
        for i in range(GROUP // 2):
            outs.append(jnp.where(low_half, o[2 * i * tq:(2 * i + 1) * tq], o[(2 * i + 1) * tq:(2 * i + 2) * tq]))
    return jnp.concatenate(outs, axis=1)


def _attn_window_kernel(nsteps, sink_ref, q_ref, kp, ko, kn, vp, vo, vn, kc, vc, o_ref):
    j = pl.program_id(1)
    qi = lax.broadcasted_iota(jnp.int32, (BLOCK, BLOCK), 0)
    kj = lax.broadcasted_iota(jnp.int32, (BLOCK, BLOCK), 1)
    first_prev_ok = kj + jnp.where(j > 0, 0, -2 * BLOCK) >= qi
    last_next_ok = kj + jnp.where(j < nsteps - 1, 0, 2 * BLOCK) <= qi
    own = lambda ref: [ref[0, b * BLOCK:(b + 1) * BLOCK] for b in range(ATTN_QBLOCKS)]
    k_blocks = [kp[0]] + own(ko) + [kn[0]]
    v_blocks = [vp[0]] + own(vo) + [vn[0]]
    for b in range(ATTN_QBLOCKS):
        out = _attn_block(sink_ref, q_ref[0, b * BLOCK:(b + 1) * BLOCK],
                          k_blocks[b:b + 3] + [kc[0]], v_blocks[b:b + 3] + [vc[0]],
                          first_prev_ok if b == 0 else kj >= qi,
                          last_next_ok if b == ATTN_QBLOCKS - 1 else kj <= qi)
        o_ref[0, b * BLOCK:(b + 1) * BLOCK, :] = out.astype(bf16)


def _attn_ctx_kernel(sink_ref, q_ref, kc, vc, o_ref):
    o_ref[0] = _attn_block(sink_ref, q_ref[0], [kc[0]], [vc[0]], None, None).astype(bf16)


def _attention(sink, q, k, v, kc, vc):
    bsz, seq, _ = q.shape
    ctx_len = kc.shape[1]
    kvw = kc.shape[2]
    ctx_spec = pl.BlockSpec((1, ctx_len, kvw), lambda b, j: (b, 0, 0))
    sink_spec = pl.BlockSpec(memory_space=pltpu.SMEM)
    if k is None:
        tq, nsteps = seq, 1
        body = _attn_ctx_kernel
        in_specs, args = [ctx_spec, ctx_spec], [kc, vc]
    else:
        tq = ATTN_QBLOCKS * BLOCK
        nsteps = seq // tq
        nblk = seq // BLOCK
        body = functools.partial(_attn_window_kernel, nsteps)
        prev = pl.BlockSpec((1, BLOCK, kvw), lambda b, j: (b, jnp.maximum(ATTN_QBLOCKS * j - 1, 0), 0))
        own = pl.BlockSpec((1, tq, kvw), lambda b, j: (b, j, 0))
        nxt = pl.BlockSpec((1, BLOCK, kvw), lambda b, j: (b, jnp.minimum(ATTN_QBLOCKS * (j + 1), nblk - 1), 0))
        in_specs = [prev, own, nxt, prev, own, nxt, ctx_spec, ctx_spec]
        args = [k, k, k, v, v, v, kc, vc]
    return pl.pallas_call(
        body,
        grid=(bsz, nsteps),
        in_specs=[sink_spec, pl.BlockSpec((1, tq, ATTN_WIDTH), lambda b, j: (b, j, 0))] + in_specs,
        out_specs=pl.BlockSpec((1, tq, ATTN_WIDTH), lambda b, j: (b, j, 0)),
        out_shape=jax.ShapeDtypeStruct((bsz, seq, ATTN_WIDTH), bf16),
        compiler_params=_params(("arbitrary", "arbitrary")),
        name="attn_ctx" if k is None else "attn_window",
    )(sink, q, *args)


SCAN_PAD = 8


def _lru_kernel(seq, ctx_len, ct,
                xl_ref, gl_ref, xc_ref, gc_ref, cw_ref, cb_ref, wa_ref, ba_ref, wx_ref, bx_ref, lam_ref,
                yl_ref, yc_ref,
                xp_ref, u_ref, hf_ref, seg_ref, run_ref):
    lane_tiles = [slice(c * LANES, (c + 1) * LANES) for c in range(ct // LANES)]
    neg_lam = -lam_ref[...]
    softplus_neg_lam = jnp.maximum(neg_lam, 0.0) + jnp.log1p(jnp.exp(-jnp.abs(neg_lam)))

    def conv_into_u(src_ref, n):
        zero = jnp.zeros((SCAN_PAD, LANES), f32)
        for c in range(ct // LANES):
            cs = slice(c * LANES, (c + 1) * LANES)
            xp_ref[c, 0:SCAN_PAD] = zero
            xp_ref[c, SCAN_PAD + n:2 * SCAN_PAD + n] = zero
            for r0 in range(0, n, SCAN_CHUNK):
                xp_ref[c, SCAN_PAD + r0:SCAN_PAD + r0 + SCAN_CHUNK] = src_ref[0, r0:r0 + SCAN_CHUNK, cs]
            for r0 in range(0, n, SCAN_CHUNK):
                acc = jnp.broadcast_to(cb_ref[:, cs], (SCAN_CHUNK, LANES))
                for k in range(LRU_CONV):
                    off = SCAN_PAD + r0 + k - LRU_PAD_LEFT
                    acc = acc + cw_ref[k:k + 1, cs] * xp_ref[c, off:off + SCAN_CHUNK, :]
                u_ref[r0:r0 + SCAN_CHUNK, cs] = acc

    half_scale = (0.5 * LRU_C) * softplus_neg_lam

    def stage(buf, r0, d):
        uc = u_ref[pl.ds(r0, SCAN_CHUNK)]
        ub = uc.astype(bf16)
        half_za = _mm(ub, wa_ref[d]) + ba_ref[d:d + 1]
        half_zx = _mm(ub, wx_ref[d]) + bx_ref[d:d + 1]
        for s in range(SCAN_SEGS):
            rs = slice(s * SCAN_SEG, (s + 1) * SCAN_SEG)
            neg_log_a = (jnp.tanh(half_za[rs]) + 1.0) * half_scale[d:d + 1]
            a = jnp.exp2(neg_log_a * (-LOG2_E))
            y = jnp.tanh(neg_log_a) * (a * a + 1.0)
            root = jnp.where(y > 0.0, y * lax.rsqrt(y), 0.0)
            b = root * ((0.5 * jnp.tanh(half_zx[rs]) + 0.5) * uc[rs])
            lo = s * SCAN_SEG_PITCH
            for c, cs in enumerate(lane_tiles):
                seg_ref[buf, d, 0, c, lo:lo + SCAN_SEG] = a[:, cs]
                seg_ref[buf, d, 1, c, lo:lo + SCAN_SEG] = b[:, cs]

    def scan(buf, d, carry):
        hl = [jnp.zeros((SCAN_SEGS, LANES), f32) for _ in lane_tiles]
        pa = [jnp.ones((SCAN_SEGS, LANES), f32) for _ in lane_tiles]
        for i in range(SCAN_SEG):
            t = i if d == 0 else SCAN_SEG - 1 - i
            rows = pl.ds(t, SCAN_SEGS, stride=SCAN_SEG_PITCH)
            for c in range(len(lane_tiles)):
                av = seg_ref[buf, d, 0, c, rows]
                hl[c] = av * hl[c] + seg_ref[buf, d, 1, c, rows]
                pa[c] = av * pa[c]
                run_ref[d, 0, c, rows] = hl[c]
                run_ref[d, 1, c, rows] = pa[c]
        order = range(SCAN_SEGS) if d == 0 else range(SCAN_SEGS - 1, -1, -1)
        states, carries = [], []
        for c, cs in enumerate(lane_tiles):
            pieces, cur = [None] * SCAN_SEGS, carry[:, cs]
            for s in order:
                lo = s * SCAN_SEG_PITCH
                pieces[s] = run_ref[d, 0, c, lo:lo + SCAN_SEG] + run_ref[d, 1, c, lo:lo + SCAN_SEG] * cur
                cur = pa[c][s:s + 1] * cur + hl[c][s:s + 1]
            states.append(jnp.concatenate(pieces, axis=0))
            carries.append(cur)
        return jnp.concatenate(states, axis=1), jnp.concatenate(carries, axis=1)

    conv_into_u(xc_ref, ctx_len)
    zero_row = jnp.zeros((1, ct), f32)
    assert ctx_len == SCAN_CHUNK
    stage(0, 0, 0)
    stage(0, 0, 1)
    cf, carry_f = scan(0, 0, zero_row)
    cr, carry_r = scan(0, 1, zero_row)
    yc_ref[0] = ((cf + cr) * gc_ref[0]).astype(bf16)

    conv_into_u(xl_ref, seq)
    nchunk = seq // SCAN_CHUNK
    fwd_start = lambda c: pl.multiple_of(c * SCAN_CHUNK, SCAN_CHUNK)
    rev_start = lambda c: pl.multiple_of((nchunk - 1 - c) * SCAN_CHUNK, SCAN_CHUNK)
    stage(0, fwd_start(0), 0)
    stage(0, rev_start(0), 1)

    assert nchunk % 4 == 0

    def emit(first_visit, r0, states):
        if first_visit:
            hf_ref[pl.ds(r0, SCAN_CHUNK)] = states
        else:
            yl_ref[0, pl.ds(r0, SCAN_CHUNK), :] = (
                (states + hf_ref[pl.ds(r0, SCAN_CHUNK)])
                * gl_ref[0, pl.ds(r0, SCAN_CHUNK), :]).astype(bf16)

    def pair_body(first_visit, it, carries):
        carry_f, carry_r = carries
        for buf in range(2):
            c = 2 * it + buf
            nxt = jnp.minimum(c + 1, nchunk - 1)
            stage(1 - buf, fwd_start(nxt), 0)
            stage(1 - buf, rev_start(nxt), 1)
            out_f, carry_f = scan(buf, 0, carry_f)
            out_r, carry_r = scan(buf, 1, carry_r)
            emit(first_visit, fwd_start(c), out_f)
            emit(first_visit, rev_start(c), out_r)
        return carry_f, carry_r

    carries = lax.fori_loop(0, nchunk // 4, functools.partial(pair_body, True), (carry_f, carry_r))
    lax.fori_loop(nchunk // 4, nchunk // 2, functools.partial(pair_body, False), carries)


def _rglru(xl, gl, xc, gc, conv_w, conv_b, wa_bd, ba, wx_bd, bx, lam, ct):
    bsz, seq, _ = xl.shape
    ctx_len = xc.shape[1]
    seg_rows = SCAN_SEGS * SCAN_SEG_PITCH
    col = lambda rows: pl.BlockSpec((1, rows, ct), lambda b, c: (b, 0, c))
    vec = lambda rows: pl.BlockSpec((rows, ct), lambda b, c: (0, c))
    bd = pl.BlockSpec((2, ct, ct), lambda b, c: (0, c, c))
    return pl.pallas_call(
        functools.partial(_lru_kernel, seq, ctx_len, ct),
        grid=(bsz, LRU_WIDTH // ct),
        in_specs=[col(seq), col(seq), col(ctx_len), col(ctx_len),
                  vec(LRU_CONV), vec(1), bd, vec(2), bd, vec(2), vec(2)],
        out_specs=[col(seq), col(ctx_len)],
        out_shape=[jax.ShapeDtypeStruct((bsz, seq, LRU_WIDTH), bf16),
                   jax.ShapeDtypeStruct((bsz, ctx_len, LRU_WIDTH), bf16)],
        scratch_shapes=[pltpu.VMEM((ct // LANES, seq + 2 * SCAN_PAD, LANES), f32), pltpu.VMEM((seq, ct), f32),
                        pltpu.VMEM((seq, ct), f32), pltpu.VMEM((2, 2, 2, ct // LANES, seg_rows, LANES), f32),
                        pltpu.VMEM((2, 2, ct // LANES, seg_rows, LANES), f32)],
        compiler_params=_params(("arbitrary", "arbitrary")),
        name="rglru",
    )(xl, gl, xc, gc, conv_w, conv_b.reshape(1, LRU_WIDTH), wa_bd, ba, wx_bd, bx, lam)


def _conv_ln_swish(tt, first, last, x_ref, prev_rows, next_rows, w_ref, b_ref, g_ref, beta_ref, win_ref, y_ref):
    lane_tiles = [slice(c * LANES, (c + 1) * LANES) for c in range(CONV_WIDTH // LANES)]
    zero = jnp.zeros((CONV_HALO, LANES), f32)
    for c, cs in enumerate(lane_tiles):
        win_ref[c, CONV_HALO:CONV_HALO + tt] = x_ref[:, cs]

    def fill(lo, rows):
        for c, cs in enumerate(lane_tiles):
            win_ref[c, lo:lo + CONV_HALO] = zero if rows is None else rows(cs)
    pl.when(jnp.logical_not(first))(lambda: fill(0, prev_rows))
    pl.when(first)(lambda: fill(0, None))
    pl.when(jnp.logical_not(last))(lambda: fill(CONV_HALO + tt, next_rows))
    pl.when(last)(lambda: fill(CONV_HALO + tt, None))

    gain, beta = g_ref[...], beta_ref[...]

    def rows(r0):
        pieces = []
        for c, cs in enumerate(lane_tiles):
            acc = jnp.broadcast_to(b_ref[:, cs], (CONV_ROWS, LANES))
            for k in range(CONV_KERNEL):
                off = r0 + CONV_HALO - CONV_PAD + k
                acc = acc + jnp.broadcast_to(w_ref[k:k + 1, cs], (CONV_ROWS, LANES)) * win_ref[c, off:off + CONV_ROWS, :]
            pieces.append(acc)
        acc = jnp.concatenate(pieces, axis=1)
        mu = jnp.mean(acc, axis=-1, keepdims=True)
        xc = acc - mu
        var = jnp.mean(xc * xc, axis=-1, keepdims=True)
        y = xc * lax.rsqrt(var + EPS) * gain + beta
        y = (y * jax.nn.sigmoid(y)).astype(bf16)
        y_ref[r0:r0 + CONV_ROWS, :] = y
        return y
    return rows


def _tail_kernel(final, tiles_per_seq, fixed_row,
                 one_ref, h_ref, ya_ref, hc_ref, hc_next_ref, hc_after_ref, yc_ref, mod_ref, g1_ref, g2_ref, gf_ref,
                 cw_ref, cb_ref, cg_ref, cbeta_ref,
                 wg_ref, woa_ref, wob_ref, woc_ref, wout_ref, wup_ref, wdn_ref, o_ref,
                 act_ref, win_ref, yb_ref, yb_next_ref):
    i = pl.program_id(0)
    tm = h_ref.shape[0]
    conv = functools.partial(_conv_ln_swish, tm, w_ref=cw_ref, b_ref=cb_ref, g_ref=cg_ref, beta_ref=cbeta_ref,
                             win_ref=win_ref, y_ref=yb_next_ref)
    head_of_next = lambda cs: hc_next_ref[0:CONV_HALO, cs]

    @pl.when(i == 0)
    def _():
        rows = conv(True, tiles_per_seq == 1, hc_ref, None, head_of_next)
        for r0 in range(0, tm, CONV_ROWS):
            rows(r0)
    yb_ref[...] = yb_next_ref[...]
    next_in_seq = (i + 1) % tiles_per_seq
    conv_rows = conv(next_in_seq == 0, next_in_seq == tiles_per_seq - 1, hc_next_ref,
                     lambda cs: hc_ref[tm - CONV_HALO:tm, cs], lambda cs: hc_after_ref[:, cs])
    row = fixed_row if fixed_row is not None else i // tiles_per_seq
    mod = lambda n: mod_ref[pl.ds(row, 1), n * D_MODEL:(n + 1) * D_MODEL]
    x = h_ref[...]
    a = _norm_mod(x, g1_ref[...], mod(0), mod(1)).astype(bf16)
    col_blocks = [slice(n0, n0 + OUT_CHUNK) for n0 in range(0, D_MODEL, OUT_CHUNK)]
    nffn = FFN_HIDDEN // FFN_CHUNK
    always = one_ref[0] == 1
    assert OUT_CHUNK == FFN_CHUNK
    anchors = len(col_blocks) + nffn
    nconv = tm // CONV_ROWS

    def tied(val, anchor):
        pieces, done = [], 0
        for rc in range(anchor * nconv // anchors, (anchor + 1) * nconv // anchors):
            r0 = rc * CONV_ROWS
            y = conv_rows(r0)
            pieces += [val[done:r0]] * (r0 > done) + [jnp.where(always, val[r0:r0 + CONV_ROWS], y[:, 0:FFN_CHUNK])]
            done = r0 + CONV_ROWS
        return jnp.concatenate(pieces + [val[done:]] * (done < tm), axis=0) if pieces else val

    branches = ((ya_ref, woa_ref), (yb_ref, wob_ref), (yc_ref, woc_ref))
    merged = []
    for nb, cs in enumerate(col_blocks):
        total = None
        for n, (y_ref, w_ref) in enumerate(branches):
            gate = jax.nn.sigmoid(_mm(a, wg_ref[:, n * D_MODEL + cs.start:n * D_MODEL + cs.stop]))
            term = gate * _mm(y_ref[...], w_ref[:, cs])
            total = term if total is None else total + term
        merged.append(tied(total.astype(bf16), nb))
    h1 = x + mod(2) * _mm(jnp.concatenate(merged, axis=1), wout_ref[...])
    a2 = _norm_mod(h1, g2_ref[...], mod(3), mod(4)).astype(bf16)

    for j, c0 in enumerate(range(0, FFN_HIDDEN, FFN_CHUNK)):
        up = _mm(a2, wup_ref[:, c0:c0 + FFN_CHUNK])
        gate = _mm(a2, wup_ref[:, FFN_HIDDEN + c0:FFN_HIDDEN + c0 + FFN_CHUNK])
        act = ((gate * jax.nn.sigmoid(gate)) * up).astype(bf16)
        act_ref[:, c0:c0 + FFN_CHUNK] = tied(act, len(col_blocks) + j)
    gate2 = mod(5)
    h2 =[h1[:, cs] + gate2[:, cs] * _mm(act_ref[...], wdn_ref[:, cs]) for cs in col_blocks]
    if final:
        sum_sq = functools.reduce(lambda u, v: u + v, [jnp.sum(p * p, axis=-1, keepdims=True) for p in h2])
        inv_rms = lax.rsqrt(sum_sq * (1.0 / D_MODEL) + EPS)
        h2 = [p * inv_rms * gf_ref[:, cs] for p, cs in zip(h2, col_blocks)]
    for p, cs in zip(h2, col_blocks):
        o_ref[:, cs] = p


def _tail(h, ya, hc, yc, mod, g1, g2, gf, conv, wts, tm, tiles_per_seq, fixed_row, final):
    rows = h.shape[0]
    row_spec = lambda width: pl.BlockSpec((tm, width), lambda i: (i, 0))
    ntiles = rows // tm
    hpt = tm // CONV_HALO
    nhalo = rows // CONV_HALO
    wg, woa, wob, woc, wout, wup, wdn = wts
    cw, cb, cg, cbeta = conv
    vec = lambda v: v.reshape(1, -1)
    return pl.pallas_call(
        functools.partial(_tail_kernel, final, tiles_per_seq, fixed_row),
        grid=(rows // tm,),
        in_specs=[pl.BlockSpec(memory_space=pltpu.SMEM),
                  row_spec(D_MODEL), row_spec(ATTN_WIDTH), row_spec(CONV_WIDTH),
                  pl.BlockSpec((tm, CONV_WIDTH), lambda i: (jnp.minimum(i + 1, ntiles - 1), 0)),
                  pl.BlockSpec((CONV_HALO, CONV_WIDTH), lambda i: (jnp.minimum((i + 2) * hpt, nhalo - 1), 0)),
                  row_spec(LRU_WIDTH),
                  _const_spec((MOD_ROWS, 6 * D_MODEL)),
                  _const_spec((1, D_MODEL)), _const_spec((1, D_MODEL)), _const_spec((1, D_MODEL)),
                  _const_spec(cw.shape), _const_spec((1, CONV_WIDTH)), _const_spec((1, CONV_WIDTH)),
                  _const_spec((1, CONV_WIDTH)),
                  _const_spec(wg.shape), _const_spec(woa.shape), _const_spec(wob.shape),
                  _const_spec(woc.shape), _const_spec(wout.shape), _const_spec(wup.shape),
                  _const_spec(wdn.shape)],
        out_specs=row_spec(D_MODEL),
        out_shape=jax.ShapeDtypeStruct((rows, D_MODEL), f32),
        scratch_shapes=[pltpu.VMEM((tm, FFN_HIDDEN), bf16),
                        pltpu.VMEM((CONV_WIDTH // LANES, tm + 2 * CONV_HALO, LANES), f32),
                        pltpu.VMEM((tm, CONV_WIDTH), bf16), pltpu.VMEM((tm, CONV_WIDTH), bf16)],
        compiler_params=_params(("arbitrary",)),
        name="merge_out_swiglu",
    )(jnp.ones((1,), jnp.int32), h, ya, hc, hc, hc, yc, mod, g1, g2, gf, cw, vec(cb), vec(cg), vec(cbeta),
      wg, woa, wob, woc, wout, wup, wdn)


def _rope_tables(seq):
    rows = seq // GRID_W
    inv = jnp.power(ROPE_BASE, -jnp.arange(ROPE_FREQS, dtype=f32) / ROPE_FREQS)
    row_ang = jnp.arange(rows, dtype=f32)[:, None] * inv[None]
    col_ang = jnp.arange(GRID_W, dtype=f32)[:, None] * inv[None]

    def table(fn):
        by_row = jnp.broadcast_to(fn(row_ang)[:, None, :], (rows, GRID_W, ROPE_FREQS))
        by_col = jnp.broadcast_to(fn(col_ang)[None, :, :], (rows, GRID_W, ROPE_FREQS))
        return jnp.concatenate([by_row, by_col], axis=-1).reshape(seq, 2 * ROPE_FREQS)
    cos, sin = table(jnp.cos), table(jnp.sin)
    reps = LANES // HEAD_DIM
    return jnp.tile(jnp.concatenate([cos, cos], axis=-1), (1, reps)), \
        jnp.tile(jnp.concatenate([-sin, sin], axis=-1), (1, reps))


def _block_diag(w):
    two, nb, d, e = w.shape
    eye = jnp.eye(nb, dtype=w.dtype)
    return jnp.einsum('xnde,nm->xndme', w, eye).reshape(two, nb * d, nb * e)


def kernel(x, c, ctx, c_ctx, mod_w, mod_b, norm1_g, norm2_g, w_in, attn_sink, conv_dw_w, conv_dw_b, conv_ln_g,
           conv_ln_b, lru_conv_w, lru_conv_b, lru_wa, lru_ba, lru_wx, lru_bx, lru_lam, w_o_attn, w_o_conv,
           w_o_lru, w_out, ffn_w_up, ffn_w_down, final_norm_g):
    bsz, seq, _ = x.shape
    ctx_len = ctx.shape[1]
    depth = mod_w.shape[0]
    assert bsz + 1 <= MOD_ROWS and seq % (ATTN_QBLOCKS * BLOCK) == 0 and ctx_len == SCAN_CHUNK
    tm = 512
    tiles_per_seq = seq // tm
    tm_ctx = ctx_len
    ctx_row = bsz

    cvec = jnp.zeros((MOD_ROWS, D_MODEL), f32).at[:bsz].set(c).at[ctx_row].set(c_ctx)
    mod_all = _modulation(cvec, mod_w, mod_b)
    cos, sin = _rope_tables(seq)
    row = lambda v: v.reshape(1, -1)

    h_lat = x.reshape(bsz * seq, D_MODEL)
    h_ctx = ctx.reshape(bsz * ctx_len, D_MODEL)
    for l in range(depth):
        need_ctx = l < depth - 1
        mod = mod_all[l]
        w_mix = w_in[l, :, :MIX_COLS].astype(bf16)
        wts = (
            w_in[l, :, MIX_COLS:].astype(bf16),
            w_o_attn[l].astype(bf16), w_o_conv[l].astype(bf16), w_o_lru[l].astype(bf16),
            w_out[l].astype(bf16),
            ffn_w_up[l].astype(bf16),
            ffn_w_down[l].astype(bf16),
        )
        g1, g2 = row(norm1_g[l]), row(norm2_g[l])

        q, k, v, hc, lx, gl = _inproj(h_lat, mod, g1, w_mix, tm, tiles_per_seq, None, cos, sin)
        qc, kc, vc, hcc, lxc, glc = _inproj(h_ctx, mod, g1, w_mix, tm_ctx, 1, ctx_row)
        shp = lambda t, n: t.reshape(bsz, n, t.shape[-1])
        kc3, vc3 = shp(kc, ctx_len), shp(vc, ctx_len)

        y_attn = _attention(attn_sink[l], shp(q, seq), shp(k, seq), shp(v, seq), kc3, vc3)
        conv = (conv_dw_w[l], conv_dw_b[l], conv_ln_g[l], conv_ln_b[l])
        y_lru, y_lru_c = _rglru(shp(lx, seq), shp(gl, seq), shp(lxc, ctx_len), shp(glc, ctx_len),
                                lru_conv_w[l], lru_conv_b[l],
                                _block_diag(0.5 * lru_wa[l]).astype(bf16), 0.5 * lru_ba[l],
                                _block_diag(0.5 * lru_wx[l]).astype(bf16), 0.5 * lru_bx[l], lru_lam[l], LRU_TILE)
        flat = lambda t: t.reshape(-1, t.shape[-1])
        h_lat = _tail(h_lat, flat(y_attn), hc, flat(y_lru), mod, g1, g2, row(final_norm_g), conv, wts,
                      tm, tiles_per_seq, None, final=not need_ctx)
        if need_ctx:
            y_attn_c = _attention(attn_sink[l], shp(qc, ctx_len), None, None, kc3, vc3)
            h_ctx = _tail(h_ctx, flat(y_attn_c), hcc, flat(y_lru_c), mod, g1, g2,
                          row(final_norm_g), conv, wts, tm_ctx, 1, ctx_row, final=False)
    return h_lat.reshape(bsz, seq, D_MODEL)
```

```python
import functools

import jax
import jax.numpy as jnp
from jax import lax
from jax.experimental import pallas as pl
from jax.experimental.pallas import tpu as pltpu

D_MODEL = 1024
GRID_W = 64
N_HEADS = 8
N_KV_HEADS = 2
GROUP = N_HEADS // N_KV_HEADS
HEAD_DIM = 64
ATTN_WIDTH = N_HEADS * HEAD_DIM
KV_WIDTH = N_KV_HEADS * HEAD_DIM
BLOCK = 128
ATTN_SCALE = HEAD_DIM ** -0.5
LOG2_E = 1.4426950408889634
ROPE_BASE = 10000.0
ROPE_FREQS = HEAD_DIM // 4
CONV_WIDTH = 512
CONV_KERNEL = 31
CONV_PAD = (CONV_KERNEL - 1) // 2
LRU_WIDTH = 512
LRU_BLOCKS = 8
LRU_BLOCK_DIM = LRU_WIDTH // LRU_BLOCKS
LRU_CONV = 4
LRU_PAD_LEFT = 2
LRU_C = 8.0
FFN_HIDDEN = 2816
MIX_COLS = ATTN_WIDTH + 2 * KV_WIDTH + 2 * CONV_WIDTH + 2 * LRU_WIDTH
EPS = 1e-6
NEG_INF = -1e30

LANES = 128
SUBLANES = 8
VMEM_LIMIT_BYTES = 56 * 1024 * 1024

MOD_ROWS = 8
FFN_CHUNK = 256
OUT_CHUNK = 256
ATTN_ROWS = 32
ATTN_QBLOCKS = 4
CONV_HALO = 16
CONV_ROWS = 32
LRU_TILE = 128
SCAN_CHUNK = 256
SCAN_SEGS = SUBLANES
SCAN_SEG = SCAN_CHUNK // SCAN_SEGS
SCAN_SEG_PITCH = SCAN_SEG + 8

f32 = jnp.float32
bf16 = jnp.bfloat16


def _mm(a, b):
    return jnp.dot(a, b, preferred_element_type=f32)


def _const_spec(shape):
    n = len(shape)
    return pl.BlockSpec(shape, lambda *_: (0,) * n, pipeline_mode=pl.Buffered(1))


def _layer_spec(stacked, layer):
    tail = (0,) * (stacked.ndim - 1)
    return pl.BlockSpec((None,) + stacked.shape[1:], lambda *_: (layer,) + tail, pipeline_mode=pl.Buffered(1))


def _params(sem):
    return pltpu.CompilerParams(dimension_semantics=sem, vmem_limit_bytes=VMEM_LIMIT_BYTES)


def _norm_mod(x, g, shift, scale):
    y = x * lax.rsqrt(jnp.mean(x * x, axis=-1, keepdims=True) + EPS)
    return (y * g) * (1.0 + scale) + shift


def _mod_kernel(c_ref, w_ref, b_ref, o_ref):
    c = c_ref[...]
    s = (c * jax.nn.sigmoid(c)).astype(bf16)
    o_ref[0] = _mm(s, w_ref[0].astype(bf16)) + b_ref[0]


def _modulation(cvec, mod_w, mod_b):
    depth, _, cols = mod_w.shape
    tn = 1024
    return pl.pallas_call(
        _mod_kernel,
        grid=(depth, cols // tn),
        in_specs=[
            pl.BlockSpec((MOD_ROWS, D_MODEL), lambda l, j: (0, 0)),
            pl.BlockSpec((1, D_MODEL, tn), lambda l, j: (l, 0, j)),
            pl.BlockSpec((1, 1, tn), lambda l, j: (l, 0, j)),
        ],
        out_specs=pl.BlockSpec((1, MOD_ROWS, tn), lambda l, j: (l, 0, j)),
        out_shape=jax.ShapeDtypeStruct((depth, MOD_ROWS, cols), f32),
        compiler_params=_params(("arbitrary", "arbitrary")),
        name="modulation",
    )(cvec, mod_w, mod_b.reshape(depth, 1, cols))


def _rope(z, cos, sin_signed):
    lane = lax.broadcasted_iota(jnp.int32, z.shape, 1)
    first_half = (lane & (HEAD_DIM - 1)) < HEAD_DIM // 2
    partner = jnp.where(first_half, pltpu.roll(z, LANES - HEAD_DIM // 2, axis=1),
                        pltpu.roll(z, HEAD_DIM // 2, axis=1))
    return z * cos + partner * sin_signed


def _dup_heads(z):
    low_half = lax.broadcasted_iota(jnp.int32, z.shape, 1) < HEAD_DIM
    swapped = pltpu.roll(z, HEAD_DIM, axis=1)
    return jnp.concatenate([jnp.where(low_half, z, swapped), jnp.where(low_half, swapped, z)], axis=1)


def _inproj_kernel(rope, tiles_per_seq, fixed_row, h_ref, mod_ref, g_ref, w_ref, *rest):
    if rope:
        cos_ref, sin_ref, q_ref, k_ref, v_ref, hc_ref, lx_ref, gl_ref = rest
    else:
        q_ref, k_ref, v_ref, hc_ref, lx_ref, gl_ref = rest
    row = fixed_row if fixed_row is not None else pl.program_id(0) // tiles_per_seq
    shift = mod_ref[pl.ds(row, 1), 0:D_MODEL]
    scale = mod_ref[pl.ds(row, 1), D_MODEL:2 * D_MODEL]
    a = _norm_mod(h_ref[...], g_ref[...], shift, scale).astype(bf16)

    if rope:
        cos, sin = cos_ref[...], sin_ref[...]
        fix = lambda z: _rope(z, cos, sin)
    else:
        fix = lambda z: z
    zq = _mm(a, w_ref[:, 0:ATTN_WIDTH])
    for j in range(ATTN_WIDTH // LANES):
        zj = fix(zq[:, j * LANES:(j + 1) * LANES]) * ATTN_SCALE
        q_ref[:, j * LANES:(j + 1) * LANES] = zj.astype(bf16)
    c0 = ATTN_WIDTH
    zkv = _mm(a, w_ref[:, c0:c0 + 2 * KV_WIDTH])
    k_ref[...] = _dup_heads(fix(zkv[:, 0:KV_WIDTH])).astype(bf16)
    v_ref[...] = _dup_heads(zkv[:, KV_WIDTH:]).astype(bf16)
    c0 += 2 * KV_WIDTH
    val = _mm(a, w_ref[:, c0:c0 + CONV_WIDTH])
    gate = _mm(a, w_ref[:, c0 + CONV_WIDTH:c0 + 2 * CONV_WIDTH])
    hc_ref[...] = val * jax.nn.sigmoid(gate)
    c0 += 2 * CONV_WIDTH
    gl_ref[...] = jax.nn.gelu(_mm(a, w_ref[:, c0 + LRU_WIDTH:c0 + 2 * LRU_WIDTH]))
    lx_ref[...] = _mm(a, w_ref[:, c0:c0 + LRU_WIDTH])


def _inproj(h, mod, g, w_mix, layer, tm, tiles_per_seq, fixed_row, cos=None, sin=None):
    rows = h.shape[0]
    rope = cos is not None
    row_spec = lambda width: pl.BlockSpec((tm, width), lambda i: (i, 0))
    in_specs = [row_spec(D_MODEL), _const_spec((MOD_ROWS, 6 * D_MODEL)), _const_spec((1, D_MODEL)),
                _layer_spec(w_mix, layer)]
    args = [h, mod, g, w_mix]
    if rope:
        tab = pl.BlockSpec((tm, LANES), lambda i: (i % tiles_per_seq, 0))
        in_specs += [tab, tab]
        args += [cos, sin]
    widths = (ATTN_WIDTH, 2 * KV_WIDTH, 2 * KV_WIDTH, CONV_WIDTH, LRU_WIDTH, LRU_WIDTH)
    dtypes = (bf16, bf16, bf16, f32, f32, f32)
    return pl.pallas_call(
        functools.partial(_inproj_kernel, rope, tiles_per_seq, fixed_row),
        grid=(rows // tm,),
        in_specs=in_specs,
        out_specs=[row_spec(w) for w in widths],
        out_shape=[jax.ShapeDtypeStruct((rows, w), d) for w, d in zip(widths, dtypes)],
        compiler_params=_params(("arbitrary",)),
        name="inproj_rope" if rope else "inproj_ctx",
    )(*args)


def _attn_block(sink_ref, q, k_parts, v_parts, prev_ok, next_ok):
    tq = q.shape[0]
    low_half = lax.broadcasted_iota(jnp.int32, (tq, LANES), 1) < HEAD_DIM
    zero = jnp.zeros((tq, LANES), q.dtype)
    outs = []
    for g in range(N_KV_HEADS):
        gs = slice(g * LANES, (g + 1) * LANES)
        pairs = [q[:, (2 * g + i) * LANES:(2 * g + i + 1) * LANES] for i in range(GROUP // 2)]
        q4 = jnp.concatenate(
            [jnp.where(low_half if r % 2 == 0 else ~low_half, pairs[r // 2], zero) for r in range(GROUP)], axis=0)
        kcat = jnp.concatenate([p[:, gs] for p in k_parts], axis=0)
        vcat = jnp.concatenate([p[:, gs] for p in v_parts], axis=0)
        s = lax.dot_general(q4, kcat, (((1,), (1,)), ((), ())), preferred_element_type=f32)
        ps, dens = [], []
        for r0 in range(0, GROUP * tq, ATTN_ROWS):
            sc = s[r0:r0 + ATTN_ROWS]
            if prev_ok is not None:
                rs = slice(r0 % tq, r0 % tq + ATTN_ROWS)
                sc = jnp.concatenate([
                    jnp.where(prev_ok[rs], sc[:, 0:BLOCK], NEG_INF),
                    sc[:, BLOCK:2 * BLOCK],
                    jnp.where(next_ok[rs], sc[:, 2 * BLOCK:3 * BLOCK], NEG_INF),
                    sc[:, 3 * BLOCK:]], axis=1)
            sink = sink_ref[g * GROUP + r0 // tq]
            m = jnp.maximum(jnp.max(sc, axis=-1, keepdims=True), sink)
            p = jnp.exp(sc - m)
            dens.append(jnp.sum(p, axis=-1, keepdims=True) + jnp.exp(sink - m))
            ps.append(p.astype(bf16))
        o = _mm(jnp.concatenate(ps, axis=0), vcat) / jnp.concatenate(dens, axis=0)
        for i in range(GROUP // 2):
            outs.append(jnp.where(low_half, o[2 * i * tq:(2 * i + 1) * tq], o[(2 * i + 1) * tq:(2 * i + 2) * tq]))
    return jnp.concatenate(outs, axis=1)


def _attn_window_kernel(nsteps, sink_ref, q_ref, kp, ko, kn, vp, vo, vn, kc, vc, o_ref):
    j = pl.program_id(1)
    qi = lax.broadcasted_iota(jnp.int32, (BLOCK, BLOCK), 0)
    kj = lax.broadcasted_iota(jnp.int32, (BLOCK, BLOCK), 1)
    first_prev_ok = kj + jnp.where(j > 0, 0, -2 * BLOCK) >= qi
    last_next_ok = kj + jnp.where(j < nsteps - 1, 0, 2 * BLOCK) <= qi
    own = lambda ref: [ref[0, b * BLOCK:(b + 1) * BLOCK] for b in range(ATTN_QBLOCKS)]
    k_blocks = [kp[0]] + own(ko) + [kn[0]]
    v_blocks = [vp[0]] + own(vo) + [vn[0]]
    for b in range(ATTN_QBLOCKS):
        out = _attn_block(sink_ref, q_ref[0, b * BLOCK:(b + 1) * BLOCK],
                          k_blocks[b:b + 3] + [kc[0]], v_blocks[b:b + 3] + [vc[0]],
                          first_prev_ok if b == 0 else kj >= qi,
                          last_next_ok if b == ATTN_QBLOCKS - 1 else kj <= qi)
        o_ref[0, b * BLOCK:(b + 1) * BLOCK, :] = out.astype(bf16)


def _attn_ctx_kernel(sink_ref, q_ref, kc, vc, o_ref):
    o_ref[0] = _attn_block(sink_ref, q_ref[0], [kc[0]], [vc[0]], None, None).astype(bf16)


def _attention(sink, q, k, v, kc, vc):
    bsz, seq, _ = q.shape
    ctx_len = kc.shape[1]
    kvw = kc.shape[2]
    ctx_spec = pl.BlockSpec((1, ctx_len, kvw), lambda b, j: (b, 0, 0))
    sink_spec = pl.BlockSpec(memory_space=pltpu.SMEM)
    if k is None:
        tq, nsteps = seq, 1
        body = _attn_ctx_kernel
        in_specs, args = [ctx_spec, ctx_spec], [kc, vc]
    else:
        tq = ATTN_QBLOCKS * BLOCK
        nsteps = seq // tq
        nblk = seq // BLOCK
        body = functools.partial(_attn_window_kernel, nsteps)
        prev = pl.BlockSpec((1, BLOCK, kvw), lambda b, j: (b, jnp.maximum(ATTN_QBLOCKS * j - 1, 0), 0))
        own = pl.BlockSpec((1, tq, kvw), lambda b, j: (b, j, 0))
        nxt = pl.BlockSpec((1, BLOCK, kvw), lambda b, j: (b, jnp.minimum(ATTN_QBLOCKS * (j + 1), nblk - 1), 0))
        in_specs = [prev, own, nxt, prev, own, nxt, ctx_spec, ctx_spec]
        args = [k, k, k, v, v, v, kc, vc]
    return pl.pallas_call(
        body,
        grid=(bsz, nsteps),
        in_specs=[sink_spec, pl.BlockSpec((1, tq, ATTN_WIDTH), lambda b, j: (b, j, 0))] + in_specs,
        out_specs=pl.BlockSpec((1, tq, ATTN_WIDTH), lambda b, j: (b, j, 0)),
        out_shape=jax.ShapeDtypeStruct((bsz, seq, ATTN_WIDTH), bf16),
        compiler_params=_params(("arbitrary", "arbitrary")),
        name="attn_ctx" if k is None else "attn_window",
    )(sink, q, *args)


SCAN_PAD = 8


def _lru_kernel(seq, ctx_len, ct,
                xl_ref, gl_ref, xc_ref, gc_ref, cw_ref, cb_ref, wa_ref, ba_ref, wx_ref, bx_ref, lam_ref,
                yl_ref, yc_ref,
                xp_ref, u_ref, hf_ref, seg_ref, run_ref):
    lane_tiles = [slice(c * LANES, (c + 1) * LANES) for c in range(ct // LANES)]
    neg_lam = -lam_ref[...]
    softplus_neg_lam = jnp.maximum(neg_lam, 0.0) + jnp.log1p(jnp.exp(-jnp.abs(neg_lam)))

    def conv_into_u(src_ref, n):
        zero = jnp.zeros((SCAN_PAD, LANES), f32)
        for c in range(ct // LANES):
            cs = slice(c * LANES, (c + 1) * LANES)
            xp_ref[c, 0:SCAN_PAD] = zero
            xp_ref[c, SCAN_PAD + n:2 * SCAN_PAD + n] = zero
            for r0 in range(0, n, SCAN_CHUNK):
                xp_ref[c, SCAN_PAD + r0:SCAN_PAD + r0 + SCAN_CHUNK] = src_ref[0, r0:r0 + SCAN_CHUNK, cs]
            for r0 in range(0, n, SCAN_CHUNK):
                acc = jnp.broadcast_to(cb_ref[:, cs], (SCAN_CHUNK, LANES))
                for k in range(LRU_CONV):
                    off = SCAN_PAD + r0 + k - LRU_PAD_LEFT
                    acc = acc + cw_ref[k:k + 1, cs] * xp_ref[c, off:off + SCAN_CHUNK, :]
                u_ref[r0:r0 + SCAN_CHUNK, cs] = acc

    half_scale = (0.5 * LRU_C) * softplus_neg_lam

    def stage(buf, r0, d):
        uc = u_ref[pl.ds(r0, SCAN_CHUNK)]
        ub = uc.astype(bf16)
        half_za = _mm(ub, wa_ref[d]) + ba_ref[d:d + 1]
        half_zx = _mm(ub, wx_ref[d]) + bx_ref[d:d + 1]
        for s in range(SCAN_SEGS):
            rs = slice(s * SCAN_SEG, (s + 1) * SCAN_SEG)
            neg_log_a = (jnp.tanh(half_za[rs]) + 1.0) * half_scale[d:d + 1]
            a = jnp.exp2(neg_log_a * (-LOG2_E))
            y = jnp.tanh(neg_log_a) * (a * a + 1.0)
            root = jnp.where(y > 0.0, y * lax.rsqrt(y), 0.0)
            b = root * ((0.5 * jnp.tanh(half_zx[rs]) + 0.5) * uc[rs])
            lo = s * SCAN_SEG_PITCH
            for c, cs in enumerate(lane_tiles):
                seg_ref[buf, d, 0, c, lo:lo + SCAN_SEG] = a[:, cs]
                seg_ref[buf, d, 1, c, lo:lo + SCAN_SEG] = b[:, cs]

    def scan(buf, d, carry):
        hl = [jnp.zeros((SCAN_SEGS, LANES), f32) for _ in lane_tiles]
        pa = [jnp.ones((SCAN_SEGS, LANES), f32) for _ in lane_tiles]
        for i in range(SCAN_SEG):
            t = i if d == 0 else SCAN_SEG - 1 - i
            rows = pl.ds(t, SCAN_SEGS, stride=SCAN_SEG_PITCH)
            for c in range(len(lane_tiles)):
                av = seg_ref[buf, d, 0, c, rows]
                hl[c] = av * hl[c] + seg_ref[buf, d, 1, c, rows]
                pa[c] = av * pa[c]
                run_ref[d, 0, c, rows] = hl[c]
                run_ref[d, 1, c, rows] = pa[c]
        order = range(SCAN_SEGS) if d == 0 else range(SCAN_SEGS - 1, -1, -1)
        states, carries = [], []
        for c, cs in enumerate(lane_tiles):
            pieces, cur = [None] * SCAN_SEGS, carry[:, cs]
            for s in order:
                lo = s * SCAN_SEG_PITCH
                pieces[s] = run_ref[d, 0, c, lo:lo + SCAN_SEG] + run_ref[d, 1, c, lo:lo + SCAN_SEG] * cur
                cur = pa[c][s:s + 1] * cur + hl[c][s:s + 1]
            states.append(jnp.concatenate(pieces, axis=0))
            carries.append(cur)
        return jnp.concatenate(states, axis=1), jnp.concatenate(carries, axis=1)

    conv_into_u(xc_ref, ctx_len)
    zero_row = jnp.zeros((1, ct), f32)
    assert ctx_len == SCAN_CHUNK
    stage(0, 0, 0)
    stage(0, 0, 1)
    cf, carry_f = scan(0, 0, zero_row)
    cr, carry_r = scan(0, 1, zero_row)
    yc_ref[0] = ((cf + cr) * gc_ref[0]).astype(bf16)

    conv_into_u(xl_ref, seq)
    nchunk = seq // SCAN_CHUNK
    fwd_start = lambda c: pl.multiple_of(c * SCAN_CHUNK, SCAN_CHUNK)
    rev_start = lambda c: pl.multiple_of((nchunk - 1 - c) * SCAN_CHUNK, SCAN_CHUNK)
    stage(0, fwd_start(0), 0)
    stage(0, rev_start(0), 1)

    assert nchunk % 4 == 0

    def emit(first_visit, r0, states):
        if first_visit:
            hf_ref[pl.ds(r0, SCAN_CHUNK)] = states
        else:
            yl_ref[0, pl.ds(r0, SCAN_CHUNK), :] = (
                (states + hf_ref[pl.ds(r0, SCAN_CHUNK)])
                * gl_ref[0, pl.ds(r0, SCAN_CHUNK), :]).astype(bf16)

    def pair_body(first_visit, it, carries):
        carry_f, carry_r = carries
        for buf in range(2):
            c = 2 * it + buf
            nxt = jnp.minimum(c + 1, nchunk - 1)
            stage(1 - buf, fwd_start(nxt), 0)
            stage(1 - buf, rev_start(nxt), 1)
            out_f, carry_f = scan(buf, 0, carry_f)
            out_r, carry_r = scan(buf, 1, carry_r)
            emit(first_visit, fwd_start(c), out_f)
            emit(first_visit, rev_start(c), out_r)
        return carry_f, carry_r

    carries = lax.fori_loop(0, nchunk // 4, functools.partial(pair_body, True), (carry_f, carry_r))
    lax.fori_loop(nchunk // 4, nchunk // 2, functools.partial(pair_body, False), carries)


def _rglru(xl, gl, xc, gc, conv_w, conv_b, wa_bd, ba, wx_bd, bx, lam, ct):
    bsz, seq, _ = xl.shape
    ctx_len = xc.shape[1]
    seg_rows = SCAN_SEGS * SCAN_SEG_PITCH
    col = lambda rows: pl.BlockSpec((1, rows, ct), lambda b, c: (b, 0, c))
    vec = lambda rows: pl.BlockSpec((rows, ct), lambda b, c: (0, c))
    bd = pl.BlockSpec((2, ct, ct), lambda b, c: (0, c, c))
    return pl.pallas_call(
        functools.partial(_lru_kernel, seq, ctx_len, ct),
        grid=(bsz, LRU_WIDTH // ct),
        in_specs=[col(seq), col(seq), col(ctx_len), col(ctx_len),
                  vec(LRU_CONV), vec(1), bd, vec(2), bd, vec(2), vec(2)],
        out_specs=[col(seq), col(ctx_len)],
        out_shape=[jax.ShapeDtypeStruct((bsz, seq, LRU_WIDTH), bf16),
                   jax.ShapeDtypeStruct((bsz, ctx_len, LRU_WIDTH), bf16)],
        scratch_shapes=[pltpu.VMEM((ct // LANES, seq + 2 * SCAN_PAD, LANES), f32), pltpu.VMEM((seq, ct), f32),
                        pltpu.VMEM((seq, ct), f32), pltpu.VMEM((2, 2, 2, ct // LANES, seg_rows, LANES), f32),
                        pltpu.VMEM((2, 2, ct // LANES, seg_rows, LANES), f32)],
        compiler_params=_params(("arbitrary", "arbitrary")),
        name="rglru",
    )(xl, gl, xc, gc, conv_w, conv_b.reshape(1, LRU_WIDTH), wa_bd, ba, wx_bd, bx, lam)


def _conv_ln_swish(tt, first, last, x_ref, prev_rows, next_rows, w_ref, b_ref, g_ref, beta_ref, win_ref, y_ref):
    lane_tiles = [slice(c * LANES, (c + 1) * LANES) for c in range(CONV_WIDTH // LANES)]
    zero = jnp.zeros((CONV_HALO, LANES), f32)
    for c, cs in enumerate(lane_tiles):
        win_ref[c, CONV_HALO:CONV_HALO + tt] = x_ref[:, cs]

    def fill(lo, rows):
        for c, cs in enumerate(lane_tiles):
            win_ref[c, lo:lo + CONV_HALO] = zero if rows is None else rows(cs)
    pl.when(jnp.logical_not(first))(lambda: fill(0, prev_rows))
    pl.when(first)(lambda: fill(0, None))
    pl.when(jnp.logical_not(last))(lambda: fill(CONV_HALO + tt, next_rows))
    pl.when(last)(lambda: fill(CONV_HALO + tt, None))

    gain, beta = g_ref[...], beta_ref[...]

    def rows(r0):
        pieces = []
        for c, cs in enumerate(lane_tiles):
            acc = jnp.broadcast_to(b_ref[:, cs], (CONV_ROWS, LANES))
            for k in range(CONV_KERNEL):
                off = r0 + CONV_HALO - CONV_PAD + k
                acc = acc + jnp.broadcast_to(w_ref[k:k + 1, cs], (CONV_ROWS, LANES)) * win_ref[c, off:off + CONV_ROWS, :]
            pieces.append(acc)
        acc = jnp.concatenate(pieces, axis=1)
        mu = jnp.mean(acc, axis=-1, keepdims=True)
        xc = acc - mu
        var = jnp.mean(xc * xc, axis=-1, keepdims=True)
        y = xc * lax.rsqrt(var + EPS) * gain + beta
        y = (y * jax.nn.sigmoid(y)).astype(bf16)
        y_ref[r0:r0 + CONV_ROWS, :] = y
        return y
    return rows


def _tail_kernel(final, tiles_per_seq, fixed_row,
                 one_ref, h_ref, ya_ref, hc_ref, hc_next_ref, hc_after_ref, yc_ref, mod_ref, g1_ref, g2_ref, gf_ref,
                 cw_ref, cb_ref, cg_ref, cbeta_ref,
                 wg_ref, woa_ref, wob_ref, woc_ref, wout_ref, wup_ref, wdn_ref, o_ref,
                 act_ref, win_ref, yb_ref, yb_next_ref):
    i = pl.program_id(0)
    tm = h_ref.shape[0]
    conv = functools.partial(_conv_ln_swish, tm, w_ref=cw_ref, b_ref=cb_ref, g_ref=cg_ref, beta_ref=cbeta_ref,
                             win_ref=win_ref, y_ref=yb_next_ref)
    head_of_next = lambda cs: hc_next_ref[0:CONV_HALO, cs]

    @pl.when(i == 0)
    def _():
        rows = conv(True, tiles_per_seq == 1, hc_ref, None, head_of_next)
        for r0 in range(0, tm, CONV_ROWS):
            rows(r0)
    yb_ref[...] = yb_next_ref[...]
    next_in_seq = (i + 1) % tiles_per_seq
    conv_rows = conv(next_in_seq == 0, next_in_seq == tiles_per_seq - 1, hc_next_ref,
                     lambda cs: hc_ref[tm - CONV_HALO:tm, cs], lambda cs: hc_after_ref[:, cs])
    row = fixed_row if fixed_row is not None else i // tiles_per_seq
    mod = lambda n: mod_ref[pl.ds(row, 1), n * D_MODEL:(n + 1) * D_MODEL]
    x = h_ref[...]
    a = _norm_mod(x, g1_ref[...], mod(0), mod(1)).astype(bf16)
    col_blocks = [slice(n0, n0 + OUT_CHUNK) for n0 in range(0, D_MODEL, OUT_CHUNK)]
    nffn = FFN_HIDDEN // FFN_CHUNK
    always = one_ref[0] == 1
    assert OUT_CHUNK == FFN_CHUNK
    anchors = len(col_blocks) + nffn
    nconv = tm // CONV_ROWS

    def tied(val, anchor):
        pieces, done = [], 0
        for rc in range(anchor * nconv // anchors, (anchor + 1) * nconv // anchors):
            r0 = rc * CONV_ROWS
            y = conv_rows(r0)
            pieces += [val[done:r0]] * (r0 > done) + [jnp.where(always, val[r0:r0 + CONV_ROWS], y[:, 0:FFN_CHUNK])]
            done = r0 + CONV_ROWS
        return jnp.concatenate(pieces + [val[done:]] * (done < tm), axis=0) if pieces else val

    branches = ((ya_ref, woa_ref), (yb_ref, wob_ref), (yc_ref, woc_ref))
    merged = []
    for nb, cs in enumerate(col_blocks):
        total = None
        for n, (y_ref, w_ref) in enumerate(branches):
            gate = jax.nn.sigmoid(_mm(a, wg_ref[:, n * D_MODEL + cs.start:n * D_MODEL + cs.stop]))
            term = gate * _mm(y_ref[...], w_ref[:, cs])
            total = term if total is None else total + term
        merged.append(tied(total.astype(bf16), nb))
    h1 = x + mod(2) * _mm(jnp.concatenate(merged, axis=1), wout_ref[...])
    a2 = _norm_mod(h1, g2_ref[...], mod(3), mod(4)).astype(bf16)

    for j, c0 in enumerate(range(0, FFN_HIDDEN, FFN_CHUNK)):
        up = _mm(a2, wup_ref[:, c0:c0 + FFN_CHUNK])
        gate = _mm(a2, wup_ref[:, FFN_HIDDEN + c0:FFN_HIDDEN + c0 + FFN_CHUNK])
        act = ((gate * jax.nn.sigmoid(gate)) * up).astype(bf16)
        act_ref[:, c0:c0 + FFN_CHUNK] = tied(act, len(col_blocks) + j)
    gate2 = mod(5)
    h2 =[h1[:, cs] + gate2[:, cs] * _mm(act_ref[...], wdn_ref[:, cs]) for cs in col_blocks]
    if final:
        sum_sq = functools.reduce(lambda u, v: u + v, [jnp.sum(p * p, axis=-1, keepdims=True) for p in h2])
        inv_rms = lax.rsqrt(sum_sq * (1.0 / D_MODEL) + EPS)
        h2 = [p * inv_rms * gf_ref[:, cs] for p, cs in zip(h2, col_blocks)]
    for p, cs in zip(h2, col_blocks):
        o_ref[:, cs] = p


def _tail(h, ya, hc, yc, mod, g1, g2, gf, conv, wts, layer, tm, tiles_per_seq, fixed_row, final):
    rows = h.shape[0]
    row_spec = lambda width: pl.BlockSpec((tm, width), lambda i: (i, 0))
    ntiles = rows // tm
    hpt = tm // CONV_HALO
    nhalo = rows // CONV_HALO
    wg, woa, wob, woc, wout, wup, wdn = wts
    cw, cb, cg, cbeta = conv
    vec = lambda v: v.reshape(1, -1)
    return pl.pallas_call(
        functools.partial(_tail_kernel, final, tiles_per_seq, fixed_row),
        grid=(rows // tm,),
        in_specs=[pl.BlockSpec(memory_space=pltpu.SMEM),
                  row_spec(D_MODEL), row_spec(ATTN_WIDTH), row_spec(CONV_WIDTH),
                  pl.BlockSpec((tm, CONV_WIDTH), lambda i: (jnp.minimum(i + 1, ntiles - 1), 0)),
                  pl.BlockSpec((CONV_HALO, CONV_WIDTH), lambda i: (jnp.minimum((i + 2) * hpt, nhalo - 1), 0)),
                  row_spec(LRU_WIDTH),
                  _const_spec((MOD_ROWS, 6 * D_MODEL)),
                  _const_spec((1, D_MODEL)), _const_spec((1, D_MODEL)), _const_spec((1, D_MODEL)),
                  _const_spec(cw.shape), _const_spec((1, CONV_WIDTH)), _const_spec((1, CONV_WIDTH)),
                  _const_spec((1, CONV_WIDTH)),
                  ] + [_layer_spec(w, layer) for w in wts],
        out_specs=row_spec(D_MODEL),
        out_shape=jax.ShapeDtypeStruct((rows, D_MODEL), f32),
        scratch_shapes=[pltpu.VMEM((tm, FFN_HIDDEN), bf16),
                        pltpu.VMEM((CONV_WIDTH // LANES, tm + 2 * CONV_HALO, LANES), f32),
                        pltpu.VMEM((tm, CONV_WIDTH), bf16), pltpu.VMEM((tm, CONV_WIDTH), bf16)],
        compiler_params=_params(("arbitrary",)),
        name="merge_out_swiglu",
    )(jnp.ones((1,), jnp.int32), h, ya, hc, hc, hc, yc, mod, g1, g2, gf, cw, vec(cb), vec(cg), vec(cbeta),
      wg, woa, wob, woc, wout, wup, wdn)


def _rope_tables(seq):
    rows = seq // GRID_W
    inv = jnp.power(ROPE_BASE, -jnp.arange(ROPE_FREQS, dtype=f32) / ROPE_FREQS)
    row_ang = jnp.arange(rows, dtype=f32)[:, None] * inv[None]
    col_ang = jnp.arange(GRID_W, dtype=f32)[:, None] * inv[None]

    def table(fn):
        by_row = jnp.broadcast_to(fn(row_ang)[:, None, :], (rows, GRID_W, ROPE_FREQS))
        by_col = jnp.broadcast_to(fn(col_ang)[None, :, :], (rows, GRID_W, ROPE_FREQS))
        return jnp.concatenate([by_row, by_col], axis=-1).reshape(seq, 2 * ROPE_FREQS)
    cos, sin = table(jnp.cos), table(jnp.sin)
    reps = LANES // HEAD_DIM
    return jnp.tile(jnp.concatenate([cos, cos], axis=-1), (1, reps)), \
        jnp.tile(jnp.concatenate([-sin, sin], axis=-1), (1, reps))


def _block_diag(w):
    two, nb, d, e = w.shape
    eye = jnp.eye(nb, dtype=w.dtype)
    return jnp.einsum('xnde,nm->xndme', w, eye).reshape(two, nb * d, nb * e)


def kernel(x, c, ctx, c_ctx, mod_w, mod_b, norm1_g, norm2_g, w_in, attn_sink, conv_dw_w, conv_dw_b, conv_ln_g,
           conv_ln_b, lru_conv_w, lru_conv_b, lru_wa, lru_ba, lru_wx, lru_bx, lru_lam, w_o_attn, w_o_conv,
           w_o_lru, w_out, ffn_w_up, ffn_w_down, final_norm_g):
    bsz, seq, _ = x.shape
    ctx_len = ctx.shape[1]
    depth = mod_w.shape[0]
    assert bsz + 1 <= MOD_ROWS and seq % (ATTN_QBLOCKS * BLOCK) == 0 and ctx_len == SCAN_CHUNK
    tm = 512
    tiles_per_seq = seq // tm
    tm_ctx = ctx_len
    ctx_row = bsz

    cvec = jnp.zeros((MOD_ROWS, D_MODEL), f32).at[:bsz].set(c).at[ctx_row].set(c_ctx)
    mod_all = _modulation(cvec, mod_w, mod_b)
    cos, sin = _rope_tables(seq)
    row = lambda v: v.reshape(1, -1)

    w_mix = w_in[:, :, :MIX_COLS].astype(bf16)
    wts = (w_in[:, :, MIX_COLS:].astype(bf16), w_o_attn.astype(bf16), w_o_conv.astype(bf16),
           w_o_lru.astype(bf16), w_out.astype(bf16), ffn_w_up.astype(bf16), ffn_w_down.astype(bf16))

    h_lat = x.reshape(bsz * seq, D_MODEL)
    h_ctx = ctx.reshape(bsz * ctx_len, D_MODEL)
    for l in range(depth):
        need_ctx = l < depth - 1
        mod = mod_all[l]
        g1, g2 = row(norm1_g[l]), row(norm2_g[l])

        q, k, v, hc, lx, gl = _inproj(h_lat, mod, g1, w_mix, l, tm, tiles_per_seq, None, cos, sin)
        qc, kc, vc, hcc, lxc, glc = _inproj(h_ctx, mod, g1, w_mix, l, tm_ctx, 1, ctx_row)
        shp = lambda t, n: t.reshape(bsz, n, t.shape[-1])
        kc3, vc3 = shp(kc, ctx_len), shp(vc, ctx_len)

        y_attn = _attention(attn_sink[l], shp(q, seq), shp(k, seq), shp(v, seq), kc3, vc3)
        conv = (conv_dw_w[l], conv_dw_b[l], conv_ln_g[l], conv_ln_b[l])
        y_lru, y_lru_c = _rglru(shp(lx, seq), shp(gl, seq), shp(lxc, ctx_len), shp(glc, ctx_len),
                                lru_conv_w[l], lru_conv_b[l],
                                _block_diag(0.5 * lru_wa[l]).astype(bf16), 0.5 * lru_ba[l],
                                _block_diag(0.5 * lru_wx[l]).astype(bf16), 0.5 * lru_bx[l], lru_lam[l], LRU_TILE)
        flat = lambda t: t.reshape(-1, t.shape[-1])
        h_lat = _tail(h_lat, flat(y_attn), hc, flat(y_lru), mod, g1, g2, row(final_norm_g), conv, wts, l,
                      tm, tiles_per_seq, None, final=not need_ctx)
        if need_ctx:
            y_attn_c = _attention(attn_sink[l], shp(qc, ctx_len), None, None, kc3, vc3)
            h_ctx = _tail(h_ctx, flat(y_attn_c), hcc, flat(y_lru_c), mod, g1, g2,
                          row(final_norm_g), conv, wts, l, tm_ctx, 1, ctx_row, final=False)
    return h_lat.reshape(bsz, seq, D_MODEL)
```

```python
import functools

import jax
import jax.numpy as jnp
from jax import lax
from jax.experimental import pallas as pl
from jax.experimental.pallas import tpu as pltpu

D_MODEL = 1024
GRID_W = 64
N_HEADS = 8
N_KV_HEADS = 2
GROUP = N_HEADS // N_KV_HEADS
HEAD_DIM = 64
ATTN_WIDTH = N_HEADS * HEAD_DIM
KV_WIDTH = N_KV_HEADS * HEAD_DIM
BLOCK = 128
ATTN_SCALE = HEAD_DIM ** -0.5
LOG2_E = 1.4426950408889634
ROPE_BASE = 10000.0
ROPE_FREQS = HEAD_DIM // 4
CONV_WIDTH = 512
CONV_KERNEL = 31
CONV_PAD = (CONV_KERNEL - 1) // 2
LRU_WIDTH = 512
LRU_BLOCKS = 8
LRU_BLOCK_DIM = LRU_WIDTH // LRU_BLOCKS
LRU_CONV = 4
LRU_PAD_LEFT = 2
LRU_C = 8.0
FFN_HIDDEN = 2816
MIX_COLS = ATTN_WIDTH + 2 * KV_WIDTH + 2 * CONV_WIDTH + 2 * LRU_WIDTH
EPS = 1e-6
NEG_INF = -1e30

LANES = 128
SUBLANES = 8
VMEM_LIMIT_BYTES = 56 * 1024 * 1024

MOD_ROWS = 8
FFN_CHUNK = 256
OUT_CHUNK = 256
ATTN_ROWS = 32
ATTN_QBLOCKS = 4
CONV_HALO = 16
CONV_ROWS = 32
LRU_TILE = 128
SCAN_CHUNK = 256
SCAN_SEGS = SUBLANES
SCAN_SEG = SCAN_CHUNK // SCAN_SEGS
SCAN_SEG_PITCH = SCAN_SEG + 8

f32 = jnp.float32
bf16 = jnp.bfloat16


def _mm(a, b):
    return jnp.dot(a, b, preferred_element_type=f32)


def _const_spec(shape):
    n = len(shape)
    return pl.BlockSpec(shape, lambda *_: (0,) * n, pipeline_mode=pl.Buffered(1))


def _layer_spec(stacked, layer):
    tail = (0,) * (stacked.ndim - 1)
    return pl.BlockSpec((None,) + stacked.shape[1:], lambda *_: (layer,) + tail, pipeline_mode=pl.Buffered(1))


def _params(sem):
    return pltpu.CompilerParams(dimension_semantics=sem, vmem_limit_bytes=VMEM_LIMIT_BYTES)


def _norm_mod(x, g, shift, scale):
    y = x * lax.rsqrt(jnp.mean(x * x, axis=-1, keepdims=True) + EPS)
    return (y * g) * (1.0 + scale) + shift


def _mod_kernel(c_ref, w_ref, b_ref, o_ref):
    c = c_ref[...]
    s = (c * jax.nn.sigmoid(c)).astype(bf16)
    o_ref[0] = _mm(s, w_ref[0].astype(bf16)) + b_ref[0]


def _modulation(cvec, mod_w, mod_b):
    depth, _, cols = mod_w.shape
    tn = 1024
    return pl.pallas_call(
        _mod_kernel,
        grid=(depth, cols // tn),
        in_specs=[
            pl.BlockSpec((MOD_ROWS, D_MODEL), lambda l, j: (0, 0)),
            pl.BlockSpec((1, D_MODEL, tn), lambda l, j: (l, 0, j)),
            pl.BlockSpec((1, 1, tn), lambda l, j: (l, 0, j)),
        ],
        out_specs=pl.BlockSpec((1, MOD_ROWS, tn), lambda l, j: (l, 0, j)),
        out_shape=jax.ShapeDtypeStruct((depth, MOD_ROWS, cols), f32),
        compiler_params=_params(("arbitrary", "arbitrary")),
        name="modulation",
    )(cvec, mod_w, mod_b.reshape(depth, 1, cols))


def _rope(z, cos, sin_signed):
    lane = lax.broadcasted_iota(jnp.int32, z.shape, 1)
    first_half = (lane & (HEAD_DIM - 1)) < HEAD_DIM // 2
    partner = jnp.where(first_half, pltpu.roll(z, LANES - HEAD_DIM // 2, axis=1),
                        pltpu.roll(z, HEAD_DIM // 2, axis=1))
    return z * cos + partner * sin_signed


def _dup_heads(z):
    low_half = lax.broadcasted_iota(jnp.int32, z.shape, 1) < HEAD_DIM
    swapped = pltpu.roll(z, HEAD_DIM, axis=1)
    return jnp.concatenate([jnp.where(low_half, z, swapped), jnp.where(low_half, swapped, z)], axis=1)


def _inproj_kernel(rope, tiles_per_seq, fixed_row, h_ref, mod_ref, g_ref, w_ref, *rest):
    if rope:
        cos_ref, sin_ref, q_ref, k_ref, v_ref, hc_ref, lx_ref, gl_ref = rest
    else:
        q_ref, k_ref, v_ref, hc_ref, lx_ref, gl_ref = rest
    row = fixed_row if fixed_row is not None else pl.program_id(0) // tiles_per_seq
    shift = mod_ref[pl.ds(row, 1), 0:D_MODEL]
    scale = mod_ref[pl.ds(row, 1), D_MODEL:2 * D_MODEL]
    a = _norm_mod(h_ref[...], g_ref[...], shift, scale).astype(bf16)

    if rope:
        cos, sin = cos_ref[...], sin_ref[...]
        fix = lambda z: _rope(z, cos, sin)
    else:
        fix = lambda z: z
    zq = _mm(a, w_ref[:, 0:ATTN_WIDTH])
    for j in range(ATTN_WIDTH // LANES):
        zj = fix(zq[:, j * LANES:(j + 1) * LANES]) * ATTN_SCALE
        q_ref[:, j * LANES:(j + 1) * LANES] = zj.astype(bf16)
    c0 = ATTN_WIDTH
    zkv = _mm(a, w_ref[:, c0:c0 + 2 * KV_WIDTH])
    k_ref[...] = _dup_heads(fix(zkv[:, 0:KV_WIDTH])).astype(bf16)
    v_ref[...] = _dup_heads(zkv[:, KV_WIDTH:]).astype(bf16)
    c0 += 2 * KV_WIDTH
    val = _mm(a, w_ref[:, c0:c0 + CONV_WIDTH])
    gate = _mm(a, w_ref[:, c0 + CONV_WIDTH:c0 + 2 * CONV_WIDTH])
    hc_ref[...] = val * jax.nn.sigmoid(gate)
    c0 += 2 * CONV_WIDTH
    gl_ref[...] = jax.nn.gelu(_mm(a, w_ref[:, c0 + LRU_WIDTH:c0 + 2 * LRU_WIDTH]))
    lx_ref[...] = _mm(a, w_ref[:, c0:c0 + LRU_WIDTH])


def _inproj(h, mod, g, w_mix, layer, tm, tiles_per_seq, fixed_row, cos=None, sin=None):
    rows = h.shape[0]
    rope = cos is not None
    row_spec = lambda width: pl.BlockSpec((tm, width), lambda i: (i, 0))
    in_specs = [row_spec(D_MODEL), _const_spec((MOD_ROWS, 6 * D_MODEL)), _const_spec((1, D_MODEL)),
                _layer_spec(w_mix, layer)]
    args = [h, mod, g, w_mix]
    if rope:
        tab = pl.BlockSpec((tm, LANES), lambda i: (i % tiles_per_seq, 0))
        in_specs += [tab, tab]
        args += [cos, sin]
    widths = (ATTN_WIDTH, 2 * KV_WIDTH, 2 * KV_WIDTH, CONV_WIDTH, LRU_WIDTH, LRU_WIDTH)
    dtypes = (bf16, bf16, bf16, f32, f32, f32)
    return pl.pallas_call(
        functools.partial(_inproj_kernel, rope, tiles_per_seq, fixed_row),
        grid=(rows // tm,),
        in_specs=in_specs,
        out_specs=[row_spec(w) for w in widths],
        out_shape=[jax.ShapeDtypeStruct((rows, w), d) for w, d in zip(widths, dtypes)],
        compiler_params=_params(("arbitrary",)),
        name="inproj_rope" if rope else "inproj_ctx",
    )(*args)


def _attn_block(sink_ref, q, k_parts, v_parts, prev_ok, next_ok):
    tq = q.shape[0]
    low_half = lax.broadcasted_iota(jnp.int32, (tq, LANES), 1) < HEAD_DIM
    zero = jnp.zeros((tq, LANES), q.dtype)
    outs = []
    for g in range(N_KV_HEADS):
        gs = slice(g * LANES, (g + 1) * LANES)
        pairs = [q[:, (2 * g + i) * LANES:(2 * g + i + 1) * LANES] for i in range(GROUP // 2)]
        q4 = jnp.concatenate(
            [jnp.where(low_half if r % 2 == 0 else ~low_half, pairs[r // 2], zero) for r in range(GROUP)], axis=0)
        kcat = jnp.concatenate([p[:, gs] for p in k_parts], axis=0)
        vcat = jnp.concatenate([p[:, gs] for p in v_parts], axis=0)
        s = lax.dot_general(q4, kcat, (((1,), (1,)), ((), ())), preferred_element_type=f32)
        ps, dens = [], []
        for r0 in range(0, GROUP * tq, ATTN_ROWS):
            sc = s[r0:r0 + ATTN_ROWS]
            if prev_ok is not None:
                rs = slice(r0 % tq, r0 % tq + ATTN_ROWS)
                sc = jnp.concatenate([
                    jnp.where(prev_ok[rs], sc[:, 0:BLOCK], NEG_INF),
                    sc[:, BLOCK:2 * BLOCK],
                    jnp.where(next_ok[rs], sc[:, 2 * BLOCK:3 * BLOCK], NEG_INF),
                    sc[:, 3 * BLOCK:]], axis=1)
            sink = sink_ref[g * GROUP + r0 // tq]
            m = jnp.maximum(jnp.max(sc, axis=-1, keepdims=True), sink)
            p = jnp.exp(sc - m)
            dens.append(jnp.sum(p, axis=-1, keepdims=True) + jnp.exp(sink - m))
            ps.append(p.astype(bf16))
        o = _mm(jnp.concatenate(ps, axis=0), vcat) / jnp.concatenate(dens, axis=0)
        for i in range(GROUP // 2):
            outs.append(jnp.where(low_half, o[2 * i * tq:(2 * i + 1) * tq], o[(2 * i + 1) * tq:(2 * i + 2) * tq]))
    return jnp.concatenate(outs, axis=1)


def _attn_window_kernel(nsteps, sink_ref, q_ref, kp, ko, kn, vp, vo, vn, kc, vc, o_ref):
    j = pl.program_id(1)
    qi = lax.broadcasted_iota(jnp.int32, (BLOCK, BLOCK), 0)
    kj = lax.broadcasted_iota(jnp.int32, (BLOCK, BLOCK), 1)
    first_prev_ok = kj + jnp.where(j > 0, 0, -2 * BLOCK) >= qi
    last_next_ok = kj + jnp.where(j < nsteps - 1, 0, 2 * BLOCK) <= qi
    own = lambda ref: [ref[0, b * BLOCK:(b + 1) * BLOCK] for b in range(ATTN_QBLOCKS)]
    k_blocks = [kp[0]] + own(ko) + [kn[0]]
    v_blocks = [vp[0]] + own(vo) + [vn[0]]
    for b in range(ATTN_QBLOCKS):
        out = _attn_block(sink_ref, q_ref[0, b * BLOCK:(b + 1) * BLOCK],
                          k_blocks[b:b + 3] + [kc[0]], v_blocks[b:b + 3] + [vc[0]],
                          first_prev_ok if b == 0 else kj >= qi,
                          last_next_ok if b == ATTN_QBLOCKS - 1 else kj <= qi)
        o_ref[0, b * BLOCK:(b + 1) * BLOCK, :] = out.astype(bf16)


def _attn_ctx_kernel(sink_ref, q_ref, kc, vc, o_ref):
    o_ref[0] = _attn_block(sink_ref, q_ref[0], [kc[0]], [vc[0]], None, None).astype(bf16)


def _attention(sink, q, k, v, kc, vc):
    bsz, seq, _ = q.shape
    ctx_len = kc.shape[1]
    kvw = kc.shape[2]
    ctx_spec = pl.BlockSpec((1, ctx_len, kvw), lambda b, j: (b, 0, 0))
    sink_spec = pl.BlockSpec(memory_space=pltpu.SMEM)
    if k is None:
        tq, nsteps = seq, 1
        body = _attn_ctx_kernel
        in_specs, args = [ctx_spec, ctx_spec], [kc, vc]
    else:
        tq = ATTN_QBLOCKS * BLOCK
        nsteps = seq // tq
        nblk = seq // BLOCK
        body = functools.partial(_attn_window_kernel, nsteps)
        prev = pl.BlockSpec((1, BLOCK, kvw), lambda b, j: (b, jnp.maximum(ATTN_QBLOCKS * j - 1, 0), 0))
        own = pl.BlockSpec((1, tq, kvw), lambda b, j: (b, j, 0))
        nxt = pl.BlockSpec((1, BLOCK, kvw), lambda b, j: (b, jnp.minimum(ATTN_QBLOCKS * (j + 1), nblk - 1), 0))
        in_specs = [prev, own, nxt, prev, own, nxt, ctx_spec, ctx_spec]
        args = [k, k, k, v, v, v, kc, vc]
    return pl.pallas_call(
        body,
        grid=(bsz, nsteps),
        in_specs=[sink_spec, pl.BlockSpec((1, tq, ATTN_WIDTH), lambda b, j: (b, j, 0))] + in_specs,
        out_specs=pl.BlockSpec((1, tq, ATTN_WIDTH), lambda b, j: (b, j, 0)),
        out_shape=jax.ShapeDtypeStruct((bsz, seq, ATTN_WIDTH), bf16),
        compiler_params=_params(("arbitrary", "arbitrary")),
        name="attn_ctx" if k is None else "attn_window",
    )(sink, q, *args)


SCAN_PAD = 8


def _lru_kernel(seq, ctx_len, ct,
                xl_ref, gl_ref, xc_ref, gc_ref, cw_ref, cb_ref, wa_ref, ba_ref, wx_ref, bx_ref, lam_ref,
                yl_ref, yc_ref,
                xp_ref, u_ref, hf_ref, seg_ref, run_ref):
    lane_tiles = [slice(c * LANES, (c + 1) * LANES) for c in range(ct // LANES)]
    neg_lam = -lam_ref[...]
    softplus_neg_lam = jnp.maximum(neg_lam, 0.0) + jnp.log1p(jnp.exp(-jnp.abs(neg_lam)))

    def conv_into_u(src_ref, n):
        zero = jnp.zeros((SCAN_PAD, LANES), f32)
        for c in range(ct // LANES):
            cs = slice(c * LANES, (c + 1) * LANES)
            xp_ref[c, 0:SCAN_PAD] = zero
            xp_ref[c, SCAN_PAD + n:2 * SCAN_PAD + n] = zero
            for r0 in range(0, n, SCAN_CHUNK):
                xp_ref[c, SCAN_PAD + r0:SCAN_PAD + r0 + SCAN_CHUNK] = src_ref[0, r0:r0 + SCAN_CHUNK, cs]
            for r0 in range(0, n, SCAN_CHUNK):
                acc = jnp.broadcast_to(cb_ref[:, cs], (SCAN_CHUNK, LANES))
                for k in range(LRU_CONV):
                    off = SCAN_PAD + r0 + k - LRU_PAD_LEFT
                    acc = acc + cw_ref[k:k + 1, cs] * xp_ref[c, off:off + SCAN_CHUNK, :]
                u_ref[r0:r0 + SCAN_CHUNK, cs] = acc

    half_scale = (0.5 * LRU_C) * softplus_neg_lam

    def stage(buf, r0, d):
        uc = u_ref[pl.ds(r0, SCAN_CHUNK)]
        ub = uc.astype(bf16)
        half_za = _mm(ub, wa_ref[d]) + ba_ref[d:d + 1]
        half_zx = _mm(ub, wx_ref[d]) + bx_ref[d:d + 1]
        for s in range(SCAN_SEGS):
            rs = slice(s * SCAN_SEG, (s + 1) * SCAN_SEG)
            neg_log_a = (jnp.tanh(half_za[rs]) + 1.0) * half_scale[d:d + 1]
            a = jnp.exp2(neg_log_a * (-LOG2_E))
            y = jnp.tanh(neg_log_a) * (a * a + 1.0)
            root = jnp.where(y > 0.0, y * lax.rsqrt(y), 0.0)
            b = root * ((0.5 * jnp.tanh(half_zx[rs]) + 0.5) * uc[rs])
            lo = s * SCAN_SEG_PITCH
            for c, cs in enumerate(lane_tiles):
                seg_ref[buf, d, 0, c, lo:lo + SCAN_SEG] = a[:, cs]
                seg_ref[buf, d, 1, c, lo:lo + SCAN_SEG] = b[:, cs]

    def scan(buf, d, carry):
        hl = [jnp.zeros((SCAN_SEGS, LANES), f32) for _ in lane_tiles]
        pa = [jnp.ones((SCAN_SEGS, LANES), f32) for _ in lane_tiles]
        for i in range(SCAN_SEG):
            t = i if d == 0 else SCAN_SEG - 1 - i
            rows = pl.ds(t, SCAN_SEGS, stride=SCAN_SEG_PITCH)
            for c in range(len(lane_tiles)):
                av = seg_ref[buf, d, 0, c, rows]
                hl[c] = av * hl[c] + seg_ref[buf, d, 1, c, rows]
                pa[c] = av * pa[c]
                run_ref[d, 0, c, rows] = hl[c]
                run_ref[d, 1, c, rows] = pa[c]
        order = range(SCAN_SEGS) if d == 0 else range(SCAN_SEGS - 1, -1, -1)
        states, carries = [], []
        for c, cs in enumerate(lane_tiles):
            pieces, cur = [None] * SCAN_SEGS, carry[:, cs]
            for s in order:
                lo = s * SCAN_SEG_PITCH
                pieces[s] = run_ref[d, 0, c, lo:lo + SCAN_SEG] + run_ref[d, 1, c, lo:lo + SCAN_SEG] * cur
                cur = pa[c][s:s + 1] * cur + hl[c][s:s + 1]
            states.append(jnp.concatenate(pieces, axis=0))
            carries.append(cur)
        return jnp.concatenate(states, axis=1), jnp.concatenate(carries, axis=1)

    conv_into_u(xc_ref, ctx_len)
    zero_row = jnp.zeros((1, ct), f32)
    assert ctx_len == SCAN_CHUNK
    stage(0, 0, 0)
    stage(0, 0, 1)
    cf, carry_f = scan(0, 0, zero_row)
    cr, carry_r = scan(0, 1, zero_row)
    yc_ref[0] = ((cf + cr) * gc_ref[0]).astype(bf16)

    conv_into_u(xl_ref, seq)
    nchunk = seq // SCAN_CHUNK
    fwd_start = lambda c: pl.multiple_of(c * SCAN_CHUNK, SCAN_CHUNK)
    rev_start = lambda c: pl.multiple_of((nchunk - 1 - c) * SCAN_CHUNK, SCAN_CHUNK)
    stage(0, fwd_start(0), 0)
    stage(0, rev_start(0), 1)

    assert nchunk % 4 == 0

    def emit(first_visit, r0, states):
        if first_visit:
            hf_ref[pl.ds(r0, SCAN_CHUNK)] = states
        else:
            yl_ref[0, pl.ds(r0, SCAN_CHUNK), :] = (
                (states + hf_ref[pl.ds(r0, SCAN_CHUNK)])
                * gl_ref[0, pl.ds(r0, SCAN_CHUNK), :]).astype(bf16)

    def pair_body(first_visit, it, carries):
        carry_f, carry_r = carries
        for buf in range(2):
            c = 2 * it + buf
            nxt = jnp.minimum(c + 1, nchunk - 1)
            stage(1 - buf, fwd_start(nxt), 0)
            stage(1 - buf, rev_start(nxt), 1)
            out_f, carry_f = scan(buf, 0, carry_f)
            out_r, carry_r = scan(buf, 1, carry_r)
            emit(first_visit, fwd_start(c), out_f)
            emit(first_visit, rev_start(c), out_r)
        return carry_f, carry_r

    carries = lax.fori_loop(0, nchunk // 4, functools.partial(pair_body, True), (carry_f, carry_r))
    lax.fori_loop(nchunk // 4, nchunk // 2, functools.partial(pair_body, False), carries)


def _rglru(xl, gl, xc, gc, conv_w, conv_b, wa_bd, ba, wx_bd, bx, lam, ct):
    bsz, seq, _ = xl.shape
    ctx_len = xc.shape[1]
    seg_rows = SCAN_SEGS * SCAN_SEG_PITCH
    col = lambda rows: pl.BlockSpec((1, rows, ct), lambda b, c: (b, 0, c))
    vec = lambda rows: pl.BlockSpec((rows, ct), lambda b, c: (0, c))
    bd = pl.BlockSpec((2, ct, ct), lambda b, c: (0, c, c))
    return pl.pallas_call(
        functools.partial(_lru_kernel, seq, ctx_len, ct),
        grid=(bsz, LRU_WIDTH // ct),
        in_specs=[col(seq), col(seq), col(ctx_len), col(ctx_len),
                  vec(LRU_CONV), vec(1), bd, vec(2), bd, vec(2), vec(2)],
        out_specs=[col(seq), col(ctx_len)],
        out_shape=[jax.ShapeDtypeStruct((bsz, seq, LRU_WIDTH), bf16),
                   jax.ShapeDtypeStruct((bsz, ctx_len, LRU_WIDTH), bf16)],
        scratch_shapes=[pltpu.VMEM((ct // LANES, seq + 2 * SCAN_PAD, LANES), f32), pltpu.VMEM((seq, ct), f32),
                        pltpu.VMEM((seq, ct), f32), pltpu.VMEM((2, 2, 2, ct // LANES, seg_rows, LANES), f32),
                        pltpu.VMEM((2, 2, ct // LANES, seg_rows, LANES), f32)],
        compiler_params=_params(("arbitrary", "arbitrary")),
        name="rglru",
    )(xl, gl, xc, gc, conv_w, conv_b.reshape(1, LRU_WIDTH), wa_bd, ba, wx_bd, bx, lam)


def _conv_ln_swish(tt, first, last, x_ref, prev_rows, next_rows, w_ref, b_ref, g_ref, beta_ref, win_ref, y_ref):
    lane_tiles = [slice(c * LANES, (c + 1) * LANES) for c in range(CONV_WIDTH // LANES)]
    zero = jnp.zeros((CONV_HALO, LANES), f32)
    for c, cs in enumerate(lane_tiles):
        win_ref[c, CONV_HALO:CONV_HALO + tt] = x_ref[:, cs]

    def fill(lo, rows):
        for c, cs in enumerate(lane_tiles):
            win_ref[c, lo:lo + CONV_HALO] = zero if rows is None else rows(cs)
    pl.when(jnp.logical_not(first))(lambda: fill(0, prev_rows))
    pl.when(first)(lambda: fill(0, None))
    pl.when(jnp.logical_not(last))(lambda: fill(CONV_HALO + tt, next_rows))
    pl.when(last)(lambda: fill(CONV_HALO + tt, None))

    gain, beta = g_ref[...], beta_ref[...]

    def rows(r0):
        pieces = []
        for c, cs in enumerate(lane_tiles):
            acc = jnp.broadcast_to(b_ref[:, cs], (CONV_ROWS, LANES))
            for k in range(CONV_KERNEL):
                off = r0 + CONV_HALO - CONV_PAD + k
                acc = acc + jnp.broadcast_to(w_ref[k:k + 1, cs], (CONV_ROWS, LANES)) * win_ref[c, off:off + CONV_ROWS, :]
            pieces.append(acc)
        acc = jnp.concatenate(pieces, axis=1)
        mu = jnp.mean(acc, axis=-1, keepdims=True)
        xc = acc - mu
        var = jnp.mean(xc * xc, axis=-1, keepdims=True)
        y = xc * lax.rsqrt(var + EPS) * gain + beta
        y = (y * jax.nn.sigmoid(y)).astype(bf16)
        y_ref[r0:r0 + CONV_ROWS, :] = y
        return y
    return rows


def _tail_kernel(final, tiles_per_seq, fixed_row,
                 one_ref, h_ref, ya_ref, hc_ref, hc_next_ref, hc_after_ref, yc_ref, mod_ref, g1_ref, g2_ref, gf_ref,
                 cw_ref, cb_ref, cg_ref, cbeta_ref,
                 wg_ref, woa_ref, wob_ref, woc_ref, wout_ref, wup_ref, wdn_ref, o_ref,
                 act_ref, win_ref, yb_ref, yb_next_ref):
    i = pl.program_id(0)
    tm = h_ref.shape[0]
    conv = functools.partial(_conv_ln_swish, tm, w_ref=cw_ref, b_ref=cb_ref, g_ref=cg_ref, beta_ref=cbeta_ref,
                             win_ref=win_ref, y_ref=yb_next_ref)
    head_of_next = lambda cs: hc_next_ref[0:CONV_HALO, cs]

    @pl.when(i == 0)
    def _():
        rows = conv(True, tiles_per_seq == 1, hc_ref, None, head_of_next)
        for r0 in range(0, tm, CONV_ROWS):
            rows(r0)
    yb_ref[...] = yb_next_ref[...]
    next_in_seq = (i + 1) % tiles_per_seq
    conv_rows = conv(next_in_seq == 0, next_in_seq == tiles_per_seq - 1, hc_next_ref,
                     lambda cs: hc_ref[tm - CONV_HALO:tm, cs], lambda cs: hc_after_ref[:, cs])
    row = fixed_row if fixed_row is not None else i // tiles_per_seq
    mod = lambda n: mod_ref[pl.ds(row, 1), n * D_MODEL:(n + 1) * D_MODEL]
    x = h_ref[...]
    a = _norm_mod(x, g1_ref[...], mod(0), mod(1)).astype(bf16)
    col_blocks = [slice(n0, n0 + OUT_CHUNK) for n0 in range(0, D_MODEL, OUT_CHUNK)]
    nffn = FFN_HIDDEN // FFN_CHUNK
    always = one_ref[0] == 1
    assert OUT_CHUNK == FFN_CHUNK
    anchors = len(col_blocks) + nffn
    nconv = tm // CONV_ROWS

    def tied(val, anchor):
        pieces, done = [], 0
        for rc in range(anchor * nconv // anchors, (anchor + 1) * nconv // anchors):
            r0 = rc * CONV_ROWS
            y = conv_rows(r0)
            pieces += [val[done:r0]] * (r0 > done) + [jnp.where(always, val[r0:r0 + CONV_ROWS], y[:, 0:FFN_CHUNK])]
            done = r0 + CONV_ROWS
        return jnp.concatenate(pieces + [val[done:]] * (done < tm), axis=0) if pieces else val

    branches = ((ya_ref, woa_ref), (yb_ref, wob_ref), (yc_ref, woc_ref))
    merged = []
    for nb, cs in enumerate(col_blocks):
        total = None
        for n, (y_ref, w_ref) in enumerate(branches):
            gate = jax.nn.sigmoid(_mm(a, wg_ref[:, n * D_MODEL + cs.start:n * D_MODEL + cs.stop]))
            term = gate * _mm(y_ref[...], w_ref[:, cs])
            total = term if total is None else total + term
        merged.append(tied(total.astype(bf16), nb))
    h1 = x + mod(2) * _mm(jnp.concatenate(merged, axis=1), wout_ref[...])
    a2 = _norm_mod(h1, g2_ref[...], mod(3), mod(4)).astype(bf16)

    for j, c0 in enumerate(range(0, FFN_HIDDEN, FFN_CHUNK)):
        up = _mm(a2, wup_ref[:, c0:c0 + FFN_CHUNK])
        gate = _mm(a2, wup_ref[:, FFN_HIDDEN + c0:FFN_HIDDEN + c0 + FFN_CHUNK])
        act = ((gate * jax.nn.sigmoid(gate)) * up).astype(bf16)
        act_ref[:, c0:c0 + FFN_CHUNK] = tied(act, len(col_blocks) + j)
    gate2 = mod(5)
    h2 =[h1[:, cs] + gate2[:, cs] * _mm(act_ref[...], wdn_ref[:, cs]) for cs in col_blocks]
    if final:
        sum_sq = functools.reduce(lambda u, v: u + v, [jnp.sum(p * p, axis=-1, keepdims=True) for p in h2])
        inv_rms = lax.rsqrt(sum_sq * (1.0 / D_MODEL) + EPS)
        h2 = [p * inv_rms * gf_ref[:, cs] for p, cs in zip(h2, col_blocks)]
    for p, cs in zip(h2, col_blocks):
        o_ref[:, cs] = p


def _tail(h, ya, hc, yc, mod, g1, g2, gf, conv, wts, layer, tm, tiles_per_seq, fixed_row, final):
    rows = h.shape[0]
    row_spec = lambda width: pl.BlockSpec((tm, width), lambda i: (i, 0))
    ntiles = rows // tm
    hpt = tm // CONV_HALO
    nhalo = rows // CONV_HALO
    wg, woa, wob, woc, wout, wup, wdn = wts
    cw, cb, cg, cbeta = conv
    vec = lambda v: v.reshape(1, -1)
    return pl.pallas_call(
        functools.partial(_tail_kernel, final, tiles_per_seq, fixed_row),
        grid=(rows // tm,),
        in_specs=[pl.BlockSpec(memory_space=pltpu.SMEM),
                  row_spec(D_MODEL), row_spec(ATTN_WIDTH), row_spec(CONV_WIDTH),
                  pl.BlockSpec((tm, CONV_WIDTH), lambda i: (jnp.minimum(i + 1, ntiles - 1), 0)),
                  pl.BlockSpec((CONV_HALO, CONV_WIDTH), lambda i: (jnp.minimum((i + 2) * hpt, nhalo - 1), 0)),
                  row_spec(LRU_WIDTH),
                  _const_spec((MOD_ROWS, 6 * D_MODEL)),
                  _const_spec((1, D_MODEL)), _const_spec((1, D_MODEL)), _const_spec((1, D_MODEL)),
                  _const_spec(cw.shape), _const_spec((1, CONV_WIDTH)), _const_spec((1, CONV_WIDTH)),
                  _const_spec((1, CONV_WIDTH)),
                  ] + [_layer_spec(w, layer) for w in wts],
        out_specs=row_spec(D_MODEL),
        out_shape=jax.ShapeDtypeStruct((rows, D_MODEL), f32),
        scratch_shapes=[pltpu.VMEM((tm, FFN_HIDDEN), bf16),
                        pltpu.VMEM((CONV_WIDTH // LANES, tm + 2 * CONV_HALO, LANES), f32),
                        pltpu.VMEM((tm, CONV_WIDTH), bf16), pltpu.VMEM((tm, CONV_WIDTH), bf16)],
        compiler_params=_params(("arbitrary",)),
        name="merge_out_swiglu",
    )(jnp.ones((1,), jnp.int32), h, ya, hc, hc, hc, yc, mod, g1, g2, gf, cw, vec(cb), vec(cg), vec(cbeta),
      wg, woa, wob, woc, wout, wup, wdn)


def _rope_tables(seq):
    rows = seq // GRID_W
    inv = jnp.power(ROPE_BASE, -jnp.arange(ROPE_FREQS, dtype=f32) / ROPE_FREQS)
    row_ang = jnp.arange(rows, dtype=f32)[:, None] * inv[None]
    col_ang = jnp.arange(GRID_W, dtype=f32)[:, None] * inv[None]

    def table(fn):
        by_row = jnp.broadcast_to(fn(row_ang)[:, None, :], (rows, GRID_W, ROPE_FREQS))
        by_col = jnp.broadcast_to(fn(col_ang)[None, :, :], (rows, GRID_W, ROPE_FREQS))
        return jnp.concatenate([by_row, by_col], axis=-1).reshape(seq, 2 * ROPE_FREQS)
    cos, sin = table(jnp.cos), table(jnp.sin)
    reps = LANES // HEAD_DIM
    return jnp.tile(jnp.concatenate([cos, cos], axis=-1), (1, reps)), \
        jnp.tile(jnp.concatenate([-sin, sin], axis=-1), (1, reps))


def _block_diag(w):
    two, nb, d, e = w.shape
    eye = jnp.eye(nb, dtype=w.dtype)
    return jnp.einsum('xnde,nm->xndme', w, eye).reshape(two, nb * d, nb * e)


def kernel(x, c, ctx, c_ctx, mod_w, mod_b, norm1_g, norm2_g, w_in, attn_sink, conv_dw_w, conv_dw_b, conv_ln_g,
           conv_ln_b, lru_conv_w, lru_conv_b, lru_wa, lru_ba, lru_wx, lru_bx, lru_lam, w_o_attn, w_o_conv,
           w_o_lru, w_out, ffn_w_up, ffn_w_down, final_norm_g):
    bsz, seq, _ = x.shape
    ctx_len = ctx.shape[1]
    depth = mod_w.shape[0]
    assert bsz + 1 <= MOD_ROWS and seq % (ATTN_QBLOCKS * BLOCK) == 0 and ctx_len == SCAN_CHUNK
    tm = 512
    tm_in = 1024
    tiles_per_seq = seq // tm
    tm_ctx = ctx_len
    ctx_row = bsz

    cvec = jnp.zeros((MOD_ROWS, D_MODEL), f32).at[:bsz].set(c).at[ctx_row].set(c_ctx)
    mod_all = _modulation(cvec, mod_w, mod_b)
    cos, sin = _rope_tables(seq)
    row = lambda v: v.reshape(1, -1)

    w_mix = w_in[:, :, :MIX_COLS].astype(bf16)
    wts = (w_in[:, :, MIX_COLS:].astype(bf16), w_o_attn.astype(bf16), w_o_conv.astype(bf16),
           w_o_lru.astype(bf16), w_out.astype(bf16), ffn_w_up.astype(bf16), ffn_w_down.astype(bf16))

    h_lat = x.reshape(bsz * seq, D_MODEL)
    h_ctx = ctx.reshape(bsz * ctx_len, D_MODEL)
    for l in range(depth):
        need_ctx = l < depth - 1
        mod = mod_all[l]
        g1, g2 = row(norm1_g[l]), row(norm2_g[l])

        q, k, v, hc, lx, gl = _inproj(h_lat, mod, g1, w_mix, l, tm_in, seq // tm_in, None, cos, sin)
        qc, kc, vc, hcc, lxc, glc = _inproj(h_ctx, mod, g1, w_mix, l, tm_ctx, 1, ctx_row)
        shp = lambda t, n: t.reshape(bsz, n, t.shape[-1])
        kc3, vc3 = shp(kc, ctx_len), shp(vc, ctx_len)

        y_attn = _attention(attn_sink[l], shp(q, seq), shp(k, seq), shp(v, seq), kc3, vc3)
        conv = (conv_dw_w[l], conv_dw_b[l], conv_ln_g[l], conv_ln_b[l])
        y_lru, y_lru_c = _rglru(shp(lx, seq), shp(gl, seq), shp(lxc, ctx_len), shp(glc, ctx_len),
                                lru_conv_w[l], lru_conv_b[l],
                                _block_diag(0.5 * lru_wa[l]).astype(bf16), 0.5 * lru_ba[l],
                                _block_diag(0.5 * lru_wx[l]).astype(bf16), 0.5 * lru_bx[l], lru_lam[l], LRU_TILE)
        flat = lambda t: t.reshape(-1, t.shape[-1])
        h_lat = _tail(h_lat, flat(y_attn), hc, flat(y_lru), mod, g1, g2, row(final_norm_g), conv, wts, l,
                      tm, tiles_per_seq, None, final=not need_ctx)
        if need_ctx:
            y_attn_c = _attention(attn_sink[l], shp(qc, ctx_len), None, None, kc3, vc3)
            h_ctx = _tail(h_ctx, flat(y_attn_c), hcc, flat(y_lru_c), mod, g1, g2,
                          row(final_norm_g), conv, wts, l, tm_ctx, 1, ctx_row, final=False)
    return h_lat.reshape(bsz, seq, D_MODEL)
```
